```python
import jax, jax.numpy as jnp
from jax import lax
import numpy as np

D_MODEL = 2048
BATCH = 2
SEQ = 8192
DEPTH = 1

D_LRU = 2048
LRU_HEADS = 16
LRU_BLOCK = D_LRU // LRU_HEADS
CONV_WIDTH = 4
LRU_C = 8.0
LRU_A_MIN = 0.9
LRU_A_MAX = 0.999
RET_HEADS = 8
RET_DK = 256
RET_DV = 512
D_QK = RET_HEADS * RET_DK
D_RV = RET_HEADS * RET_DV
RET_CHUNK = 128
ROPE_THETA = 10000.0
N_EXPERTS = 64
TOP_K = 8
N_GROUPS = 8
TOPK_GROUPS = 4
D_EXPERT = 512
D_SHARED = 512
ROUTED_SCALE = 2.5
EXPERT_BLOCK = 128
DN_ALPHA = (2.0 * DEPTH) ** 0.25
DN_BETA = (8.0 * DEPTH) ** -0.25
LN_EPS = 1e-5
PROJ_SIZES = (D_LRU, D_LRU, D_QK, D_QK, D_RV, D_RV, D_MODEL, D_MODEL)
N_PROJ = sum(PROJ_SIZES)

kernel_name = "hybrid_rglru_retention_moe_deepnorm"


def layer_norm(x, g, b):
    xf = x.astype(jnp.float32)
    mu = xf.mean(-1, keepdims=True)
    var = jnp.square(xf - mu).mean(-1, keepdims=True)
    return ((xf - mu) * lax.rsqrt(var + LN_EPS) * g + b).astype(x.dtype)


def causal_conv(x, w, b):
    W = w.shape[0]
    S = x.shape[1]
    xp = jnp.pad(x, ((0, 0), (W - 1, 0), (0, 0)))
    out = b
    for t in range(W):
        out = out + xp[:, t:t + S] * w[t]
    return out


def _lin_combine(c1, c2):
    a1, b1 = c1
    a2, b2 = c2
    return a1 * a2, a2 * b1 + b2


def rg_lru(xa, wa, ba, wi, bi, lam):
    B, S, C = xa.shape
    xf = xa.astype(jnp.float32)
    xh = xf.reshape(B, S, LRU_HEADS, LRU_BLOCK)
    r = jax.nn.sigmoid(jnp.einsum('bshi,hij->bshj', xh, wa.astype(jnp.float32)) + ba).reshape(B, S, C)
    i = jax.nn.sigmoid(jnp.einsum('bshi,hij->bshj', xh, wi.astype(jnp.float32)) + bi).reshape(B, S, C)
    log_a = -LRU_C * r * jax.nn.softplus(-lam.astype(jnp.float32))
    a = jnp.exp(log_a)
    mult = jnp.sqrt(-jnp.expm1(2.0 * log_a))
    first = (jnp.arange(S) == 0)[None, :, None]
    mult = jnp.where(first, 1.0, mult)
    _, h = lax.associative_scan(_lin_combine, (a, mult * i * xf), axis=1)
    return h.astype(xa.dtype)


def rope(t, pos):
    half = t.shape[-1] // 2
    freq = ROPE_THETA ** (-jnp.arange(half, dtype=jnp.float32) / half)
    ang = pos.astype(jnp.float32)[:, None] * freq
    cos = jnp.cos(ang)[None, :, None, :]
    sin = jnp.sin(ang)[None, :, None, :]
    t1 = t[..., :half].astype(jnp.float32)
    t2 = t[..., half:].astype(jnp.float32)
    return jnp.concatenate([t1 * cos - t2 * sin, t1 * sin + t2 * cos], -1)


def chunk_retention(q, k, v):
    B, S, H, dk = q.shape
    dv = v.shape[-1]
    C = RET_CHUNK
    N = S // C
    log_g = jnp.log1p(-jnp.exp2(-5.0 - jnp.arange(H, dtype=jnp.float32)))
    idx = jnp.arange(C, dtype=jnp.float32)
    diff = idx[:, None] - idx[None, :]
    dmask = jnp.where(diff >= 0, jnp.exp(jnp.maximum(diff, 0.0)[None] * log_g[:, None, None]), 0.0)
    xi = jnp.exp((idx[None] + 1.0) * log_g[:, None])
    zeta = jnp.exp((C - 1.0 - idx[None]) * log_g[:, None])
    g_c = jnp.exp(C * log_g)

    def to_chunks(t):
        return t.reshape(B, N, C, H, t.shape[-1]).transpose(1, 0, 3, 2, 4)

    def step(state, blk):
        qc, kc, vc = blk
        s = jnp.einsum('bhid,bhjd->bhij', qc, kc) * dmask
        o = jnp.einsum('bhij,bhje->bhie', s, vc)
        o = o + jnp.einsum('bhid,bhde->bhie', qc, state) * xi[None, :, :, None]
        state = g_c[None, :, None, None] * state + jnp.einsum('bhjd,bhje->bhde', kc * zeta[None, :, :, None], vc)
        return state, o

    state0 = jnp.zeros((B, H, dk, dv), jnp.float32)
    _, o = lax.scan(step, state0, (to_chunks(q), to_chunks(k), to_chunks(v.astype(jnp.float32))))
    return o.transpose(1, 0, 3, 2, 4).reshape(B, S, H, dv)


def head_group_norm(o):
    mu = o.mean(-1, keepdims=True)
    var = jnp.square(o - mu).mean(-1, keepdims=True)
    return (o - mu) * lax.rsqrt(var + LN_EPS)


def moe_ffn(xt, w_router, router_bias, wg, wu, wd, wgs, wus, wds):
    T, D = xt.shape
    E = N_EXPERTS
    f32 = jnp.float32
    scores = jax.nn.sigmoid((xt @ w_router).astype(f32))
    biased = scores + router_bias.astype(f32)
    grp = lax.top_k(biased.reshape(T, N_GROUPS, E // N_GROUPS), 2)[0].sum(-1)
    _, gidx = lax.top_k(grp, TOPK_GROUPS)
    gmask = jax.nn.one_hot(gidx, N_GROUPS, dtype=f32).sum(1) > 0
    emask = jnp.repeat(gmask, E // N_GROUPS, axis=1)
    _, eidx = lax.top_k(jnp.where(emask, biased, -jnp.inf), TOP_K)
    wsel = jnp.take_along_axis(scores, eidx, axis=1)
    wsel = wsel / wsel.sum(-1, keepdims=True) * ROUTED_SCALE

    A = T * TOP_K
    flat_e = eidx.reshape(-1)
    flat_w = wsel.reshape(-1)
    order = jnp.argsort(flat_e)
    sorted_e = flat_e[order]
    sorted_tok = (order // TOP_K).astype(jnp.int32)
    sorted_w = flat_w[order]
    counts = jnp.zeros((E,), jnp.int32).at[flat_e].add(1)
    padded = (counts + EXPERT_BLOCK - 1) // EXPERT_BLOCK * EXPERT_BLOCK
    pad_end = jnp.cumsum(padded)
    pad_start = pad_end - padded
    start = jnp.cumsum(counts) - counts
    dest = pad_start[sorted_e] + jnp.arange(A, dtype=jnp.int32) - start[sorted_e]
    NB = (A + E * (EXPERT_BLOCK - 1) + EXPERT_BLOCK - 1) // EXPERT_BLOCK
    R = NB * EXPERT_BLOCK
    row_tok = jnp.full((R,), T, jnp.int32).at[dest].set(sorted_tok)
    row_w = jnp.zeros((R,), f32).at[dest].set(sorted_w)
    block_e = jnp.minimum(jnp.searchsorted(pad_end, jnp.arange(NB, dtype=jnp.int32) * EXPERT_BLOCK, side='right'), E - 1)
    xpad = jnp.concatenate([xt, jnp.zeros((1, D), xt.dtype)], 0)

    def step(acc, blk):
        tok, wr, e = blk
        xb = xpad[tok]
        hb = jax.nn.silu(xb @ wg[e]) * (xb @ wu[e])
        yb = (hb @ wd[e]).astype(f32) * wr[:, None]
        return acc.at[tok].add(yb), None

    acc, _ = lax.scan(step, jnp.zeros((T + 1, D), f32),
                      (row_tok.reshape(NB, EXPERT_BLOCK), row_w.reshape(NB, EXPERT_BLOCK), block_e))
    routed = acc[:T].astype(xt.dtype)
    shared = (jax.nn.silu(xt @ wgs) * (xt @ wus)) @ wds
    return routed + shared


def setup_inputs(seed: int = 0) -> dict:
    key = jax.random.key(seed)
    ks = jax.random.split(key, 26)
    L, D, E = DEPTH, D_MODEL, N_EXPERTS
    f32 = jnp.float32

    def nrm(k, shape, scale):
        return jax.random.normal(k, shape, f32) * scale

    a8 = jax.random.uniform(ks[9], (L, D_LRU), f32, LRU_A_MIN, LRU_A_MAX)
    a = a8 ** (1.0 / LRU_C)
    return {
        "x": nrm(ks[0], (BATCH, SEQ, D), 1.0),
        "w_in": nrm(ks[1], (L, D, N_PROJ), D ** -0.5),
        "conv_w": nrm(ks[2], (L, CONV_WIDTH, D_LRU), CONV_WIDTH ** -0.5),
        "conv_b": nrm(ks[3], (L, D_LRU), 0.02),
        "lru_wa": nrm(ks[4], (L, LRU_HEADS, LRU_BLOCK, LRU_BLOCK), LRU_BLOCK ** -0.5),
        "lru_ba": nrm(ks[5], (L, LRU_HEADS, LRU_BLOCK), 0.02),
        "lru_wi": nrm(ks[6], (L, LRU_HEADS, LRU_BLOCK, LRU_BLOCK), LRU_BLOCK ** -0.5),
        "lru_bi": nrm(ks[7], (L, LRU_HEADS, LRU_BLOCK), 0.02),
        "lru_lambda": jnp.log(a) - jnp.log1p(-a),
        "ret_gn_gain": 1.0 + nrm(ks[8], (L, D_RV), 0.02),
        "w_lru_out": nrm(ks[10], (L, D_LRU, D), D_LRU ** -0.5 * DN_BETA),
        "w_ret_out": nrm(ks[11], (L, D_RV, D), D_RV ** -0.5 * DN_BETA),
        "b_gate": nrm(ks[12], (L, 2 * D), 0.02),
        "w_o": nrm(ks[13], (L, D, D), D ** -0.5 * DN_BETA),
        "ln1_g": 1.0 + nrm(ks[14], (L, D), 0.02),
        "ln1_b": nrm(ks[15], (L, D), 0.02),
        "w_router": nrm(ks[16], (L, D, E), D ** -0.5),
        "router_bias": nrm(ks[17], (L, E), 0.01),
        "w_gate_e": nrm(ks[18], (L, E, D, D_EXPERT), D ** -0.5),
        "w_up_e": nrm(ks[19], (L, E, D, D_EXPERT), D ** -0.5),
        "w_down_e": nrm(ks[20], (L, E, D_EXPERT, D), D_EXPERT ** -0.5 * DN_BETA),
        "w_gate_s": nrm(ks[21], (L, D, D_SHARED), D ** -0.5),
        "w_up_s": nrm(ks[22], (L, D, D_SHARED), D ** -0.5),
        "w_down_s": nrm(ks[23], (L, D_SHARED, D), D_SHARED ** -0.5 * DN_BETA),
        "ln2_g": 1.0 + nrm(ks[24], (L, D), 0.02),
        "ln2_b": nrm(ks[25], (L, D), 0.02),
    }


def reference(x, w_in, conv_w, conv_b, lru_wa, lru_ba, lru_wi, lru_bi, lru_lambda, ret_gn_gain,
              w_lru_out, w_ret_out, b_gate, w_o, ln1_g, ln1_b, w_router, router_bias,
              w_gate_e, w_up_e, w_down_e, w_gate_s, w_up_s, w_down_s, ln2_g, ln2_b):
    B, S, D = x.shape
    pos = jnp.arange(S, dtype=jnp.int32)
    offs = [0]
    for n in PROJ_SIZES:
        offs.append(offs[-1] + n)
    for l in range(DEPTH):
        proj = x @ w_in[l]
        lru_x, lru_y, q, k, v, g, gl_a, gl_b = [proj[..., offs[j]:offs[j + 1]] for j in range(len(PROJ_SIZES))]

        xa = causal_conv(lru_x, conv_w[l], conv_b[l])
        ha = rg_lru(xa, lru_wa[l], lru_ba[l], lru_wi[l], lru_bi[l], lru_lambda[l])
        ya = (ha * jax.nn.gelu(lru_y, approximate=True)) @ w_lru_out[l]

        qh = rope(q.reshape(B, S, RET_HEADS, RET_DK), pos)
        kh = rope(k.reshape(B, S, RET_HEADS, RET_DK), pos) * (RET_DK ** -0.5)
        vh = v.reshape(B, S, RET_HEADS, RET_DV)
        oh = head_group_norm(chunk_retention(qh, kh, vh)).reshape(B, S, D_RV) * ret_gn_gain[l]
        yb = (jax.nn.silu(g) * oh.astype(x.dtype)) @ w_ret_out[l]

        mixed = jax.nn.sigmoid(gl_a + b_gate[l, :D]) * ya + jax.nn.sigmoid(gl_b + b_gate[l, D:]) * yb
        h = layer_norm(DN_ALPHA * x + mixed @ w_o[l], ln1_g[l], ln1_b[l])

        f = moe_ffn(h.reshape(B * S, D), w_router[l], router_bias[l], w_gate_e[l], w_up_e[l], w_down_e[l],
                    w_gate_s[l], w_up_s[l], w_down_s[l]).reshape(B, S, D)
        x = layer_norm(DN_ALPHA * h + f, ln2_g[l], ln2_b[l])
    return x
```

```python
import functools

import jax
import jax.numpy as jnp
from jax import lax
from jax.experimental import pallas as pl
from jax.experimental.pallas import tpu as pltpu

F32 = jnp.float32
BF16 = jnp.bfloat16
I32 = jnp.int32
U32 = jnp.uint32

D_MODEL = 2048
D_LRU = 2048
LRU_HEADS = 16
LRU_BLOCK = D_LRU // LRU_HEADS
CONV_WIDTH = 4
LRU_C = 8.0
RET_HEADS = 8
RET_DK = 256
RET_DV = 512
D_QK = RET_HEADS * RET_DK
D_RV = RET_HEADS * RET_DV
RET_CHUNK = 128
ROPE_THETA = 10000.0
N_EXPERTS = 64
TOP_K = 8
N_GROUPS = 8
GROUP_SIZE = N_EXPERTS // N_GROUPS
TOPK_GROUPS = 4
D_EXPERT = 512
D_SHARED = 512
ROUTED_SCALE = 2.5
DEPTH = 1
DN_ALPHA = (2.0 * DEPTH) ** 0.25
LN_EPS = 1e-5

OFF_LRU_X = 0
OFF_LRU_Y = OFF_LRU_X + D_LRU
OFF_Q = OFF_LRU_Y + D_LRU
OFF_K = OFF_Q + D_QK
OFF_V = OFF_K + D_QK
OFF_G = OFF_V + D_RV
OFF_GA = OFF_G + D_RV
OFF_GB = OFF_GA + D_MODEL

V7X_VMEM_LIMIT = 56 * 1024 * 1024
MOE_ROWS = 256
HALF = D_MODEL // 2


def _params(sem):
    return pltpu.CompilerParams(dimension_semantics=sem, vmem_limit_bytes=V7X_VMEM_LIMIT)


def _dot(a, b):
    return jnp.dot(a, b, preferred_element_type=F32)


def _mm_kernel(x_ref, w_ref, o_ref):
    o_ref[...] = _dot(x_ref[...], w_ref[...]).astype(o_ref.dtype)


def _proj_plain(xb, w, col0, ncols, out_dtype, tm, tn, name):
    T, K = xb.shape
    jb0 = col0 // tn
    return pl.pallas_call(
        _mm_kernel,
        grid=(ncols // tn, T // tm),
        in_specs=[pl.BlockSpec((tm, K), lambda j, i: (i, 0)),
                  pl.BlockSpec((K, tn), lambda j, i: (0, jb0 + j))],
        out_specs=pl.BlockSpec((tm, tn), lambda j, i: (i, j)),
        out_shape=jax.ShapeDtypeStruct((T, ncols), out_dtype),
        compiler_params=_params(("parallel", "parallel")),
        name=name,
    )(xb, w)


def _qk_kernel(x_ref, w_ref, cos_ref, sin_ref, o_ref, *, heads_per_tile, k_tile0):
    j = pl.program_id(0)
    acc = _dot(x_ref[...], w_ref[...])
    scale = jnp.where(j >= k_tile0, RET_DK ** -0.5, 1.0).astype(F32)
    cos = cos_ref[...]
    sin = sin_ref[...]
    half = RET_DK // 2
    for hh in range(heads_per_tile):
        c0 = hh * RET_DK
        t1 = acc[:, c0:c0 + half]
        t2 = acc[:, c0 + half:c0 + RET_DK]
        o_ref[:, c0:c0 + half] = ((t1 * cos - t2 * sin) * scale).astype(o_ref.dtype)
        o_ref[:, c0 + half:c0 + RET_DK] = ((t1 * sin + t2 * cos) * scale).astype(o_ref.dtype)


def _proj_qk(xb, w, cos, sin, seq, tm, tn):
    T, K = xb.shape
    ncols = 2 * D_QK
    jb0 = OFF_Q // tn
    n_seq_tiles = seq // tm
    kern = functools.partial(_qk_kernel, heads_per_tile=tn // RET_DK, k_tile0=D_QK // tn)
    return pl.pallas_call(
        kern,
        grid=(ncols // tn, T // tm),
        in_specs=[pl.BlockSpec((tm, K), lambda j, i: (i, 0)),
                  pl.BlockSpec((K, tn), lambda j, i: (0, jb0 + j)),
                  pl.BlockSpec((tm, RET_DK // 2), lambda j, i: (i % n_seq_tiles, 0)),
                  pl.BlockSpec((tm, RET_DK // 2), lambda j, i: (i % n_seq_tiles, 0))],
        out_specs=pl.BlockSpec((tm, tn), lambda j, i: (i, j)),
        out_shape=jax.ShapeDtypeStruct((T, ncols), BF16),
        compiler_params=_params(("parallel", "parallel")),
        name="qk_rope",
    )(xb, w, cos, sin)


def _lru_kernel(x_ref, wx_ref, wy_ref, cw_ref, cb_ref, wa_ref, ba_ref, wi_ref, bi_ref, lam_ref,
                o_ref, prev_ref, hc_ref, *, tiles_per_seq, tm, width):
    i = pl.program_id(1)
    first = (i % tiles_per_seq) == 0

    @pl.when(first)
    def _():
        prev_ref[...] = jnp.zeros_like(prev_ref)
        hc_ref[...] = jnp.zeros_like(hc_ref)

    xb = x_ref[...]
    px = _dot(xb, wx_ref[...])
    py = _dot(xb, wy_ref[...])

    row = lax.broadcasted_iota(I32, (tm, width), 0)
    prev = prev_ref[...]
    cw = cw_ref[...]
    xa = cb_ref[...] + cw[CONV_WIDTH - 1:CONV_WIDTH] * px
    for s in range(1, CONV_WIDTH):
        shifted = jnp.where(row >= s, pltpu.roll(px, s, 0), pltpu.roll(prev, s, 0))
        xa = xa + cw[CONV_WIDTH - 1 - s:CONV_WIDTH - s] * shifted
    prev_ref[...] = px

    xab = xa.astype(BF16)
    r_parts, i_parts = [], []
    for hh in range(width // LRU_BLOCK):
        xh = xab[:, hh * LRU_BLOCK:(hh + 1) * LRU_BLOCK]
        r_parts.append(_dot(xh, wa_ref[hh]))
        i_parts.append(_dot(xh, wi_ref[hh]))
    rg = jax.nn.sigmoid(jnp.concatenate(r_parts, axis=1) + ba_ref[...])
    ig = jax.nn.sigmoid(jnp.concatenate(i_parts, axis=1) + bi_ref[...])

    log_a = -LRU_C * rg * jax.nn.softplus(-lam_ref[...])
    a = jnp.exp(log_a)
    mult = jnp.sqrt(-jnp.tanh(log_a) * (a * a + 1.0))
    mult = jnp.where(jnp.logical_and(first, row == 0), 1.0, mult)
    b = mult * ig * xa

    d = 1
    while d < tm:
        keep = row >= d
        a_sh = jnp.where(keep, pltpu.roll(a, d, 0), 1.0)
        b_sh = jnp.where(keep, pltpu.roll(b, d, 0), 0.0)
        b = a * b_sh + b
        a = a * a_sh
        d *= 2
    h = b + a * hc_ref[...]
    hc_ref[...] = h[tm - 1:tm, :]

    o_ref[...] = (h * jax.nn.gelu(py, approximate=True)).astype(o_ref.dtype)


def _lru_branch(xb, w, conv_w, conv_b, wa, ba, wi, bi, lam, seq, tm):
    T, K = xb.shape
    width = 2 * LRU_BLOCK
    nj = D_LRU // width
    jy0 = OFF_LRU_Y // width
    hp = width // LRU_BLOCK
    kern = functools.partial(_lru_kernel, tiles_per_seq=seq // tm, tm=tm, width=width)
    vec = lambda j, i: (0, j)
    return pl.pallas_call(
        kern,
        grid=(nj, T // tm),
        in_specs=[pl.BlockSpec((tm, K), lambda j, i: (i, 0)),
                  pl.BlockSpec((K, width), lambda j, i: (0, j)),
                  pl.BlockSpec((K, width), lambda j, i: (0, jy0 + j)),
                  pl.BlockSpec((CONV_WIDTH, width), vec),
                  pl.BlockSpec((1, width), vec),
                  pl.BlockSpec((hp, LRU_BLOCK, LRU_BLOCK), lambda j, i: (j, 0, 0)),
                  pl.BlockSpec((1, width), vec),
                  pl.BlockSpec((hp, LRU_BLOCK, LRU_BLOCK), lambda j, i: (j, 0, 0)),
                  pl.BlockSpec((1, width), vec),
                  pl.BlockSpec((1, width), vec)],
        out_specs=pl.BlockSpec((tm, width), lambda j, i: (i, j)),
        out_shape=jax.ShapeDtypeStruct((T, D_LRU), BF16),
        scratch_shapes=[pltpu.VMEM((tm, width), F32), pltpu.VMEM((1, width), F32)],
        compiler_params=_params(("parallel", "arbitrary")),
        name="lru_branch",
    )(xb, w, w, conv_w, conv_b.reshape(1, D_LRU), wa, ba.reshape(1, D_LRU), wi, bi.reshape(1, D_LRU),
      lam.reshape(1, D_LRU))


def _ret_kernel(gc_ref, q_ref, k_ref, v_ref, g_ref, gain_ref, dm_ref, xi_ref, zeta_ref, o_ref, st_ref):
    hd = pl.program_id(1)
    n = pl.program_id(2)

    @pl.when(n == 0)
    def _():
        st_ref[...] = jnp.zeros_like(st_ref)

    q = q_ref[...]
    k = k_ref[...]
    v = v_ref[...]
    st = st_ref[...]
    s = lax.dot_general(q, k, (((1,), (1,)), ((), ())), preferred_element_type=F32) * dm_ref[0]
    o = _dot(s.astype(BF16), v) + _dot(q, st.astype(BF16)) * xi_ref[0]
    kz = (k.astype(F32) * zeta_ref[0]).astype(BF16)
    st_ref[...] = gc_ref[hd] * st + lax.dot_general(kz, v, (((0,), (0,)), ((), ())),
                                                    preferred_element_type=F32)
    mu = jnp.mean(o, axis=-1, keepdims=True)
    oc = o - mu
    var = jnp.mean(oc * oc, axis=-1, keepdims=True)
    oh = oc * lax.rsqrt(var + LN_EPS) * gain_ref[...]
    o_ref[...] = (jax.nn.silu(g_ref[...]) * oh).astype(o_ref.dtype)


def _retention(qk, v, g, gain, batch, seq):
    T = batch * seq
    C = RET_CHUNK
    nchunk = seq // C
    log_g = jnp.log1p(-jnp.exp2(-5.0 - jnp.arange(RET_HEADS, dtype=F32)))
    idx = jnp.arange(C, dtype=F32)
    diff = idx[:, None] - idx[None, :]
    dmask = jnp.where(diff >= 0, jnp.exp(jnp.maximum(diff, 0.0)[None] * log_g[:, None, None]), 0.0)
    xi = jnp.exp((idx[None] + 1.0) * log_g[:, None])[:, :, None]
    zeta = jnp.exp((C - 1.0 - idx[None]) * log_g[:, None])[:, :, None]
    g_c = jnp.exp(C * log_g)
    rows = lambda b, h, n: b * nchunk + n
    return pl.pallas_call(
        _ret_kernel,
        grid=(batch, RET_HEADS, nchunk),
        in_specs=[pl.BlockSpec(memory_space=pltpu.SMEM),
                  pl.BlockSpec((C, RET_DK), lambda b, h, n: (rows(b, h, n), h)),
                  pl.BlockSpec((C, RET_DK), lambda b, h, n: (rows(b, h, n), RET_HEADS + h)),
                  pl.BlockSpec((C, RET_DV), lambda b, h, n: (rows(b, h, n), h)),
                  pl.BlockSpec((C, RET_DV), lambda b, h, n: (rows(b, h, n), h)),
                  pl.BlockSpec((1, RET_DV), lambda b, h, n: (0, h)),
                  pl.BlockSpec((1, C, C), lambda b, h, n: (h, 0, 0)),
                  pl.BlockSpec((1, C, 1), lambda b, h, n: (h, 0, 0)),
                  pl.BlockSpec((1, C, 1), lambda b, h, n: (h, 0, 0))],
        out_specs=pl.BlockSpec((C, RET_DV), lambda b, h, n: (rows(b, h, n), h)),
        out_shape=jax.ShapeDtypeStruct((T, D_RV), BF16),
        scratch_shapes=[pltpu.VMEM((RET_DK, RET_DV), F32)],
        compiler_params=_params(("parallel", "parallel", "arbitrary")),
        name="retention",
    )(g_c, qk, qk, v, g, gain.reshape(1, D_RV), dmask, xi, zeta)


def _mix_kernel(ua_ref, ub_ref, x_ref, wlo_ref, wro_ref, wga_ref, wgb_ref, bga_ref, bgb_ref, o_ref):
    x = x_ref[...]
    ya = _dot(ua_ref[...], wlo_ref[...])
    yb = _dot(ub_ref[...], wro_ref[...])
    ga = jax.nn.sigmoid(_dot(x, wga_ref[...]) + bga_ref[...])
    gb = jax.nn.sigmoid(_dot(x, wgb_ref[...]) + bgb_ref[...])
    o_ref[...] = (ga * ya + gb * yb).astype(o_ref.dtype)


def _mix(ua, ub, xb, w_lru_out, w_ret_out, w_in, b_gate, tm, tn):
    T = xb.shape[0]
    ja0 = OFF_GA // tn
    jb0 = OFF_GB // tn
    nj = D_MODEL // tn
    return pl.pallas_call(
        _mix_kernel,
        grid=(T // tm, nj),
        in_specs=[pl.BlockSpec((tm, D_LRU), lambda i, j: (i, 0)),
                  pl.BlockSpec((tm, D_RV), lambda i, j: (i, 0)),
                  pl.BlockSpec((tm, D_MODEL), lambda i, j: (i, 0)),
                  pl.BlockSpec((D_LRU, tn), lambda i, j: (0, j)),
                  pl.BlockSpec((D_RV, tn), lambda i, j: (0, j)),
                  pl.BlockSpec((D_MODEL, tn), lambda i, j: (0, ja0 + j)),
                  pl.BlockSpec((D_MODEL, tn), lambda i, j: (0, jb0 + j)),
                  pl.BlockSpec((1, tn), lambda i, j: (0, j)),
                  pl.BlockSpec((1, tn), lambda i, j: (0, nj + j))],
        out_specs=pl.BlockSpec((tm, tn), lambda i, j: (i, j)),
        out_shape=jax.ShapeDtypeStruct((T, D_MODEL), BF16),
        compiler_params=_params(("parallel", "parallel")),
        name="gated_mix",
    )(ua, ub, xb, w_lru_out, w_ret_out, w_in, w_in, b_gate.reshape(1, 2 * D_MODEL),
      b_gate.reshape(1, 2 * D_MODEL))


def _layer_norm(y, g, b):
    mu = jnp.mean(y, axis=-1, keepdims=True)
    yc = y - mu
    var = jnp.mean(yc * yc, axis=-1, keepdims=True)
    return yc * lax.rsqrt(var + LN_EPS) * g + b


def _pack_bf16_pairs(hb):
    lo = lax.bitcast_convert_type(hb[:, :HALF].astype(F32), U32)
    hi = lax.bitcast_convert_type(hb[:, HALF:].astype(F32), U32)
    return (hi & jnp.uint32(0xFFFF0000)) | (lo >> 16)


def _unpack_bf16_pairs(p):
    lo = lax.bitcast_convert_type(p << 16, F32).astype(BF16)
    hi = lax.bitcast_convert_type(p & jnp.uint32(0xFFFF0000), F32).astype(BF16)
    return lo, hi


def _max2(a):
    return jnp.max(jnp.max(a, axis=1, keepdims=True), axis=0, keepdims=True)


def _min2(a):
    return jnp.min(jnp.min(a, axis=1, keepdims=True), axis=0, keepdims=True)


def _sum2(a):
    return jnp.sum(jnp.sum(a, axis=1, keepdims=True), axis=0, keepdims=True)


def _ln1_route_kernel(mix_ref, wo_ref, x_ref, g_ref, b_ref, wr_ref, rb_ref,
                      h_ref, hp_ref, eidx_ref, rank_ref, wsel_ref, cnt_ref, cnt_sc, *, tm):
    i = pl.program_id(0)

    @pl.when(i == 0)
    def _():
        cnt_sc[...] = jnp.zeros_like(cnt_sc)

    y = DN_ALPHA * x_ref[...] + _dot(mix_ref[...], wo_ref[...])
    h = _layer_norm(y, g_ref[...], b_ref[...])
    h_ref[...] = h
    hb = h.astype(BF16)
    hp_ref[...] = _pack_bf16_pairs(hb)

    G, GS = N_GROUPS, GROUP_SIZE
    logits = lax.dot_general(wr_ref[...], hb, (((1,), (1,)), ((), ())), preferred_element_type=F32)
    scores = jax.nn.sigmoid(logits)
    s3 = scores.reshape(G, GS, tm)
    b3 = (scores + rb_ref[...]).reshape(G, GS, tm)
    neg = jnp.float32(-jnp.inf)

    in_grp = lax.broadcasted_iota(I32, (G, GS, tm), 1)
    m1 = jnp.max(b3, axis=1, keepdims=True)
    f1 = jnp.min(jnp.where(b3 == m1, in_grp, GS), axis=1, keepdims=True)
    m2 = jnp.max(jnp.where(in_grp == f1, neg, b3), axis=1, keepdims=True)
    grp = m1 + m2

    gi = lax.broadcasted_iota(I32, (G, 1, tm), 0)
    gkeep = jnp.zeros((G, 1, tm), F32)
    cur = grp
    for _ in range(TOPK_GROUPS):
        m = jnp.max(cur, axis=0, keepdims=True)
        f = jnp.min(jnp.where(cur == m, gi, G), axis=0, keepdims=True)
        hit = gi == f
        gkeep = jnp.where(hit, 1.0, gkeep)
        cur = jnp.where(hit, neg, cur)
    emask = jnp.broadcast_to(gkeep, (G, GS, tm)) > 0.0

    ei = lax.broadcasted_iota(I32, (G, GS, tm), 0) * GS + in_grp
    cur = jnp.where(emask, b3, neg)
    hits, idxs, ws = [], [], []
    for _ in range(TOP_K):
        m = _max2(cur)
        f = _min2(jnp.where(cur == m, ei, N_EXPERTS))
        hit = ei == f
        cur = jnp.where(hit, neg, cur)
        hits.append(hit)
        idxs.append(f)
        ws.append(_sum2(jnp.where(hit, s3, 0.0)))
    wsum = ws[0]
    for r in range(1, TOP_K):
        wsum = wsum + ws[r]

    sel = jnp.zeros((G, GS, tm), F32)
    for hit in hits:
        sel = jnp.where(hit, 1.0, sel)
    sel2 = sel.reshape(N_EXPERTS, tm)
    tri = (lax.broadcasted_iota(I32, (tm, tm), 0) < lax.broadcasted_iota(I32, (tm, tm), 1))
    rank_local = _dot(sel2.astype(BF16), jnp.where(tri, 1.0, 0.0).astype(BF16))
    cnt = cnt_sc[...]
    rank3 = (rank_local + cnt[:, 0:1]).reshape(G, GS, tm)
    cnt_new = cnt + jnp.sum(sel2, axis=1, keepdims=True)
    cnt_sc[...] = cnt_new
    cnt_ref[...] = cnt_new

    for r in range(TOP_K):
        eidx_ref[r:r + 1, :] = idxs[r].reshape(1, tm)
        rank_ref[r:r + 1, :] = _sum2(jnp.where(hits[r], rank3, 0.0)).reshape(1, tm).astype(I32)
        wsel_ref[r:r + 1, :] = (ws[r] / wsum * ROUTED_SCALE).reshape(1, tm)


def _ln1_route(mixed, w_o, x2, ln_g, ln_b, w_router_t, router_bias, tm):
    T = x2.shape[0]
    kern = functools.partial(_ln1_route_kernel, tm=tm)
    row = lambda i: (i, 0)
    fixed = lambda i: (0, 0)
    col = lambda i: (0, i)
    return pl.pallas_call(
        kern,
        grid=(T // tm,),
        in_specs=[pl.BlockSpec((tm, D_MODEL), row),
                  pl.BlockSpec((D_MODEL, D_MODEL), fixed),
                  pl.BlockSpec((tm, D_MODEL), row),
                  pl.BlockSpec((1, D_MODEL), fixed),
                  pl.BlockSpec((1, D_MODEL), fixed),
                  pl.BlockSpec((N_EXPERTS, D_MODEL), fixed),
                  pl.BlockSpec((N_EXPERTS, 1), fixed)],
        out_specs=[pl.BlockSpec((tm, D_MODEL), row),
                   pl.BlockSpec((tm, HALF), row),
                   pl.BlockSpec((TOP_K, tm), col),
                   pl.BlockSpec((TOP_K, tm), col),
                   pl.BlockSpec((TOP_K, tm), col),
                   pl.BlockSpec((N_EXPERTS, 128), fixed)],
        out_shape=[jax.ShapeDtypeStruct((T, D_MODEL), F32),
                   jax.ShapeDtypeStruct((T, HALF), U32),
                   jax.ShapeDtypeStruct((TOP_K, T), I32),
                   jax.ShapeDtypeStruct((TOP_K, T), I32),
                   jax.ShapeDtypeStruct((TOP_K, T), F32),
                   jax.ShapeDtypeStruct((N_EXPERTS, 128), F32)],
        scratch_shapes=[pltpu.VMEM((N_EXPERTS, 128), F32)],
        compiler_params=_params(("arbitrary",)),
        name="ln1_route",
    )(mixed, w_o, x2, ln_g.reshape(1, D_MODEL), ln_b.reshape(1, D_MODEL), w_router_t,
      router_bias.reshape(N_EXPERTS, 1))


def _row_copy(src_ref, src_row, dst_ref, dst_row, sem):
    return pltpu.make_async_copy(src_ref.at[pl.ds(src_row, 1)], dst_ref.at[pl.ds(dst_row, 1)], sem)


def _dispatch_kernel(dest_ref, hp_ref, xs_in_ref, xs_ref, sem, *, tm):
    del xs_in_ref
    base = pl.program_id(0) * tm

    def issue(t, c):
        for k in range(TOP_K):
            _row_copy(hp_ref, base + t, xs_ref, dest_ref[0, k, t], sem).start()
        return c

    def drain(t, c):
        for k in range(TOP_K):
            _row_copy(hp_ref, 0, xs_ref, 0, sem).wait()
        return c

    lax.fori_loop(0, tm, issue, 0)
    lax.fori_loop(0, tm, drain, 0)


def _dispatch(dest3, hp, n_rows, tm):
    T = hp.shape[0]
    kern = functools.partial(_dispatch_kernel, tm=tm)
    xs0 = jnp.zeros((n_rows, HALF), U32)
    return pl.pallas_call(
        kern,
        grid=(T // tm,),
        in_specs=[pl.BlockSpec((1, TOP_K, tm), lambda i: (i, 0, 0), memory_space=pltpu.SMEM),
                  pl.BlockSpec(memory_space=pl.ANY),
                  pl.BlockSpec(memory_space=pl.ANY)],
        out_specs=pl.BlockSpec(memory_space=pl.ANY),
        out_shape=jax.ShapeDtypeStruct((n_rows, HALF), U32),
        scratch_shapes=[pltpu.SemaphoreType.DMA],
        input_output_aliases={2: 0},
        compiler_params=_params(("arbitrary",)),
        name="moe_dispatch",
    )(dest3, hp, xs0)


def _experts_kernel(be_ref, nb_ref, xs_ref, wg_ref, wu_ref, wd_ref, ys_ref):
    del be_ref

    @pl.when(pl.program_id(0) < nb_ref[0])
    def _():
        lo, hi = _unpack_bf16_pairs(xs_ref[...])
        gate = _dot(lo, wg_ref[0, :HALF, :]) + _dot(hi, wg_ref[0, HALF:, :])
        up = _dot(lo, wu_ref[0, :HALF, :]) + _dot(hi, wu_ref[0, HALF:, :])
        hb = (jax.nn.silu(gate) * up).astype(BF16)
        ys_ref[...] = _dot(hb, wd_ref[0])


def _experts(block_e, nb_used, xs, wg, wu, wd):
    n_rows = xs.shape[0]
    nb = n_rows // MOE_ROWS
    rows = lambda b, be, nbu: (jnp.minimum(b, nbu[0] - 1), 0)
    wsel = lambda b, be, nbu: (be[b], 0, 0)
    grid_spec = pltpu.PrefetchScalarGridSpec(
        num_scalar_prefetch=2,
        grid=(nb,),
        in_specs=[pl.BlockSpec((MOE_ROWS, HALF), rows),
                  pl.BlockSpec((1, D_MODEL, D_EXPERT), wsel),
                  pl.BlockSpec((1, D_MODEL, D_EXPERT), wsel),
                  pl.BlockSpec((1, D_EXPERT, D_MODEL), wsel)],
        out_specs=pl.BlockSpec((MOE_ROWS, D_MODEL), rows),
    )
    return pl.pallas_call(
        _experts_kernel,
        grid_spec=grid_spec,
        out_shape=jax.ShapeDtypeStruct((n_rows, D_MODEL), F32),
        compiler_params=_params(("arbitrary",)),
        name="moe_experts",
    )(block_e, nb_used, xs, wg, wu, wd)


def _combine_kernel(dest_ref, w_ref, h_ref, wgs_ref, wus_ref, wds_ref, g_ref, b_ref, ys_ref,
                    o_ref, buf_ref, sem, *, tm):
    def issue(t, c):
        for k in range(TOP_K):
            _row_copy(ys_ref, dest_ref[0, k, t], buf_ref.at[k], t, sem).start()
        return c

    def drain(t, c):
        for k in range(TOP_K):
            _row_copy(ys_ref, 0, buf_ref.at[k], 0, sem).wait()
        return c

    lax.fori_loop(0, tm, issue, 0)

    h = h_ref[...]
    hb = h.astype(BF16)
    hid = (jax.nn.silu(_dot(hb, wgs_ref[...])) * _dot(hb, wus_ref[...])).astype(BF16)
    shared = _dot(hid, wds_ref[...])

    lax.fori_loop(0, tm, drain, 0)

    w = w_ref[...]
    routed = buf_ref[0] * w[:, 0:1]
    for k in range(1, TOP_K):
        routed = routed + buf_ref[k] * w[:, k:k + 1]
    o_ref[...] = _layer_norm(DN_ALPHA * h + (routed + shared), g_ref[...], b_ref[...])


def _combine(dest3, w_tok, h, wgs, wus, wds, ln_g, ln_b, ys, tm):
    T = h.shape[0]
    kern = functools.partial(_combine_kernel, tm=tm)
    row = lambda i: (i, 0)
    fixed = lambda i: (0, 0)
    return pl.pallas_call(
        kern,
        grid=(T // tm,),
        in_specs=[pl.BlockSpec((1, TOP_K, tm), lambda i: (i, 0, 0), memory_space=pltpu.SMEM),
                  pl.BlockSpec((tm, TOP_K), row),
                  pl.BlockSpec((tm, D_MODEL), row),
                  pl.BlockSpec((D_MODEL, D_SHARED), fixed),
                  pl.BlockSpec((D_MODEL, D_SHARED), fixed),
                  pl.BlockSpec((D_SHARED, D_MODEL), fixed),
                  pl.BlockSpec((1, D_MODEL), fixed),
                  pl.BlockSpec((1, D_MODEL), fixed),
                  pl.BlockSpec(memory_space=pl.ANY)],
        out_specs=pl.BlockSpec((tm, D_MODEL), row),
        out_shape=jax.ShapeDtypeStruct((T, D_MODEL), F32),
        scratch_shapes=[pltpu.VMEM((TOP_K, tm, D_MODEL), F32), pltpu.SemaphoreType.DMA],
        compiler_params=_params(("arbitrary",)),
        name="moe_combine",
    )(dest3, w_tok, h, wgs, wus, wds, ln_g.reshape(1, D_MODEL), ln_b.reshape(1, D_MODEL), ys)


def _tile(n, pref):
    t = min(n, pref)
    assert n % t == 0, (n, pref)
    return t


def _layer(x, w_in, conv_w, conv_b, lru_wa, lru_ba, lru_wi, lru_bi, lru_lambda, ret_gn_gain,
           w_lru_out, w_ret_out, b_gate, w_o, ln1_g, ln1_b, w_router, router_bias,
           w_gate_e, w_up_e, w_down_e, w_gate_s, w_up_s, w_down_s, ln2_g, ln2_b):
    B, S, D = x.shape
    T = B * S
    x2 = x.reshape(T, D)
    xb = x2.astype(BF16)
    w_in_b = w_in.astype(BF16)

    half = RET_DK // 2
    freq = ROPE_THETA ** (-jnp.arange(half, dtype=F32) / half)
    ang = jnp.arange(S, dtype=I32).astype(F32)[:, None] * freq
    cos, sin = jnp.cos(ang), jnp.sin(ang)

    tm_big = _tile(S, 1024)
    qk = _proj_qk(xb, w_in_b, cos, sin, S, tm_big, 1024)
    v = _proj_plain(xb, w_in_b, OFF_V, D_RV, BF16, tm_big, 1024, "v_proj")
    g = _proj_plain(xb, w_in_b, OFF_G, D_RV, F32, tm_big, 1024, "g_proj")

    ua = _lru_branch(xb, w_in_b, conv_w, conv_b, lru_wa.astype(BF16), lru_ba, lru_wi.astype(BF16),
                     lru_bi, lru_lambda, S, _tile(S, 256))
    ub = _retention(qk, v, g, ret_gn_gain, B, S)

    mixed = _mix(ua, ub, xb, w_lru_out.astype(BF16), w_ret_out.astype(BF16), w_in_b, b_gate,
                 _tile(T, 512), 512)

    tm_r = _tile(T, 256)
    h, hp, eidx, rank, wsel, cnt = _ln1_route(mixed, w_o.astype(BF16), x2, ln1_g, ln1_b,
                                              w_router.T.astype(BF16), router_bias, tm_r)

    counts = cnt[:, 0].astype(I32)
    padded = (counts + MOE_ROWS - 1) // MOE_ROWS * MOE_ROWS
    pad_end = jnp.cumsum(padded)
    pad_start = pad_end - padded
    nb = (T * TOP_K + N_EXPERTS * (MOE_ROWS - 1) + MOE_ROWS - 1) // MOE_ROWS
    n_rows = nb * MOE_ROWS
    block_e = jnp.minimum(
        jnp.searchsorted(pad_end, jnp.arange(nb, dtype=I32) * MOE_ROWS, side='right'),
        N_EXPERTS - 1).astype(I32)
    nb_used = (pad_end[-1:] // MOE_ROWS).astype(I32)
    dest = pad_start[eidx] + rank

    tm_d = _tile(T, 256)
    dest_d = dest.reshape(TOP_K, T // tm_d, tm_d).transpose(1, 0, 2)
    xs = _dispatch(dest_d, hp, n_rows, tm_d)
    ys = _experts(block_e, nb_used, xs, w_gate_e.astype(BF16), w_up_e.astype(BF16),
                  w_down_e.astype(BF16))

    tm_c = _tile(T, 128)
    dest_c = dest.reshape(TOP_K, T // tm_c, tm_c).transpose(1, 0, 2)
    out = _combine(dest_c, wsel.T, h, w_gate_s.astype(BF16), w_up_s.astype(BF16),
                   w_down_s.astype(BF16), ln2_g, ln2_b, ys, tm_c)
    return out.reshape(B, S, D)


def kernel(x, w_in, conv_w, conv_b, lru_wa, lru_ba, lru_wi, lru_bi, lru_lambda, ret_gn_gain, w_lru_out, w_ret_out, b_gate, w_o, ln1_g, ln1_b, w_router, router_bias, w_gate_e, w_up_e, w_down_e, w_gate_s, w_up_s, w_down_s, ln2_g, ln2_b):
    assert DEPTH == 1 and w_in.shape[0] == DEPTH
    args = (w_in, conv_w, conv_b, lru_wa, lru_ba, lru_wi, lru_bi, lru_lambda, ret_gn_gain,
            w_lru_out, w_ret_out, b_gate, w_o, ln1_g, ln1_b, w_router, router_bias,
            w_gate_e, w_up_e, w_down_e, w_gate_s, w_up_s, w_down_s, ln2_g, ln2_b)
    return _layer(x, *[a[0] for a in args])
```

```python
import functools

import jax
import jax.numpy as jnp
from jax import lax
from jax.experimental import pallas as pl
from jax.experimental.pallas import tpu as pltpu

F32 = jnp.float32
BF16 = jnp.bfloat16
I32 = jnp.int32
U32 = jnp.uint32

D_MODEL = 2048
D_LRU = 2048
LRU_HEADS = 16
LRU_BLOCK = D_LRU // LRU_HEADS
CONV_WIDTH = 4
LRU_C = 8.0
RET_HEADS = 8
RET_DK = 256
RET_DV = 512
D_QK = RET_HEADS * RET_DK
D_RV = RET_HEADS * RET_DV
RET_CHUNK = 128
ROPE_THETA = 10000.0
N_EXPERTS = 64
TOP_K = 8
N_GROUPS = 8
GROUP_SIZE = N_EXPERTS // N_GROUPS
TOPK_GROUPS = 4
D_EXPERT = 512
D_SHARED = 512
ROUTED_SCALE = 2.5
DEPTH = 1
DN_ALPHA = (2.0 * DEPTH) ** 0.25
LN_EPS = 1e-5

OFF_LRU_X = 0
OFF_LRU_Y = OFF_LRU_X + D_LRU
OFF_Q = OFF_LRU_Y + D_LRU
OFF_K = OFF_Q + D_QK
OFF_V = OFF_K + D_QK
OFF_G = OFF_V + D_RV
OFF_GA = OFF_G + D_RV
OFF_GB = OFF_GA + D_MODEL

V7X_VMEM_LIMIT = 56 * 1024 * 1024
MOE_ROWS = 256
HALF = D_MODEL // 2


def _params(sem):
    return pltpu.CompilerParams(dimension_semantics=sem, vmem_limit_bytes=V7X_VMEM_LIMIT)


def _dot(a, b):
    return jnp.dot(a, b, preferred_element_type=F32)


def _mm_kernel(x_ref, w_ref, o_ref):
    o_ref[...] = _dot(x_ref[...], w_ref[...]).astype(o_ref.dtype)


def _proj_plain(xb, w, col0, ncols, out_dtype, tm, tn, name):
    T, K = xb.shape
    jb0 = col0 // tn
    return pl.pallas_call(
        _mm_kernel,
        grid=(ncols // tn, T // tm),
        in_specs=[pl.BlockSpec((tm, K), lambda j, i: (i, 0)),
                  pl.BlockSpec((K, tn), lambda j, i: (0, jb0 + j))],
        out_specs=pl.BlockSpec((tm, tn), lambda j, i: (i, j)),
        out_shape=jax.ShapeDtypeStruct((T, ncols), out_dtype),
        compiler_params=_params(("parallel", "parallel")),
        name=name,
    )(xb, w)


def _qk_kernel(x_ref, w_ref, cos_ref, sin_ref, o_ref, *, heads_per_tile, k_tile0):
    j = pl.program_id(0)
    acc = _dot(x_ref[...], w_ref[...])
    scale = jnp.where(j >= k_tile0, RET_DK ** -0.5, 1.0).astype(F32)
    cos = cos_ref[...]
    sin = sin_ref[...]
    half = RET_DK // 2
    for hh in range(heads_per_tile):
        c0 = hh * RET_DK
        t1 = acc[:, c0:c0 + half]
        t2 = acc[:, c0 + half:c0 + RET_DK]
        o_ref[:, c0:c0 + half] = ((t1 * cos - t2 * sin) * scale).astype(o_ref.dtype)
        o_ref[:, c0 + half:c0 + RET_DK] = ((t1 * sin + t2 * cos) * scale).astype(o_ref.dtype)


def _proj_qk(xb, w, cos, sin, seq, tm, tn):
    T, K = xb.shape
    ncols = 2 * D_QK
    jb0 = OFF_Q // tn
    n_seq_tiles = seq // tm
    kern = functools.partial(_qk_kernel, heads_per_tile=tn // RET_DK, k_tile0=D_QK // tn)
    return pl.pallas_call(
        kern,
        grid=(ncols // tn, T // tm),
        in_specs=[pl.BlockSpec((tm, K), lambda j, i: (i, 0)),
                  pl.BlockSpec((K, tn), lambda j, i: (0, jb0 + j)),
                  pl.BlockSpec((tm, RET_DK // 2), lambda j, i: (i % n_seq_tiles, 0)),
                  pl.BlockSpec((tm, RET_DK // 2), lambda j, i: (i % n_seq_tiles, 0))],
        out_specs=pl.BlockSpec((tm, tn), lambda j, i: (i, j)),
        out_shape=jax.ShapeDtypeStruct((T, ncols), BF16),
        compiler_params=_params(("parallel", "parallel")),
        name="qk_rope",
    )(xb, w, cos, sin)


def _lru_kernel(x_ref, wx_ref, wy_ref, cw_ref, cb_ref, wa_ref, ba_ref, wi_ref, bi_ref, lam_ref,
                o_ref, prev_ref, hc_ref, *, tiles_per_seq, tm, width):
    i = pl.program_id(1)
    first = (i % tiles_per_seq) == 0

    @pl.when(first)
    def _():
        prev_ref[...] = jnp.zeros_like(prev_ref)
        hc_ref[...] = jnp.zeros_like(hc_ref)

    xb = x_ref[...]
    px = _dot(xb, wx_ref[...])
    py = _dot(xb, wy_ref[...])

    row = lax.broadcasted_iota(I32, (tm, width), 0)
    prev = prev_ref[...]
    cw = cw_ref[...]
    xa = cb_ref[...] + cw[CONV_WIDTH - 1:CONV_WIDTH] * px
    for s in range(1, CONV_WIDTH):
        shifted = jnp.where(row >= s, pltpu.roll(px, s, 0), pltpu.roll(prev, s, 0))
        xa = xa + cw[CONV_WIDTH - 1 - s:CONV_WIDTH - s] * shifted
    prev_ref[...] = px

    xab = xa.astype(BF16)
    r_parts, i_parts = [], []
    for hh in range(width // LRU_BLOCK):
        xh = xab[:, hh * LRU_BLOCK:(hh + 1) * LRU_BLOCK]
        r_parts.append(_dot(xh, wa_ref[hh]))
        i_parts.append(_dot(xh, wi_ref[hh]))
    rg = jax.nn.sigmoid(jnp.concatenate(r_parts, axis=1) + ba_ref[...])
    ig = jax.nn.sigmoid(jnp.concatenate(i_parts, axis=1) + bi_ref[...])

    log_a = -LRU_C * rg * jax.nn.softplus(-lam_ref[...])
    a = jnp.exp(log_a)
    mult = jnp.sqrt(-jnp.tanh(log_a) * (a * a + 1.0))
    mult = jnp.where(jnp.logical_and(first, row == 0), 1.0, mult)
    b = mult * ig * xa

    d = 1
    while d < tm:
        keep = row >= d
        a_sh = jnp.where(keep, pltpu.roll(a, d, 0), 1.0)
        b_sh = jnp.where(keep, pltpu.roll(b, d, 0), 0.0)
        b = a * b_sh + b
        a = a * a_sh
        d *= 2
    h = b + a * hc_ref[...]
    hc_ref[...] = h[tm - 1:tm, :]

    o_ref[...] = (h * jax.nn.gelu(py, approximate=True)).astype(o_ref.dtype)


def _lru_branch(xb, w, conv_w, conv_b, wa, ba, wi, bi, lam, seq, tm):
    T, K = xb.shape
    width = 2 * LRU_BLOCK
    nj = D_LRU // width
    jy0 = OFF_LRU_Y // width
    hp = width // LRU_BLOCK
    kern = functools.partial(_lru_kernel, tiles_per_seq=seq // tm, tm=tm, width=width)
    vec = lambda j, i: (0, j)
    return pl.pallas_call(
        kern,
        grid=(nj, T // tm),
        in_specs=[pl.BlockSpec((tm, K), lambda j, i: (i, 0)),
                  pl.BlockSpec((K, width), lambda j, i: (0, j)),
                  pl.BlockSpec((K, width), lambda j, i: (0, jy0 + j)),
                  pl.BlockSpec((CONV_WIDTH, width), vec),
                  pl.BlockSpec((1, width), vec),
                  pl.BlockSpec((hp, LRU_BLOCK, LRU_BLOCK), lambda j, i: (j, 0, 0)),
                  pl.BlockSpec((1, width), vec),
                  pl.BlockSpec((hp, LRU_BLOCK, LRU_BLOCK), lambda j, i: (j, 0, 0)),
                  pl.BlockSpec((1, width), vec),
                  pl.BlockSpec((1, width), vec)],
        out_specs=pl.BlockSpec((tm, width), lambda j, i: (i, j)),
        out_shape=jax.ShapeDtypeStruct((T, D_LRU), BF16),
        scratch_shapes=[pltpu.VMEM((tm, width), F32), pltpu.VMEM((1, width), F32)],
        compiler_params=_params(("parallel", "arbitrary")),
        name="lru_branch",
    )(xb, w, w, conv_w, conv_b.reshape(1, D_LRU), wa, ba.reshape(1, D_LRU), wi, bi.reshape(1, D_LRU),
      lam.reshape(1, D_LRU))


def _ret_kernel(gc_ref, q_ref, k_ref, v_ref, g_ref, gain_ref, dm_ref, xi_ref, zeta_ref, o_ref, st_ref):
    hd = pl.program_id(1)
    n = pl.program_id(2)

    @pl.when(n == 0)
    def _():
        st_ref[...] = jnp.zeros_like(st_ref)

    q = q_ref[...]
    k = k_ref[...]
    v = v_ref[...]
    st = st_ref[...]
    s = lax.dot_general(q, k, (((1,), (1,)), ((), ())), preferred_element_type=F32) * dm_ref[0]
    o = _dot(s.astype(BF16), v) + _dot(q, st.astype(BF16)) * xi_ref[0]
    kz = (k.astype(F32) * zeta_ref[0]).astype(BF16)
    st_ref[...] = gc_ref[hd] * st + lax.dot_general(kz, v, (((0,), (0,)), ((), ())),
                                                    preferred_element_type=F32)
    mu = jnp.mean(o, axis=-1, keepdims=True)
    oc = o - mu
    var = jnp.mean(oc * oc, axis=-1, keepdims=True)
    oh = oc * lax.rsqrt(var + LN_EPS) * gain_ref[...]
    o_ref[...] = (jax.nn.silu(g_ref[...]) * oh).astype(o_ref.dtype)


def _retention(qk, v, g, gain, batch, seq):
    T = batch * seq
    C = RET_CHUNK
    nchunk = seq // C
    log_g = jnp.log1p(-jnp.exp2(-5.0 - jnp.arange(RET_HEADS, dtype=F32)))
    idx = jnp.arange(C, dtype=F32)
    diff = idx[:, None] - idx[None, :]
    dmask = jnp.where(diff >= 0, jnp.exp(jnp.maximum(diff, 0.0)[None] * log_g[:, None, None]), 0.0)
    xi = jnp.exp((idx[None] + 1.0) * log_g[:, None])[:, :, None]
    zeta = jnp.exp((C - 1.0 - idx[None]) * log_g[:, None])[:, :, None]
    g_c = jnp.exp(C * log_g)
    rows = lambda b, h, n: b * nchunk + n
    return pl.pallas_call(
        _ret_kernel,
        grid=(batch, RET_HEADS, nchunk),
        in_specs=[pl.BlockSpec(memory_space=pltpu.SMEM),
                  pl.BlockSpec((C, RET_DK), lambda b, h, n: (rows(b, h, n), h)),
                  pl.BlockSpec((C, RET_DK), lambda b, h, n: (rows(b, h, n), RET_HEADS + h)),
                  pl.BlockSpec((C, RET_DV), lambda b, h, n: (rows(b, h, n), h)),
                  pl.BlockSpec((C, RET_DV), lambda b, h, n: (rows(b, h, n), h)),
                  pl.BlockSpec((1, RET_DV), lambda b, h, n: (0, h)),
                  pl.BlockSpec((1, C, C), lambda b, h, n: (h, 0, 0)),
                  pl.BlockSpec((1, C, 1), lambda b, h, n: (h, 0, 0)),
                  pl.BlockSpec((1, C, 1), lambda b, h, n: (h, 0, 0))],
        out_specs=pl.BlockSpec((C, RET_DV), lambda b, h, n: (rows(b, h, n), h)),
        out_shape=jax.ShapeDtypeStruct((T, D_RV), BF16),
        scratch_shapes=[pltpu.VMEM((RET_DK, RET_DV), F32)],
        compiler_params=_params(("parallel", "parallel", "arbitrary")),
        name="retention",
    )(g_c, qk, qk, v, g, gain.reshape(1, D_RV), dmask, xi, zeta)


def _mix_kernel(ua_ref, ub_ref, x_ref, wlo_ref, wro_ref, wga_ref, wgb_ref, bga_ref, bgb_ref, o_ref):
    x = x_ref[...]
    ya = _dot(ua_ref[...], wlo_ref[...])
    yb = _dot(ub_ref[...], wro_ref[...])
    ga = jax.nn.sigmoid(_dot(x, wga_ref[...]) + bga_ref[...])
    gb = jax.nn.sigmoid(_dot(x, wgb_ref[...]) + bgb_ref[...])
    o_ref[...] = (ga * ya + gb * yb).astype(o_ref.dtype)


def _mix(ua, ub, xb, w_lru_out, w_ret_out, w_in, b_gate, tm, tn):
    T = xb.shape[0]
    ja0 = OFF_GA // tn
    jb0 = OFF_GB // tn
    nj = D_MODEL // tn
    return pl.pallas_call(
        _mix_kernel,
        grid=(T // tm, nj),
        in_specs=[pl.BlockSpec((tm, D_LRU), lambda i, j: (i, 0)),
                  pl.BlockSpec((tm, D_RV), lambda i, j: (i, 0)),
                  pl.BlockSpec((tm, D_MODEL), lambda i, j: (i, 0)),
                  pl.BlockSpec((D_LRU, tn), lambda i, j: (0, j)),
                  pl.BlockSpec((D_RV, tn), lambda i, j: (0, j)),
                  pl.BlockSpec((D_MODEL, tn), lambda i, j: (0, ja0 + j)),
                  pl.BlockSpec((D_MODEL, tn), lambda i, j: (0, jb0 + j)),
                  pl.BlockSpec((1, tn), lambda i, j: (0, j)),
                  pl.BlockSpec((1, tn), lambda i, j: (0, nj + j))],
        out_specs=pl.BlockSpec((tm, tn), lambda i, j: (i, j)),
        out_shape=jax.ShapeDtypeStruct((T, D_MODEL), BF16),
        compiler_params=_params(("parallel", "parallel")),
        name="gated_mix",
    )(ua, ub, xb, w_lru_out, w_ret_out, w_in, w_in, b_gate.reshape(1, 2 * D_MODEL),
      b_gate.reshape(1, 2 * D_MODEL))


def _layer_norm(y, g, b):
    mu = jnp.mean(y, axis=-1, keepdims=True)
    yc = y - mu
    var = jnp.mean(yc * yc, axis=-1, keepdims=True)
    return yc * lax.rsqrt(var + LN_EPS) * g + b


def _pack_bf16_pairs(hb):
    lo = lax.bitcast_convert_type(hb[:, :HALF].astype(F32), U32)
    hi = lax.bitcast_convert_type(hb[:, HALF:].astype(F32), U32)
    return (hi & jnp.uint32(0xFFFF0000)) | (lo >> 16)


def _unpack_bf16_pairs(p):
    lo = lax.bitcast_convert_type(p << 16, F32).astype(BF16)
    hi = lax.bitcast_convert_type(p & jnp.uint32(0xFFFF0000), F32).astype(BF16)
    return lo, hi


def _max2(a):
    return jnp.max(jnp.max(a, axis=1, keepdims=True), axis=0, keepdims=True)


def _min2(a):
    return jnp.min(jnp.min(a, axis=1, keepdims=True), axis=0, keepdims=True)


def _sum2(a):
    return jnp.sum(jnp.sum(a, axis=1, keepdims=True), axis=0, keepdims=True)


def _ln1_route_kernel(mix_ref, wo_ref, x_ref, g_ref, b_ref, wr_ref, rb_ref,
                      h_ref, hp_ref, eidx_ref, rank_ref, wsel_ref, cnt_ref, cnt_sc, *, tm):
    i = pl.program_id(0)

    @pl.when(i == 0)
    def _():
        cnt_sc[...] = jnp.zeros_like(cnt_sc)

    y = DN_ALPHA * x_ref[...] + _dot(mix_ref[...], wo_ref[...])
    h = _layer_norm(y, g_ref[...], b_ref[...])
    h_ref[...] = h
    hb = h.astype(BF16)
    hp_ref[...] = _pack_bf16_pairs(hb)

    G, GS = N_GROUPS, GROUP_SIZE
    logits = lax.dot_general(wr_ref[...], hb, (((1,), (1,)), ((), ())), preferred_element_type=F32)
    scores = jax.nn.sigmoid(logits)
    s3 = scores.reshape(G, GS, tm)
    b3 = (scores + rb_ref[...]).reshape(G, GS, tm)
    neg = jnp.float32(-jnp.inf)

    in_grp = lax.broadcasted_iota(I32, (G, GS, tm), 1)
    m1 = jnp.max(b3, axis=1, keepdims=True)
    f1 = jnp.min(jnp.where(b3 == m1, in_grp, GS), axis=1, keepdims=True)
    m2 = jnp.max(jnp.where(in_grp == f1, neg, b3), axis=1, keepdims=True)
    grp = m1 + m2

    gi = lax.broadcasted_iota(I32, (G, 1, tm), 0)
    gkeep = jnp.zeros((G, 1, tm), F32)
    cur = grp
    for _ in range(TOPK_GROUPS):
        m = jnp.max(cur, axis=0, keepdims=True)
        f = jnp.min(jnp.where(cur == m, gi, G), axis=0, keepdims=True)
        hit = gi == f
        gkeep = jnp.where(hit, 1.0, gkeep)
        cur = jnp.where(hit, neg, cur)
    emask = jnp.broadcast_to(gkeep, (G, GS, tm)) > 0.0

    ei = lax.broadcasted_iota(I32, (G, GS, tm), 0) * GS + in_grp
    cur = jnp.where(emask, b3, neg)
    hits, idxs, ws = [], [], []
    for _ in range(TOP_K):
        m = _max2(cur)
        f = _min2(jnp.where(cur == m, ei, N_EXPERTS))
        hit = ei == f
        cur = jnp.where(hit, neg, cur)
        hits.append(hit)
        idxs.append(f)
        ws.append(_sum2(jnp.where(hit, s3, 0.0)))
    wsum = ws[0]
    for r in range(1, TOP_K):
        wsum = wsum + ws[r]

    sel = jnp.zeros((G, GS, tm), F32)
    for hit in hits:
        sel = jnp.where(hit, 1.0, sel)
    sel2 = sel.reshape(N_EXPERTS, tm)
    tri = (lax.broadcasted_iota(I32, (tm, tm), 0) < lax.broadcasted_iota(I32, (tm, tm), 1))
    rank_local = _dot(sel2.astype(BF16), jnp.where(tri, 1.0, 0.0).astype(BF16))
    cnt = cnt_sc[...]
    rank3 = (rank_local + cnt[:, 0:1]).reshape(G, GS, tm)
    cnt_new = cnt + jnp.sum(sel2, axis=1, keepdims=True)
    cnt_sc[...] = cnt_new
    cnt_ref[...] = cnt_new

    for r in range(TOP_K):
        eidx_ref[0, r:r + 1, :] = idxs[r].reshape(1, tm)
        rank_ref[0, r:r + 1, :] = _sum2(jnp.where(hits[r], rank3, 0.0)).reshape(1, tm).astype(I32)
        wsel_ref[r:r + 1, :] = (ws[r] / wsum * ROUTED_SCALE).reshape(1, tm)


def _ln1_route(mixed, w_o, x2, ln_g, ln_b, w_router_t, router_bias, tm):
    T = x2.shape[0]
    kern = functools.partial(_ln1_route_kernel, tm=tm)
    row = lambda i: (i, 0)
    fixed = lambda i: (0, 0)
    col = lambda i: (0, i)
    return pl.pallas_call(
        kern,
        grid=(T // tm,),
        in_specs=[pl.BlockSpec((tm, D_MODEL), row),
                  pl.BlockSpec((D_MODEL, D_MODEL), fixed),
                  pl.BlockSpec((tm, D_MODEL), row),
                  pl.BlockSpec((1, D_MODEL), fixed),
                  pl.BlockSpec((1, D_MODEL), fixed),
                  pl.BlockSpec((N_EXPERTS, D_MODEL), fixed),
                  pl.BlockSpec((N_EXPERTS, 1), fixed)],
        out_specs=[pl.BlockSpec((tm, D_MODEL), row),
                   pl.BlockSpec((tm, HALF), row),
                   pl.BlockSpec((1, TOP_K, tm), lambda i: (i, 0, 0)),
                   pl.BlockSpec((1, TOP_K, tm), lambda i: (i, 0, 0)),
                   pl.BlockSpec((TOP_K, tm), col),
                   pl.BlockSpec((N_EXPERTS, 128), fixed)],
        out_shape=[jax.ShapeDtypeStruct((T, D_MODEL), F32),
                   jax.ShapeDtypeStruct((T, HALF), U32),
                   jax.ShapeDtypeStruct((T // tm, TOP_K, tm), I32),
                   jax.ShapeDtypeStruct((T // tm, TOP_K, tm), I32),
                   jax.ShapeDtypeStruct((TOP_K, T), F32),
                   jax.ShapeDtypeStruct((N_EXPERTS, 128), F32)],
        scratch_shapes=[pltpu.VMEM((N_EXPERTS, 128), F32)],
        compiler_params=_params(("arbitrary",)),
        name="ln1_route",
    )(mixed, w_o, x2, ln_g.reshape(1, D_MODEL), ln_b.reshape(1, D_MODEL), w_router_t,
      router_bias.reshape(N_EXPERTS, 1))


def _row_copy(src_ref, src_row, dst_ref, dst_row, sem):
    return pltpu.make_async_copy(src_ref.at[pl.ds(src_row, 1)], dst_ref.at[pl.ds(dst_row, 1)], sem)


def _dispatch_kernel(eidx_ref, rank_ref, ps_ref, fill_ref, pe_ref, hp_ref, xs_ref, zero_ref, sem, zsem,
                     *, tm):
    @pl.when(pl.program_id(0) == 0)
    def _():
        zero_ref[...] = jnp.zeros_like(zero_ref)

        def fill_expert(e, n):
            lo = fill_ref[e]
            hi = pe_ref[e]

            def fill_row(r, c):
                _row_copy(zero_ref, 0, xs_ref, r, zsem).start()
                return c

            lax.fori_loop(lo, hi, fill_row, 0)
            return n + (hi - lo)

        n_fill = lax.fori_loop(0, N_EXPERTS, fill_expert, 0)

        def drain_fill(r, c):
            _row_copy(zero_ref, 0, xs_ref, 0, zsem).wait()
            return c

        lax.fori_loop(0, n_fill, drain_fill, 0)

    def issue(t, c):
        for k in range(TOP_K):
            d = ps_ref[eidx_ref[0, k, t]] + rank_ref[0, k, t]
            _row_copy(hp_ref, t, xs_ref, d, sem).start()
        return c

    def drain(t, c):
        for k in range(TOP_K):
            _row_copy(hp_ref, 0, xs_ref, 0, sem).wait()
        return c

    lax.fori_loop(0, tm, issue, 0)
    lax.fori_loop(0, tm, drain, 0)


def _dispatch(eidx3, rank3, pad_start, fill_start, pad_end, hp, n_rows):
    T = hp.shape[0]
    tm = eidx3.shape[2]
    kern = functools.partial(_dispatch_kernel, tm=tm)
    smem = pl.BlockSpec(memory_space=pltpu.SMEM)
    idx = pl.BlockSpec((1, TOP_K, tm), lambda i: (i, 0, 0), memory_space=pltpu.SMEM)
    return pl.pallas_call(
        kern,
        grid=(T // tm,),
        in_specs=[idx, idx, smem, smem, smem, pl.BlockSpec((tm, HALF), lambda i: (i, 0))],
        out_specs=pl.BlockSpec(memory_space=pl.ANY),
        out_shape=jax.ShapeDtypeStruct((n_rows, HALF), U32),
        scratch_shapes=[pltpu.VMEM((8, HALF), U32), pltpu.SemaphoreType.DMA, pltpu.SemaphoreType.DMA],
        compiler_params=_params(("arbitrary",)),
        name="moe_dispatch",
    )(eidx3, rank3, pad_start, fill_start, pad_end, hp)


def _experts_kernel(be_ref, nb_ref, xs_ref, wg_ref, wu_ref, wd_ref, ys_ref):
    del be_ref

    @pl.when(pl.program_id(0) < nb_ref[0])
    def _():
        lo, hi = _unpack_bf16_pairs(xs_ref[...])
        gate = _dot(lo, wg_ref[0, :HALF, :]) + _dot(hi, wg_ref[0, HALF:, :])
        up = _dot(lo, wu_ref[0, :HALF, :]) + _dot(hi, wu_ref[0, HALF:, :])
        hb = (jax.nn.silu(gate) * up).astype(BF16)
        ys_ref[...] = _dot(hb, wd_ref[0])


def _experts(block_e, nb_used, xs, wg, wu, wd):
    n_rows = xs.shape[0]
    nb = n_rows // MOE_ROWS
    rows = lambda b, be, nbu: (jnp.minimum(b, nbu[0] - 1), 0)
    wsel = lambda b, be, nbu: (be[b], 0, 0)
    grid_spec = pltpu.PrefetchScalarGridSpec(
        num_scalar_prefetch=2,
        grid=(nb,),
        in_specs=[pl.BlockSpec((MOE_ROWS, HALF), rows),
                  pl.BlockSpec((1, D_MODEL, D_EXPERT), wsel),
                  pl.BlockSpec((1, D_MODEL, D_EXPERT), wsel),
                  pl.BlockSpec((1, D_EXPERT, D_MODEL), wsel)],
        out_specs=pl.BlockSpec((MOE_ROWS, D_MODEL), rows),
    )
    return pl.pallas_call(
        _experts_kernel,
        grid_spec=grid_spec,
        out_shape=jax.ShapeDtypeStruct((n_rows, D_MODEL), F32),
        compiler_params=_params(("arbitrary",)),
        name="moe_experts",
    )(block_e, nb_used, xs, wg, wu, wd)


def _combine_kernel(eidx_ref, rank_ref, ps_ref, w_ref, h_ref, wgs_ref, wus_ref, wds_ref, g_ref, b_ref,
                    ys_ref, o_ref, buf_ref, sem, *, tm):
    def issue(t, c):
        for k in range(TOP_K):
            d = ps_ref[eidx_ref[0, k, t]] + rank_ref[0, k, t]
            _row_copy(ys_ref, d, buf_ref.at[k], t, sem).start()
        return c

    def drain(t, c):
        for k in range(TOP_K):
            _row_copy(ys_ref, 0, buf_ref.at[k], 0, sem).wait()
        return c

    lax.fori_loop(0, tm, issue, 0)

    h = h_ref[...]
    hb = h.astype(BF16)
    hid = (jax.nn.silu(_dot(hb, wgs_ref[...])) * _dot(hb, wus_ref[...])).astype(BF16)
    shared = _dot(hid, wds_ref[...])

    lax.fori_loop(0, tm, drain, 0)

    w = w_ref[...]
    routed = buf_ref[0] * w[:, 0:1]
    for k in range(1, TOP_K):
        routed = routed + buf_ref[k] * w[:, k:k + 1]
    o_ref[...] = _layer_norm(DN_ALPHA * h + (routed + shared), g_ref[...], b_ref[...])


def _combine(eidx3, rank3, pad_start, w_tok, h, wgs, wus, wds, ln_g, ln_b, ys, tm):
    T = h.shape[0]
    per = eidx3.shape[2] // tm
    kern = functools.partial(_combine_kernel, tm=tm)
    row = lambda i: (i, 0)
    fixed = lambda i: (0, 0)
    idx = pl.BlockSpec((1, TOP_K, tm), lambda i: (i // per, 0, i % per), memory_space=pltpu.SMEM)
    return pl.pallas_call(
        kern,
        grid=(T // tm,),
        in_specs=[idx, idx, pl.BlockSpec(memory_space=pltpu.SMEM),
                  pl.BlockSpec((tm, TOP_K), row),
                  pl.BlockSpec((tm, D_MODEL), row),
                  pl.BlockSpec((D_MODEL, D_SHARED), fixed),
                  pl.BlockSpec((D_MODEL, D_SHARED), fixed),
                  pl.BlockSpec((D_SHARED, D_MODEL), fixed),
                  pl.BlockSpec((1, D_MODEL), fixed),
                  pl.BlockSpec((1, D_MODEL), fixed),
                  pl.BlockSpec(memory_space=pl.ANY)],
        out_specs=pl.BlockSpec((tm, D_MODEL), row),
        out_shape=jax.ShapeDtypeStruct((T, D_MODEL), F32),
        scratch_shapes=[pltpu.VMEM((TOP_K, tm, D_MODEL), F32), pltpu.SemaphoreType.DMA],
        compiler_params=_params(("arbitrary",)),
        name="moe_combine",
    )(eidx3, rank3, pad_start, w_tok, h, wgs, wus, wds, ln_g.reshape(1, D_MODEL),
      ln_b.reshape(1, D_MODEL), ys)


def _tile(n, pref):
    t = min(n, pref)
    assert n % t == 0, (n, pref)
    return t


def _layer(x, w_in, conv_w, conv_b, lru_wa, lru_ba, lru_wi, lru_bi, lru_lambda, ret_gn_gain,
           w_lru_out, w_ret_out, b_gate, w_o, ln1_g, ln1_b, w_router, router_bias,
           w_gate_e, w_up_e, w_down_e, w_gate_s, w_up_s, w_down_s, ln2_g, ln2_b):
    B, S, D = x.shape
    T = B * S
    x2 = x.reshape(T, D)
    xb = x2.astype(BF16)
    w_in_b = w_in.astype(BF16)

    half = RET_DK // 2
    freq = ROPE_THETA ** (-jnp.arange(half, dtype=F32) / half)
    ang = jnp.arange(S, dtype=I32).astype(F32)[:, None] * freq
    cos, sin = jnp.cos(ang), jnp.sin(ang)

    tm_big = _tile(S, 1024)
    qk = _proj_qk(xb, w_in_b, cos, sin, S, tm_big, 1024)
    v = _proj_plain(xb, w_in_b, OFF_V, D_RV, BF16, tm_big, 1024, "v_proj")
    g = _proj_plain(xb, w_in_b, OFF_G, D_RV, F32, tm_big, 1024, "g_proj")

    ua = _lru_branch(xb, w_in_b, conv_w, conv_b, lru_wa.astype(BF16), lru_ba, lru_wi.astype(BF16),
                     lru_bi, lru_lambda, S, _tile(S, 256))
    ub = _retention(qk, v, g, ret_gn_gain, B, S)

    mixed = _mix(ua, ub, xb, w_lru_out.astype(BF16), w_ret_out.astype(BF16), w_in_b, b_gate,
                 _tile(T, 512), 512)

    tm_r = _tile(T, 256)
    h, hp, eidx, rank, wsel, cnt = _ln1_route(mixed, w_o.astype(BF16), x2, ln1_g, ln1_b,
                                              w_router.T.astype(BF16), router_bias, tm_r)

    counts = cnt[:, 0].astype(I32)
    padded = (counts + MOE_ROWS - 1) // MOE_ROWS * MOE_ROWS
    pad_end = jnp.cumsum(padded)
    pad_start = pad_end - padded
    nb = (T * TOP_K + N_EXPERTS * (MOE_ROWS - 1) + MOE_ROWS - 1) // MOE_ROWS
    n_rows = nb * MOE_ROWS
    block_row0 = jnp.arange(nb, dtype=I32) * MOE_ROWS
    block_e = jnp.minimum(jnp.sum((pad_end[None, :] <= block_row0[:, None]).astype(I32), axis=1),
                          N_EXPERTS - 1)
    nb_used = (pad_end[-1:] // MOE_ROWS).astype(I32)

    xs = _dispatch(eidx, rank, pad_start, pad_start + counts, pad_end, hp, n_rows)
    ys = _experts(block_e, nb_used, xs, w_gate_e.astype(BF16), w_up_e.astype(BF16),
                  w_down_e.astype(BF16))
    out = _combine(eidx, rank, pad_start, wsel.T, h, w_gate_s.astype(BF16), w_up_s.astype(BF16),
                   w_down_s.astype(BF16), ln2_g, ln2_b, ys, _tile(tm_r, 128))
    return out.reshape(B, S, D)


def kernel(x, w_in, conv_w, conv_b, lru_wa, lru_ba, lru_wi, lru_bi, lru_lambda, ret_gn_gain, w_lru_out, w_ret_out, b_gate, w_o, ln1_g, ln1_b, w_router, router_bias, w_gate_e, w_up_e, w_down_e, w_gate_s, w_up_s, w_down_s, ln2_g, ln2_b):
    assert DEPTH == 1 and w_in.shape[0] == DEPTH
    args = (w_in, conv_w, conv_b, lru_wa, lru_ba, lru_wi, lru_bi, lru_lambda, ret_gn_gain,
            w_lru_out, w_ret_out, b_gate, w_o, ln1_g, ln1_b, w_router, router_bias,
            w_gate_e, w_up_e, w_down_e, w_gate_s, w_up_s, w_down_s, ln2_g, ln2_b)
    return _layer(x, *[a[0] for a in args])
```

```python
import functools

import jax
import jax.numpy as jnp
from jax import lax
from jax.experimental import pallas as pl
from jax.experimental.pallas import tpu as pltpu

F32 = jnp.float32
BF16 = jnp.bfloat16
I32 = jnp.int32
U32 = jnp.uint32

D_MODEL = 2048
D_LRU = 2048
LRU_HEADS = 16
LRU_BLOCK = D_LRU // LRU_HEADS
CONV_WIDTH = 4
LRU_C = 8.0
RET_HEADS = 8
RET_DK = 256
RET_DV = 512
D_QK = RET_HEADS * RET_DK
D_RV = RET_HEADS * RET_DV
ROPE_THETA = 10000.0
N_EXPERTS = 64
TOP_K = 8
N_GROUPS = 8
GROUP_SIZE = N_EXPERTS // N_GROUPS
TOPK_GROUPS = 4
D_EXPERT = 512
D_SHARED = 512
ROUTED_SCALE = 2.5
DEPTH = 1
DN_ALPHA = (2.0 * DEPTH) ** 0.25
LN_EPS = 1e-5

OFF_LRU_X = 0
OFF_LRU_Y = OFF_LRU_X + D_LRU
OFF_Q = OFF_LRU_Y + D_LRU
OFF_K = OFF_Q + D_QK
OFF_V = OFF_K + D_QK
OFF_G = OFF_V + D_RV
OFF_GA = OFF_G + D_RV
OFF_GB = OFF_GA + D_MODEL

V7X_VMEM_LIMIT = 56 * 1024 * 1024
LANES = 128
SUBLANES = 8
MOE_ROWS = 256
RET_CHUNK = 256
HALF = D_MODEL // 2
HP_TILES = HALF // LANES
YS_TILES = D_MODEL // LANES


def _params(sem):
    return pltpu.CompilerParams(dimension_semantics=sem, vmem_limit_bytes=V7X_VMEM_LIMIT)


def _dot(a, b):
    return jnp.dot(a, b, preferred_element_type=F32)


def _rows_to_matrix(ref3, n):
    return jnp.concatenate([ref3[:, c, :] for c in range(n)], axis=1)


def _matrix_to_rows(ref3, val, n):
    for c in range(n):
        ref3[:, c, :] = val[:, c * LANES:(c + 1) * LANES]


def _cast_weight(w_ref, wb_ref):
    @pl.when(pl.program_id(1) == 0)
    def _():
        wb_ref[...] = w_ref[...].astype(BF16)


def _mm_kernel(x_ref, w_ref, o_ref, wb_ref):
    _cast_weight(w_ref, wb_ref)
    o_ref[...] = _dot(x_ref[...], wb_ref[...]).astype(o_ref.dtype)


def _proj_plain(xb, w, col0, ncols, out_dtype, tm, tn, name):
    T, K = xb.shape
    jb0 = col0 // tn
    return pl.pallas_call(
        _mm_kernel,
        grid=(ncols // tn, T // tm),
        in_specs=[pl.BlockSpec((tm, K), lambda j, i: (i, 0)),
                  pl.BlockSpec((K, tn), lambda j, i: (0, jb0 + j))],
        out_specs=pl.BlockSpec((tm, tn), lambda j, i: (i, j)),
        out_shape=jax.ShapeDtypeStruct((T, ncols), out_dtype),
        scratch_shapes=[pltpu.VMEM((K, tn), BF16)],
        compiler_params=_params(("parallel", "arbitrary")),
        name=name,
    )(xb, w)


def _qk_kernel(x_ref, w_ref, cos_ref, sin_ref, o_ref, wb_ref, *, heads_per_tile, k_tile0):
    _cast_weight(w_ref, wb_ref)
    j = pl.program_id(0)
    acc = _dot(x_ref[...], wb_ref[...])
    scale = jnp.where(j >= k_tile0, RET_DK ** -0.5, 1.0).astype(F32)
    cos = cos_ref[...]
    sin = sin_ref[...]
    half = RET_DK // 2
    for hh in range(heads_per_tile):
        c0 = hh * RET_DK
        t1 = acc[:, c0:c0 + half]
        t2 = acc[:, c0 + half:c0 + RET_DK]
        o_ref[:, c0:c0 + half] = ((t1 * cos - t2 * sin) * scale).astype(o_ref.dtype)
        o_ref[:, c0 + half:c0 + RET_DK] = ((t1 * sin + t2 * cos) * scale).astype(o_ref.dtype)


def _proj_qk(xb, w, cos, sin, seq, tm, tn):
    T, K = xb.shape
    ncols = 2 * D_QK
    jb0 = OFF_Q // tn
    n_seq_tiles = seq // tm
    kern = functools.partial(_qk_kernel, heads_per_tile=tn // RET_DK, k_tile0=D_QK // tn)
    return pl.pallas_call(
        kern,
        grid=(ncols // tn, T // tm),
        in_specs=[pl.BlockSpec((tm, K), lambda j, i: (i, 0)),
                  pl.BlockSpec((K, tn), lambda j, i: (0, jb0 + j)),
                  pl.BlockSpec((tm, RET_DK // 2), lambda j, i: (i % n_seq_tiles, 0)),
                  pl.BlockSpec((tm, RET_DK // 2), lambda j, i: (i % n_seq_tiles, 0))],
        out_specs=pl.BlockSpec((tm, tn), lambda j, i: (i, j)),
        out_shape=jax.ShapeDtypeStruct((T, ncols), BF16),
        scratch_shapes=[pltpu.VMEM((K, tn), BF16)],
        compiler_params=_params(("parallel", "arbitrary")),
        name="qk_rope",
    )(xb, w, cos, sin)


def _lru_kernel(x_ref, wx_ref, wy_ref, cw_ref, cb_ref, wa_ref, ba_ref, wi_ref, bi_ref, lam_ref,
                o_ref, wxb_ref, wyb_ref, prev_ref, hc_ref, a_ref, b_ref, gy_ref,
                *, tiles_per_seq, tm, width):
    i = pl.program_id(1)
    first = (i % tiles_per_seq) == 0
    _cast_weight(wx_ref, wxb_ref)
    _cast_weight(wy_ref, wyb_ref)

    @pl.when(first)
    def _():
        prev_ref[...] = jnp.zeros_like(prev_ref)
        hc_ref[...] = jnp.zeros_like(hc_ref)

    xb = x_ref[...]
    px = _dot(xb, wxb_ref[...])
    gy_ref[...] = jax.nn.gelu(_dot(xb, wyb_ref[...]), approximate=True)

    row = lax.broadcasted_iota(I32, (tm, width), 0)
    prev = prev_ref[...]
    cw = cw_ref[...]
    xa = cb_ref[...] + cw[CONV_WIDTH - 1:CONV_WIDTH] * px
    for s in range(1, CONV_WIDTH):
        shifted = jnp.where(row >= s, pltpu.roll(px, s, 0), pltpu.roll(prev, s, 0))
        xa = xa + cw[CONV_WIDTH - 1 - s:CONV_WIDTH - s] * shifted
    prev_ref[...] = px

    xab = xa.astype(BF16)
    r_parts, i_parts = [], []
    for hh in range(width // LRU_BLOCK):
        xh = xab[:, hh * LRU_BLOCK:(hh + 1) * LRU_BLOCK]
        r_parts.append(_dot(xh, wa_ref[hh]))
        i_parts.append(_dot(xh, wi_ref[hh]))
    rg = jax.nn.sigmoid(jnp.concatenate(r_parts, axis=1) + ba_ref[...])
    ig = jax.nn.sigmoid(jnp.concatenate(i_parts, axis=1) + bi_ref[...])

    log_a = -LRU_C * rg * jax.nn.softplus(-lam_ref[...])
    a = jnp.exp(log_a)
    mult = jnp.sqrt(-jnp.tanh(log_a) * (a * a + 1.0))
    mult = jnp.where(jnp.logical_and(first, row == 0), 1.0, mult)
    b = mult * ig * xa
    nl = width // LANES
    for c in range(nl):
        a_ref[c] = a[:, c * LANES:(c + 1) * LANES]
        b_ref[c] = b[:, c * LANES:(c + 1) * LANES]

    seg = tm // SUBLANES
    run_a = [jnp.ones((SUBLANES, LANES), F32) for _ in range(nl)]
    run_h = [jnp.zeros((SUBLANES, LANES), F32) for _ in range(nl)]
    for q in range(seg):
        rows = pl.ds(q, SUBLANES, stride=seg)
        for c in range(nl):
            aq = a_ref[c, rows, :]
            run_h[c] = aq * run_h[c] + b_ref[c, rows, :]
            run_a[c] = aq * run_a[c]
            b_ref[c, rows, :] = run_h[c]
            a_ref[c, rows, :] = run_a[c]
    end_a = jnp.concatenate(run_a, axis=1)
    end_h = jnp.concatenate(run_h, axis=1)

    carry = hc_ref[...]
    for s in range(SUBLANES):
        rows = slice(s * seg, (s + 1) * seg)
        loc = jnp.concatenate([b_ref[c, rows, :] for c in range(nl)], axis=1)
        cum = jnp.concatenate([a_ref[c, rows, :] for c in range(nl)], axis=1)
        h = loc + cum * carry
        o_ref[rows, :] = (h * gy_ref[rows, :]).astype(o_ref.dtype)
        carry = end_a[s:s + 1, :] * carry + end_h[s:s + 1, :]
    hc_ref[...] = carry


def _lru_branch(xb, w, conv_w, conv_b, wa, ba, wi, bi, lam, seq, tm, width):
    T, K = xb.shape
    nj = D_LRU // width
    jy0 = OFF_LRU_Y // width
    hp = width // LRU_BLOCK
    kern = functools.partial(_lru_kernel, tiles_per_seq=seq // tm, tm=tm, width=width)
    vec = lambda j, i: (0, j)
    return pl.pallas_call(
        kern,
        grid=(nj, T // tm),
        in_specs=[pl.BlockSpec((tm, K), lambda j, i: (i, 0)),
                  pl.BlockSpec((K, width), lambda j, i: (0, j)),
                  pl.BlockSpec((K, width), lambda j, i: (0, jy0 + j)),
                  pl.BlockSpec((CONV_WIDTH, width), vec),
                  pl.BlockSpec((1, width), vec),
                  pl.BlockSpec((hp, LRU_BLOCK, LRU_BLOCK), lambda j, i: (j, 0, 0)),
                  pl.BlockSpec((1, width), vec),
                  pl.BlockSpec((hp, LRU_BLOCK, LRU_BLOCK), lambda j, i: (j, 0, 0)),
                  pl.BlockSpec((1, width), vec),
                  pl.BlockSpec((1, width), vec)],
        out_specs=pl.BlockSpec((tm, width), lambda j, i: (i, j)),
        out_shape=jax.ShapeDtypeStruct((T, D_LRU), BF16),
        scratch_shapes=[pltpu.VMEM((K, width), BF16), pltpu.VMEM((K, width), BF16),
                        pltpu.VMEM((tm, width), F32), pltpu.VMEM((1, width), F32),
                        pltpu.VMEM((width // LANES, tm, LANES), F32),
                        pltpu.VMEM((width // LANES, tm, LANES), F32),
                        pltpu.VMEM((tm, width), F32)],
        compiler_params=_params(("parallel", "arbitrary")),
        name="lru_branch",
    )(xb, w, w, conv_w, conv_b.reshape(1, D_LRU), wa, ba.reshape(1, D_LRU), wi, bi.reshape(1, D_LRU),
      lam.reshape(1, D_LRU))


def _ret_kernel(gc_ref, q_ref, k_ref, v_ref, g_ref, gain_ref, dm_ref, xi_ref, zeta_ref, o_ref, st_ref):
    @pl.when(pl.program_id(1) == 0)
    def _():
        st_ref[...] = jnp.zeros_like(st_ref)

    for hd in range(RET_HEADS):
        qc = slice(hd * RET_DK, (hd + 1) * RET_DK)
        vc = slice(hd * RET_DV, (hd + 1) * RET_DV)
        q = q_ref[:, qc]
        k = k_ref[:, qc]
        v = v_ref[:, vc]
        st = st_ref[hd]
        s = lax.dot_general(q, k, (((1,), (1,)), ((), ())), preferred_element_type=F32) * dm_ref[hd]
        o = _dot(s.astype(BF16), v) + _dot(q, st.astype(BF16)) * xi_ref[hd]
        kz = (k.astype(F32) * zeta_ref[hd]).astype(BF16)
        st_ref[hd] = gc_ref[hd] * st + lax.dot_general(kz, v, (((0,), (0,)), ((), ())),
                                                       preferred_element_type=F32)
        mu = jnp.mean(o, axis=-1, keepdims=True)
        oc = o - mu
        var = jnp.mean(oc * oc, axis=-1, keepdims=True)
        oh = oc * lax.rsqrt(var + LN_EPS) * gain_ref[:, vc]
        o_ref[:, vc] = (jax.nn.silu(g_ref[:, vc]) * oh).astype(o_ref.dtype)


def _retention(qk, v, g, gain, batch, seq):
    T = batch * seq
    C = min(RET_CHUNK, seq)
    nchunk = seq // C
    H = RET_HEADS
    log_g = jnp.log1p(-jnp.exp2(-5.0 - jnp.arange(H, dtype=F32)))
    idx = jnp.arange(C, dtype=F32)
    diff = idx[:, None] - idx[None, :]
    dmask = jnp.where(diff >= 0, jnp.exp(jnp.maximum(diff, 0.0)[None] * log_g[:, None, None]), 0.0)
    xi = jnp.exp((idx[None] + 1.0) * log_g[:, None])[:, :, None]
    zeta = jnp.exp((C - 1.0 - idx[None]) * log_g[:, None])[:, :, None]
    g_c = jnp.exp(C * log_g)
    rows = lambda b, n: b * nchunk + n
    whole3 = lambda b, n: (0, 0, 0)
    return pl.pallas_call(
        _ret_kernel,
        grid=(batch, nchunk),
        in_specs=[pl.BlockSpec(memory_space=pltpu.SMEM),
                  pl.BlockSpec((C, D_QK), lambda b, n: (rows(b, n), 0)),
                  pl.BlockSpec((C, D_QK), lambda b, n: (rows(b, n), 1)),
                  pl.BlockSpec((C, D_RV), lambda b, n: (rows(b, n), 0)),
                  pl.BlockSpec((C, D_RV), lambda b, n: (rows(b, n), 0)),
                  pl.BlockSpec((1, D_RV), lambda b, n: (0, 0)),
                  pl.BlockSpec((H, C, C), whole3),
                  pl.BlockSpec((H, C, 1), whole3),
                  pl.BlockSpec((H, C, 1), whole3)],
        out_specs=pl.BlockSpec((C, D_RV), lambda b, n: (rows(b, n), 0)),
        out_shape=jax.ShapeDtypeStruct((T, D_RV), BF16),
        scratch_shapes=[pltpu.VMEM((H, RET_DK, RET_DV), F32)],
        compiler_params=_params(("parallel", "arbitrary")),
        name="retention",
    )(g_c, qk, qk, v, g, gain.reshape(1, D_RV), dmask, xi, zeta)


def _mix_kernel(ua_ref, ub_ref, x_ref, wlo_ref, wro_ref, wga_ref, wgb_ref, bga_ref, bgb_ref, o_ref):
    x = x_ref[...]
    ya = _dot(ua_ref[...], wlo_ref[...])
    yb = _dot(ub_ref[...], wro_ref[...])
    ga = jax.nn.sigmoid(_dot(x, wga_ref[...]) + bga_ref[...])
    gb = jax.nn.sigmoid(_dot(x, wgb_ref[...]) + bgb_ref[...])
    o_ref[...] = (ga * ya + gb * yb).astype(o_ref.dtype)


def _mix(ua, ub, xb, w_lru_out, w_ret_out, w_gates, b_gate, tm, tn):
    T = xb.shape[0]
    nj = D_MODEL // tn
    return pl.pallas_call(
        _mix_kernel,
        grid=(T // tm, nj),
        in_specs=[pl.BlockSpec((tm, D_LRU), lambda i, j: (i, 0)),
                  pl.BlockSpec((tm, D_RV), lambda i, j: (i, 0)),
                  pl.BlockSpec((tm, D_MODEL), lambda i, j: (i, 0)),
                  pl.BlockSpec((D_LRU, tn), lambda i, j: (0, j)),
                  pl.BlockSpec((D_RV, tn), lambda i, j: (0, j)),
                  pl.BlockSpec((D_MODEL, tn), lambda i, j: (0, j)),
                  pl.BlockSpec((D_MODEL, tn), lambda i, j: (0, nj + j)),
                  pl.BlockSpec((1, tn), lambda i, j: (0, j)),
                  pl.BlockSpec((1, tn), lambda i, j: (0, nj + j))],
        out_specs=pl.BlockSpec((tm, tn), lambda i, j: (i, j)),
        out_shape=jax.ShapeDtypeStruct((T, D_MODEL), BF16),
        compiler_params=_params(("parallel", "parallel")),
        name="gated_mix",
    )(ua, ub, xb, w_lru_out, w_ret_out, w_gates, w_gates, b_gate.reshape(1, 2 * D_MODEL),
      b_gate.reshape(1, 2 * D_MODEL))


def _layer_norm(y, g, b):
    mu = jnp.mean(y, axis=-1, keepdims=True)
    yc = y - mu
    var = jnp.mean(yc * yc, axis=-1, keepdims=True)
    return yc * lax.rsqrt(var + LN_EPS) * g + b


def _pack_bf16_pairs(hb):
    lo = lax.bitcast_convert_type(hb[:, :HALF].astype(F32), U32)
    hi = lax.bitcast_convert_type(hb[:, HALF:].astype(F32), U32)
    return (hi & jnp.uint32(0xFFFF0000)) | (lo >> 16)


def _unpack_bf16_pairs(p):
    lo = lax.bitcast_convert_type(p << 16, F32).astype(BF16)
    hi = lax.bitcast_convert_type(p & jnp.uint32(0xFFFF0000), F32).astype(BF16)
    return lo, hi


def _max2(a):
    return jnp.max(jnp.max(a, axis=1, keepdims=True), axis=0, keepdims=True)


def _min2(a):
    return jnp.min(jnp.min(a, axis=1, keepdims=True), axis=0, keepdims=True)


def _sum2(a):
    return jnp.sum(jnp.sum(a, axis=1, keepdims=True), axis=0, keepdims=True)


def _ln1_route_kernel(mix_ref, wo_ref, x_ref, g_ref, b_ref, wr_ref, rb_ref,
                      h_ref, hp_ref, eidx_ref, rank_ref, wsel_ref, cnt_ref, cnt_sc, *, tm):
    i = pl.program_id(0)

    @pl.when(i == 0)
    def _():
        cnt_sc[...] = jnp.zeros_like(cnt_sc)

    y = DN_ALPHA * x_ref[...] + _dot(mix_ref[...], wo_ref[...])
    h = _layer_norm(y, g_ref[...], b_ref[...])
    h_ref[...] = h
    hb = h.astype(BF16)
    _matrix_to_rows(hp_ref, _pack_bf16_pairs(hb), HP_TILES)

    G, GS = N_GROUPS, GROUP_SIZE
    logits = lax.dot_general(wr_ref[...], hb, (((1,), (1,)), ((), ())), preferred_element_type=F32)
    scores = jax.nn.sigmoid(logits)
    s3 = scores.reshape(G, GS, tm)
    b3 = (scores + rb_ref[...]).reshape(G, GS, tm)
    neg = jnp.float32(-jnp.inf)

    in_grp = lax.broadcasted_iota(I32, (G, GS, tm), 1)
    m1 = jnp.max(b3, axis=1, keepdims=True)
    f1 = jnp.min(jnp.where(b3 == m1, in_grp, GS), axis=1, keepdims=True)
    m2 = jnp.max(jnp.where(in_grp == f1, neg, b3), axis=1, keepdims=True)
    grp = m1 + m2

    gi = lax.broadcasted_iota(I32, (G, 1, tm), 0)
    gkeep = jnp.zeros((G, 1, tm), F32)
    cur = grp
    for _ in range(TOPK_GROUPS):
        m = jnp.max(cur, axis=0, keepdims=True)
        f = jnp.min(jnp.where(cur == m, gi, G), axis=0, keepdims=True)
        hit = gi == f
        gkeep = jnp.where(hit, 1.0, gkeep)
        cur = jnp.where(hit, neg, cur)
    emask = jnp.broadcast_to(gkeep, (G, GS, tm)) > 0.0

    ei = lax.broadcasted_iota(I32, (G, GS, tm), 0) * GS + in_grp
    cur = jnp.where(emask, b3, neg)
    hits, idxs, ws = [], [], []
    for _ in range(TOP_K):
        m = _max2(cur)
        f = _min2(jnp.where(cur == m, ei, N_EXPERTS))
        hit = ei == f
        cur = jnp.where(hit, neg, cur)
        hits.append(hit)
        idxs.append(f)
        ws.append(_sum2(jnp.where(hit, s3, 0.0)))
    wsum = ws[0]
    for r in range(1, TOP_K):
        wsum = wsum + ws[r]

    sel = jnp.zeros((G, GS, tm), F32)
    for hit in hits:
        sel = jnp.where(hit, 1.0, sel)
    sel2 = sel.reshape(N_EXPERTS, tm)
    tri = (lax.broadcasted_iota(I32, (tm, tm), 0) < lax.broadcasted_iota(I32, (tm, tm), 1))
    rank_local = _dot(sel2.astype(BF16), jnp.where(tri, 1.0, 0.0).astype(BF16))
    cnt = cnt_sc[...]
    rank3 = (rank_local + cnt[:, 0:1]).reshape(G, GS, tm)
    cnt_new = cnt + jnp.sum(sel2, axis=1, keepdims=True)
    cnt_sc[...] = cnt_new
    cnt_ref[...] = cnt_new

    for r in range(TOP_K):
        eidx_ref[0, r:r + 1, :] = idxs[r].reshape(1, tm)
        rank_ref[0, r:r + 1, :] = _sum2(jnp.where(hits[r], rank3, 0.0)).reshape(1, tm).astype(I32)
        wsel_ref[r:r + 1, :] = (ws[r] / wsum * ROUTED_SCALE).reshape(1, tm)


def _ln1_route(mixed, w_o, x2, ln_g, ln_b, w_router_t, router_bias, tm):
    T = x2.shape[0]
    kern = functools.partial(_ln1_route_kernel, tm=tm)
    row = lambda i: (i, 0)
    fixed = lambda i: (0, 0)
    col = lambda i: (0, i)
    tile3 = lambda i: (i, 0, 0)
    return pl.pallas_call(
        kern,
        grid=(T // tm,),
        in_specs=[pl.BlockSpec((tm, D_MODEL), row),
                  pl.BlockSpec((D_MODEL, D_MODEL), fixed),
                  pl.BlockSpec((tm, D_MODEL), row),
                  pl.BlockSpec((1, D_MODEL), fixed),
                  pl.BlockSpec((1, D_MODEL), fixed),
                  pl.BlockSpec((N_EXPERTS, D_MODEL), fixed),
                  pl.BlockSpec((N_EXPERTS, 1), fixed)],
        out_specs=[pl.BlockSpec((tm, D_MODEL), row),
                   pl.BlockSpec((tm, HP_TILES, LANES), tile3),
                   pl.BlockSpec((1, TOP_K, tm), tile3),
                   pl.BlockSpec((1, TOP_K, tm), tile3),
                   pl.BlockSpec((TOP_K, tm), col),
                   pl.BlockSpec((N_EXPERTS, LANES), fixed)],
        out_shape=[jax.ShapeDtypeStruct((T, D_MODEL), F32),
                   jax.ShapeDtypeStruct((T, HP_TILES, LANES), U32),
                   jax.ShapeDtypeStruct((T // tm, TOP_K, tm), I32),
                   jax.ShapeDtypeStruct((T // tm, TOP_K, tm), I32),
                   jax.ShapeDtypeStruct((TOP_K, T), F32),
                   jax.ShapeDtypeStruct((N_EXPERTS, LANES), F32)],
        scratch_shapes=[pltpu.VMEM((N_EXPERTS, LANES), F32)],
        compiler_params=_params(("arbitrary",)),
        name="ln1_route",
    )(mixed, w_o, x2, ln_g.reshape(1, D_MODEL), ln_b.reshape(1, D_MODEL), w_router_t,
      router_bias.reshape(N_EXPERTS, 1))


def _dest_kernel(ps_ref, eidx_ref, rank_ref, dest_ref):
    eidx = eidx_ref[...]
    dest = rank_ref[...]
    for e in range(N_EXPERTS):
        dest = dest + jnp.where(eidx == e, ps_ref[e], 0)
    dest_ref[...] = dest


def _dest_rows(pad_start, eidx3, rank3):
    vmem = pl.BlockSpec(memory_space=pltpu.VMEM)
    return pl.pallas_call(
        _dest_kernel,
        in_specs=[pl.BlockSpec(memory_space=pltpu.SMEM), vmem, vmem],
        out_specs=vmem,
        out_shape=jax.ShapeDtypeStruct(eidx3.shape, I32),
        name="moe_dest",
    )(pad_start, eidx3, rank3)


def _row_copy(src_ref, src_row, dst_ref, dst_row, sem):
    return pltpu.make_async_copy(src_ref.at[src_row], dst_ref.at[dst_row], sem)


def _dispatch_kernel(dest_ref, fill_ref, pe_ref, hp_ref, xs_ref, zero_ref, sem, zsem, *, tm):
    @pl.when(pl.program_id(0) == 0)
    def _():
        zero_ref[...] = jnp.zeros_like(zero_ref)

        def fill_expert(e, n):
            lo = fill_ref[e]
            hi = pe_ref[e]

            def fill_row(r, c):
                _row_copy(zero_ref, 0, xs_ref, r, zsem).start()
                return c

            lax.fori_loop(lo, hi, fill_row, 0)
            return n + (hi - lo)

        n_fill = lax.fori_loop(0, N_EXPERTS, fill_expert, 0)

        def drain_fill(r, c):
            _row_copy(zero_ref, 0, xs_ref, 0, zsem).wait()
            return c

        lax.fori_loop(0, n_fill, drain_fill, 0)

    def issue(t, c):
        for k in range(TOP_K):
            _row_copy(hp_ref, t, xs_ref, dest_ref[0, k, t], sem).start()
        return c

    def drain(t, c):
        for k in range(TOP_K):
            _row_copy(hp_ref, 0, xs_ref, 0, sem).wait()
        return c

    lax.fori_loop(0, tm, issue, 0)
    lax.fori_loop(0, tm, drain, 0)


def _dispatch(dest3, fill_start, pad_end, hp3, n_rows):
    T = hp3.shape[0]
    tm = dest3.shape[2]
    kern = functools.partial(_dispatch_kernel, tm=tm)
    smem = pl.BlockSpec(memory_space=pltpu.SMEM)
    return pl.pallas_call(
        kern,
        grid=(T // tm,),
        in_specs=[pl.BlockSpec((1, TOP_K, tm), lambda i: (i, 0, 0), memory_space=pltpu.SMEM),
                  smem, smem,
                  pl.BlockSpec((tm, HP_TILES, LANES), lambda i: (i, 0, 0))],
        out_specs=pl.BlockSpec(memory_space=pl.ANY),
        out_shape=jax.ShapeDtypeStruct((n_rows, HP_TILES, LANES), U32),
        scratch_shapes=[pltpu.VMEM((1, HP_TILES, LANES), U32), pltpu.SemaphoreType.DMA,
                        pltpu.SemaphoreType.DMA],
        compiler_params=_params(("arbitrary",)),
        name="moe_dispatch",
    )(dest3, fill_start, pad_end, hp3)


def _experts_kernel(be_ref, nb_ref, xs_ref, wg_ref, wu_ref, wd_ref, ys_ref):
    del be_ref

    @pl.when(pl.program_id(0) < nb_ref[0])
    def _():
        lo, hi = _unpack_bf16_pairs(_rows_to_matrix(xs_ref, HP_TILES))
        gate = _dot(lo, wg_ref[0, :HALF, :]) + _dot(hi, wg_ref[0, HALF:, :])
        up = _dot(lo, wu_ref[0, :HALF, :]) + _dot(hi, wu_ref[0, HALF:, :])
        hb = (jax.nn.silu(gate) * up).astype(BF16)
        _matrix_to_rows(ys_ref, _dot(hb, wd_ref[0]), YS_TILES)


def _experts(block_e, nb_used, xs3, wg, wu, wd):
    n_rows = xs3.shape[0]
    nb = n_rows // MOE_ROWS
    rows = lambda b, be, nbu: (jnp.minimum(b, nbu[0] - 1), 0, 0)
    wsel = lambda b, be, nbu: (be[b], 0, 0)
    grid_spec = pltpu.PrefetchScalarGridSpec(
        num_scalar_prefetch=2,
        grid=(nb,),
        in_specs=[pl.BlockSpec((MOE_ROWS, HP_TILES, LANES), rows),
                  pl.BlockSpec((1, D_MODEL, D_EXPERT), wsel),
                  pl.BlockSpec((1, D_MODEL, D_EXPERT), wsel),
                  pl.BlockSpec((1, D_EXPERT, D_MODEL), wsel)],
        out_specs=pl.BlockSpec((MOE_ROWS, YS_TILES, LANES), rows),
    )
    return pl.pallas_call(
        _experts_kernel,
        grid_spec=grid_spec,
        out_shape=jax.ShapeDtypeStruct((n_rows, YS_TILES, LANES), F32),
        compiler_params=_params(("arbitrary",)),
        name="moe_experts",
    )(block_e, nb_used, xs3, wg, wu, wd)


def _combine_kernel(dest_ref, w_ref, h_ref, wgs_ref, wus_ref, wds_ref, g_ref, b_ref,
                    ys_ref, o_ref, buf_ref, sem, *, tm):
    def issue(t, c):
        for k in range(TOP_K):
            _row_copy(ys_ref, dest_ref[0, k, t], buf_ref.at[k], t, sem).start()
        return c

    def drain(t, c):
        for k in range(TOP_K):
            _row_copy(ys_ref, 0, buf_ref.at[k], 0, sem).wait()
        return c

    lax.fori_loop(0, tm, issue, 0)

    h = h_ref[...]
    hb = h.astype(BF16)
    hid = (jax.nn.silu(_dot(hb, wgs_ref[...])) * _dot(hb, wus_ref[...])).astype(BF16)
    shared = _dot(hid, wds_ref[...])

    lax.fori_loop(0, tm, drain, 0)

    w = w_ref[...]
    routed = _rows_to_matrix(buf_ref.at[0], YS_TILES) * w[:, 0:1]
    for k in range(1, TOP_K):
        routed = routed + _rows_to_matrix(buf_ref.at[k], YS_TILES) * w[:, k:k + 1]
    o_ref[...] = _layer_norm(DN_ALPHA * h + (routed + shared), g_ref[...], b_ref[...])


def _combine(dest3, w_tok, h, wgs, wus, wds, ln_g, ln_b, ys3, tm):
    T = h.shape[0]
    per = dest3.shape[2] // tm
    kern = functools.partial(_combine_kernel, tm=tm)
    row = lambda i: (i, 0)
    fixed = lambda i: (0, 0)
    return pl.pallas_call(
        kern,
        grid=(T // tm,),
        in_specs=[pl.BlockSpec((1, TOP_K, tm), lambda i: (i // per, 0, i % per),
                               memory_space=pltpu.SMEM),
                  pl.BlockSpec((tm, TOP_K), row),
                  pl.BlockSpec((tm, D_MODEL), row),
                  pl.BlockSpec((D_MODEL, D_SHARED), fixed),
                  pl.BlockSpec((D_MODEL, D_SHARED), fixed),
                  pl.BlockSpec((D_SHARED, D_MODEL), fixed),
                  pl.BlockSpec((1, D_MODEL), fixed),
                  pl.BlockSpec((1, D_MODEL), fixed),
                  pl.BlockSpec(memory_space=pl.ANY)],
        out_specs=pl.BlockSpec((tm, D_MODEL), row),
        out_shape=jax.ShapeDtypeStruct((T, D_MODEL), F32),
        scratch_shapes=[pltpu.VMEM((TOP_K, tm, YS_TILES, LANES), F32), pltpu.SemaphoreType.DMA],
        compiler_params=_params(("arbitrary",)),
        name="moe_combine",
    )(dest3, w_tok, h, wgs, wus, wds, ln_g.reshape(1, D_MODEL), ln_b.reshape(1, D_MODEL), ys3)


def _tile(n, pref):
    t = min(n, pref)
    assert n % t == 0, (n, pref)
    return t


def _layer(x, w_in, conv_w, conv_b, lru_wa, lru_ba, lru_wi, lru_bi, lru_lambda, ret_gn_gain,
           w_lru_out, w_ret_out, b_gate, w_o, ln1_g, ln1_b, w_router, router_bias,
           w_gate_e, w_up_e, w_down_e, w_gate_s, w_up_s, w_down_s, ln2_g, ln2_b):
    B, S, D = x.shape
    T = B * S
    x2 = x.reshape(T, D)
    xb = x2.astype(BF16)

    half = RET_DK // 2
    freq = ROPE_THETA ** (-jnp.arange(half, dtype=F32) / half)
    ang = jnp.arange(S, dtype=I32).astype(F32)[:, None] * freq
    cos, sin = jnp.cos(ang), jnp.sin(ang)

    tm_big = _tile(S, 1024)
    qk = _proj_qk(xb, w_in, cos, sin, S, tm_big, 1024)
    v = _proj_plain(xb, w_in, OFF_V, D_RV, BF16, tm_big, 1024, "v_proj")
    g = _proj_plain(xb, w_in, OFF_G, D_RV, F32, tm_big, 1024, "g_proj")

    ua = _lru_branch(xb, w_in, conv_w, conv_b, lru_wa.astype(BF16), lru_ba, lru_wi.astype(BF16),
                     lru_bi, lru_lambda, S, _tile(S, 256), 512)
    ub = _retention(qk, v, g, ret_gn_gain, B, S)

    mixed = _mix(ua, ub, xb, w_lru_out.astype(BF16), w_ret_out.astype(BF16),
                 w_in[:, OFF_GA:].astype(BF16), b_gate, _tile(T, 512), 512)

    tm_r = _tile(T, 256)
    h, hp3, eidx, rank, wsel, cnt = _ln1_route(mixed, w_o.astype(BF16), x2, ln1_g, ln1_b,
                                               w_router.T.astype(BF16), router_bias, tm_r)

    counts = cnt[:, 0].astype(I32)
    padded = (counts + MOE_ROWS - 1) // MOE_ROWS * MOE_ROWS
    pad_end = jnp.cumsum(padded)
    pad_start = pad_end - padded
    nb = (T * TOP_K + N_EXPERTS * (MOE_ROWS - 1) + MOE_ROWS - 1) // MOE_ROWS
    n_rows = nb * MOE_ROWS
    block_row0 = jnp.arange(nb, dtype=I32) * MOE_ROWS
    block_e = jnp.minimum(jnp.sum((pad_end[None, :] <= block_row0[:, None]).astype(I32), axis=1),
                          N_EXPERTS - 1)
    nb_used = (pad_end[-1:] // MOE_ROWS).astype(I32)

    dest = _dest_rows(pad_start, eidx, rank)
    xs3 = _dispatch(dest, pad_start + counts, pad_end, hp3, n_rows)
    ys3 = _experts(block_e, nb_used, xs3, w_gate_e.astype(BF16), w_up_e.astype(BF16),
                   w_down_e.astype(BF16))
    out = _combine(dest, wsel.T, h, w_gate_s.astype(BF16), w_up_s.astype(BF16),
                   w_down_s.astype(BF16), ln2_g, ln2_b, ys3, _tile(tm_r, 128))
    return out.reshape(B, S, D)


def kernel(x, w_in, conv_w, conv_b, lru_wa, lru_ba, lru_wi, lru_bi, lru_lambda, ret_gn_gain, w_lru_out, w_ret_out, b_gate, w_o, ln1_g, ln1_b, w_router, router_bias, w_gate_e, w_up_e, w_down_e, w_gate_s, w_up_s, w_down_s, ln2_g, ln2_b):
    assert DEPTH == 1 and w_in.shape[0] == DEPTH
    args = (w_in, conv_w, conv_b, lru_wa, lru_ba, lru_wi, lru_bi, lru_lambda, ret_gn_gain,
            w_lru_out, w_ret_out, b_gate, w_o, ln1_g, ln1_b, w_router, router_bias,
            w_gate_e, w_up_e, w_down_e, w_gate_s, w_up_s, w_down_s, ln2_g, ln2_b)
    return _layer(x, *[a[0] for a in args])
```

```python
import functools

import jax
import jax.numpy as jnp
from jax import lax
from jax.experimental import pallas as pl
from jax.experimental.pallas import tpu as pltpu

F32 = jnp.float32
BF16 = jnp.bfloat16
I32 = jnp.int32
U32 = jnp.uint32

D_MODEL = 2048
D_LRU = 2048
LRU_HEADS = 16
LRU_BLOCK = D_LRU // LRU_HEADS
CONV_WIDTH = 4
LRU_C = 8.0
RET_HEADS = 8
RET_DK = 256
RET_DV = 512
D_QK = RET_HEADS * RET_DK
D_RV = RET_HEADS * RET_DV
ROPE_THETA = 10000.0
N_EXPERTS = 64
TOP_K = 8
N_GROUPS = 8
GROUP_SIZE = N_EXPERTS // N_GROUPS
TOPK_GROUPS = 4
D_EXPERT = 512
D_SHARED = 512
ROUTED_SCALE = 2.5
DEPTH = 1
DN_ALPHA = (2.0 * DEPTH) ** 0.25
LN_EPS = 1e-5

OFF_LRU_X = 0
OFF_LRU_Y = OFF_LRU_X + D_LRU
OFF_Q = OFF_LRU_Y + D_LRU
OFF_K = OFF_Q + D_QK
OFF_V = OFF_K + D_QK
OFF_G = OFF_V + D_RV
OFF_GA = OFF_G + D_RV
OFF_GB = OFF_GA + D_MODEL

V7X_VMEM_LIMIT = 56 * 1024 * 1024
LANES = 128
SUBLANES = 8
MOE_ROWS = 256
RET_CHUNK = 256
HALF = D_MODEL // 2


def _params(sem):
    return pltpu.CompilerParams(dimension_semantics=sem, vmem_limit_bytes=V7X_VMEM_LIMIT)


def _dot(a, b):
    return jnp.dot(a, b, preferred_element_type=F32)


def _cast_weight(w_ref, wb_ref):
    @pl.when(pl.program_id(1) == 0)
    def _():
        wb_ref[...] = w_ref[...].astype(BF16)


def _mm_kernel(x_ref, w_ref, o_ref, wb_ref):
    _cast_weight(w_ref, wb_ref)
    o_ref[...] = _dot(x_ref[...], wb_ref[...]).astype(o_ref.dtype)


def _proj_plain(xb, w, col0, ncols, out_dtype, tm, tn, name):
    T, K = xb.shape
    jb0 = col0 // tn
    return pl.pallas_call(
        _mm_kernel,
        grid=(ncols // tn, T // tm),
        in_specs=[pl.BlockSpec((tm, K), lambda j, i: (i, 0)),
                  pl.BlockSpec((K, tn), lambda j, i: (0, jb0 + j))],
        out_specs=pl.BlockSpec((tm, tn), lambda j, i: (i, j)),
        out_shape=jax.ShapeDtypeStruct((T, ncols), out_dtype),
        scratch_shapes=[pltpu.VMEM((K, tn), BF16)],
        compiler_params=_params(("parallel", "arbitrary")),
        name=name,
    )(xb, w)


def _qk_kernel(x_ref, w_ref, cos_ref, sin_ref, o_ref, wb_ref, *, heads_per_tile, k_tile0):
    _cast_weight(w_ref, wb_ref)
    j = pl.program_id(0)
    acc = _dot(x_ref[...], wb_ref[...])
    scale = jnp.where(j >= k_tile0, RET_DK ** -0.5, 1.0).astype(F32)
    cos = cos_ref[...]
    sin = sin_ref[...]
    half = RET_DK // 2
    for hh in range(heads_per_tile):
        c0 = hh * RET_DK
        t1 = acc[:, c0:c0 + half]
        t2 = acc[:, c0 + half:c0 + RET_DK]
        o_ref[:, c0:c0 + half] = ((t1 * cos - t2 * sin) * scale).astype(o_ref.dtype)
        o_ref[:, c0 + half:c0 + RET_DK] = ((t1 * sin + t2 * cos) * scale).astype(o_ref.dtype)


def _proj_qk(xb, w, cos, sin, seq, tm, tn):
    T, K = xb.shape
    ncols = 2 * D_QK
    jb0 = OFF_Q // tn
    n_seq_tiles = seq // tm
    kern = functools.partial(_qk_kernel, heads_per_tile=tn // RET_DK, k_tile0=D_QK // tn)
    return pl.pallas_call(
        kern,
        grid=(ncols // tn, T // tm),
        in_specs=[pl.BlockSpec((tm, K), lambda j, i: (i, 0)),
                  pl.BlockSpec((K, tn), lambda j, i: (0, jb0 + j)),
                  pl.BlockSpec((tm, RET_DK // 2), lambda j, i: (i % n_seq_tiles, 0)),
                  pl.BlockSpec((tm, RET_DK // 2), lambda j, i: (i % n_seq_tiles, 0))],
        out_specs=pl.BlockSpec((tm, tn), lambda j, i: (i, j)),
        out_shape=jax.ShapeDtypeStruct((T, ncols), BF16),
        scratch_shapes=[pltpu.VMEM((K, tn), BF16)],
        compiler_params=_params(("parallel", "arbitrary")),
        name="qk_rope",
    )(xb, w, cos, sin)


def _lru_kernel(x_ref, wx_ref, wy_ref, cw_ref, cb_ref, wa_ref, ba_ref, wi_ref, bi_ref, lam_ref,
                o_ref, wxb_ref, wyb_ref, perm_ref, unperm_ref, tail_ref, hc_ref, a_ref, b_ref, gy_ref,
                *, tiles_per_seq, tm, width):
    i = pl.program_id(1)
    first = (i % tiles_per_seq) == 0
    seg = tm // SUBLANES
    halo = (CONV_WIDTH - 1) * SUBLANES

    @pl.when(i == 0)
    def _():
        wxb_ref[...] = wx_ref[...].astype(BF16)
        wyb_ref[...] = wy_ref[...].astype(BF16)
        r = lax.broadcasted_iota(I32, (tm, tm), 0)
        c = lax.broadcasted_iota(I32, (tm, tm), 1)
        perm_ref[...] = jnp.where(c == (r & (SUBLANES - 1)) * seg + (r >> 3), 1.0, 0.0).astype(BF16)
        unperm_ref[...] = jnp.where(r == (c & (SUBLANES - 1)) * seg + (c >> 3), 1.0, 0.0).astype(BF16)

    @pl.when(first)
    def _():
        tail_ref[...] = jnp.zeros_like(tail_ref)
        hc_ref[...] = jnp.zeros_like(hc_ref)

    xp = _dot(perm_ref[...], x_ref[...]).astype(BF16)
    px = _dot(xp, wxb_ref[...])
    gy_ref[...] = jax.nn.gelu(_dot(xp, wyb_ref[...]), approximate=True)

    cw = cw_ref[...]
    cb = cb_ref[...]
    taps = [cw[CONV_WIDTH - 1 - d:CONV_WIDTH - d] for d in range(CONV_WIDTH)]
    xa = cb + taps[0] * px
    for d in range(1, CONV_WIDTH):
        back = jnp.concatenate([px[tm - d * SUBLANES:], px[:tm - d * SUBLANES]], axis=0)
        xa = xa + taps[d] * back
    sub = lax.broadcasted_iota(I32, (SUBLANES, width), 0)
    groups = {}
    for g in range(CONV_WIDTH - 1):
        rows = slice(g * SUBLANES, (g + 1) * SUBLANES)
        cur_end = px[tm - halo + g * SUBLANES:tm - halo + (g + 1) * SUBLANES]
        groups[g - (CONV_WIDTH - 1)] = jnp.where(sub == 0, pltpu.roll(tail_ref[rows, :], 1, 0),
                                                 pltpu.roll(cur_end, 1, 0))
        groups[g] = px[rows, :]
    head = []
    for q in range(CONV_WIDTH - 1):
        acc = cb + taps[0] * groups[q]
        for d in range(1, CONV_WIDTH):
            acc = acc + taps[d] * groups[q - d]
        head.append(acc)
    xa = jnp.concatenate(head + [xa[halo:]], axis=0)
    tail_ref[...] = px[tm - halo:]

    xab = xa.astype(BF16)
    r_parts, i_parts = [], []
    for hh in range(width // LRU_BLOCK):
        xh = xab[:, hh * LRU_BLOCK:(hh + 1) * LRU_BLOCK]
        r_parts.append(_dot(xh, wa_ref[hh]))
        i_parts.append(_dot(xh, wi_ref[hh]))
    rg = jax.nn.sigmoid(jnp.concatenate(r_parts, axis=1) + ba_ref[...])
    ig = jax.nn.sigmoid(jnp.concatenate(i_parts, axis=1) + bi_ref[...])

    log_a = -LRU_C * rg * jax.nn.softplus(-lam_ref[...])
    a = jnp.exp(log_a)
    mult = jnp.sqrt(-jnp.tanh(log_a) * (a * a + 1.0))
    row = lax.broadcasted_iota(I32, (tm, width), 0)
    mult = jnp.where(jnp.logical_and(first, row == 0), 1.0, mult)
    a_ref[...] = a
    b_ref[...] = mult * ig * xa

    run_a = jnp.ones((SUBLANES, width), F32)
    run_h = jnp.zeros((SUBLANES, width), F32)
    for q in range(seg):
        rows = slice(q * SUBLANES, (q + 1) * SUBLANES)
        aq = a_ref[rows, :]
        run_h = aq * run_h + b_ref[rows, :]
        run_a = aq * run_a
        b_ref[rows, :] = run_h
        a_ref[rows, :] = run_a

    carry = hc_ref[...]
    carries = []
    for s in range(SUBLANES):
        carries.append(carry)
        carry = run_a[s:s + 1, :] * carry + run_h[s:s + 1, :]
    hc_ref[...] = carry
    carry_in = jnp.concatenate(carries, axis=0)
    for q in range(seg):
        rows = slice(q * SUBLANES, (q + 1) * SUBLANES)
        b_ref[rows, :] = (b_ref[rows, :] + a_ref[rows, :] * carry_in) * gy_ref[rows, :]

    o_ref[...] = _dot(unperm_ref[...], b_ref[...].astype(BF16)).astype(o_ref.dtype)


def _lru_branch(xb, w, conv_w, conv_b, wa, ba, wi, bi, lam, seq, tm, width):
    T, K = xb.shape
    nj = D_LRU // width
    jy0 = OFF_LRU_Y // width
    hp = width // LRU_BLOCK
    kern = functools.partial(_lru_kernel, tiles_per_seq=seq // tm, tm=tm, width=width)
    vec = lambda j, i: (0, j)
    return pl.pallas_call(
        kern,
        grid=(nj, T // tm),
        in_specs=[pl.BlockSpec((tm, K), lambda j, i: (i, 0)),
                  pl.BlockSpec((K, width), lambda j, i: (0, j)),
                  pl.BlockSpec((K, width), lambda j, i: (0, jy0 + j)),
                  pl.BlockSpec((CONV_WIDTH, width), vec),
                  pl.BlockSpec((1, width), vec),
                  pl.BlockSpec((hp, LRU_BLOCK, LRU_BLOCK), lambda j, i: (j, 0, 0)),
                  pl.BlockSpec((1, width), vec),
                  pl.BlockSpec((hp, LRU_BLOCK, LRU_BLOCK), lambda j, i: (j, 0, 0)),
                  pl.BlockSpec((1, width), vec),
                  pl.BlockSpec((1, width), vec)],
        out_specs=pl.BlockSpec((tm, width), lambda j, i: (i, j)),
        out_shape=jax.ShapeDtypeStruct((T, D_LRU), BF16),
        scratch_shapes=[pltpu.VMEM((K, width), BF16), pltpu.VMEM((K, width), BF16),
                        pltpu.VMEM((tm, tm), BF16), pltpu.VMEM((tm, tm), BF16),
                        pltpu.VMEM(((CONV_WIDTH - 1) * SUBLANES, width), F32),
                        pltpu.VMEM((1, width), F32),
                        pltpu.VMEM((tm, width), F32), pltpu.VMEM((tm, width), F32),
                        pltpu.VMEM((tm, width), F32)],
        compiler_params=_params(("parallel", "arbitrary")),
        name="lru_branch",
    )(xb, w, w, conv_w, conv_b.reshape(1, D_LRU), wa, ba.reshape(1, D_LRU), wi, bi.reshape(1, D_LRU),
      lam.reshape(1, D_LRU))


def _ret_kernel(gc_ref, q_ref, k_ref, v_ref, g_ref, gain_ref, dm_ref, xi_ref, zeta_ref, o_ref, st_ref):
    @pl.when(pl.program_id(1) == 0)
    def _():
        st_ref[...] = jnp.zeros_like(st_ref)

    for hd in range(RET_HEADS):
        qc = slice(hd * RET_DK, (hd + 1) * RET_DK)
        vc = slice(hd * RET_DV, (hd + 1) * RET_DV)
        q = q_ref[:, qc]
        k = k_ref[:, qc]
        v = v_ref[:, vc]
        st = st_ref[hd]
        s = lax.dot_general(q, k, (((1,), (1,)), ((), ())), preferred_element_type=F32) * dm_ref[hd]
        o = _dot(s.astype(BF16), v) + _dot(q, st.astype(BF16)) * xi_ref[hd]
        kz = (k.astype(F32) * zeta_ref[hd]).astype(BF16)
        st_ref[hd] = gc_ref[hd] * st + lax.dot_general(kz, v, (((0,), (0,)), ((), ())),
                                                       preferred_element_type=F32)
        mu = jnp.mean(o, axis=-1, keepdims=True)
        oc = o - mu
        var = jnp.mean(oc * oc, axis=-1, keepdims=True)
        oh = oc * lax.rsqrt(var + LN_EPS) * gain_ref[:, vc]
        o_ref[:, vc] = (jax.nn.silu(g_ref[:, vc]) * oh).astype(o_ref.dtype)


def _retention(qk, v, g, gain, batch, seq):
    T = batch * seq
    C = min(RET_CHUNK, seq)
    nchunk = seq // C
    H = RET_HEADS
    log_g = jnp.log1p(-jnp.exp2(-5.0 - jnp.arange(H, dtype=F32)))
    idx = jnp.arange(C, dtype=F32)
    diff = idx[:, None] - idx[None, :]
    dmask = jnp.where(diff >= 0, jnp.exp(jnp.maximum(diff, 0.0)[None] * log_g[:, None, None]), 0.0)
    xi = jnp.exp((idx[None] + 1.0) * log_g[:, None])[:, :, None]
    zeta = jnp.exp((C - 1.0 - idx[None]) * log_g[:, None])[:, :, None]
    g_c = jnp.exp(C * log_g)
    rows = lambda b, n: b * nchunk + n
    whole3 = lambda b, n: (0, 0, 0)
    return pl.pallas_call(
        _ret_kernel,
        grid=(batch, nchunk),
        in_specs=[pl.BlockSpec(memory_space=pltpu.SMEM),
                  pl.BlockSpec((C, D_QK), lambda b, n: (rows(b, n), 0)),
                  pl.BlockSpec((C, D_QK), lambda b, n: (rows(b, n), 1)),
                  pl.BlockSpec((C, D_RV), lambda b, n: (rows(b, n), 0)),
                  pl.BlockSpec((C, D_RV), lambda b, n: (rows(b, n), 0)),
                  pl.BlockSpec((1, D_RV), lambda b, n: (0, 0)),
                  pl.BlockSpec((H, C, C), whole3),
                  pl.BlockSpec((H, C, 1), whole3),
                  pl.BlockSpec((H, C, 1), whole3)],
        out_specs=pl.BlockSpec((C, D_RV), lambda b, n: (rows(b, n), 0)),
        out_shape=jax.ShapeDtypeStruct((T, D_RV), BF16),
        scratch_shapes=[pltpu.VMEM((H, RET_DK, RET_DV), F32)],
        compiler_params=_params(("parallel", "arbitrary")),
        name="retention",
    )(g_c, qk, qk, v, g, gain.reshape(1, D_RV), dmask, xi, zeta)


def _mix_kernel(ua_ref, ub_ref, x_ref, wlo_ref, wro_ref, wga_ref, wgb_ref, bga_ref, bgb_ref, o_ref):
    x = x_ref[...]
    ya = _dot(ua_ref[...], wlo_ref[...])
    yb = _dot(ub_ref[...], wro_ref[...])
    ga = jax.nn.sigmoid(_dot(x, wga_ref[...]) + bga_ref[...])
    gb = jax.nn.sigmoid(_dot(x, wgb_ref[...]) + bgb_ref[...])
    o_ref[...] = (ga * ya + gb * yb).astype(o_ref.dtype)


def _mix(ua, ub, xb, w_lru_out, w_ret_out, w_gates, b_gate, tm, tn):
    T = xb.shape[0]
    nj = D_MODEL // tn
    return pl.pallas_call(
        _mix_kernel,
        grid=(T // tm, nj),
        in_specs=[pl.BlockSpec((tm, D_LRU), lambda i, j: (i, 0)),
                  pl.BlockSpec((tm, D_RV), lambda i, j: (i, 0)),
                  pl.BlockSpec((tm, D_MODEL), lambda i, j: (i, 0)),
                  pl.BlockSpec((D_LRU, tn), lambda i, j: (0, j)),
                  pl.BlockSpec((D_RV, tn), lambda i, j: (0, j)),
                  pl.BlockSpec((D_MODEL, tn), lambda i, j: (0, j)),
                  pl.BlockSpec((D_MODEL, tn), lambda i, j: (0, nj + j)),
                  pl.BlockSpec((1, tn), lambda i, j: (0, j)),
                  pl.BlockSpec((1, tn), lambda i, j: (0, nj + j))],
        out_specs=pl.BlockSpec((tm, tn), lambda i, j: (i, j)),
        out_shape=jax.ShapeDtypeStruct((T, D_MODEL), BF16),
        compiler_params=_params(("parallel", "parallel")),
        name="gated_mix",
    )(ua, ub, xb, w_lru_out, w_ret_out, w_gates, w_gates, b_gate.reshape(1, 2 * D_MODEL),
      b_gate.reshape(1, 2 * D_MODEL))


def _layer_norm(y, g, b):
    mu = jnp.mean(y, axis=-1, keepdims=True)
    yc = y - mu
    var = jnp.mean(yc * yc, axis=-1, keepdims=True)
    return yc * lax.rsqrt(var + LN_EPS) * g + b


def _pack_bf16_pairs(hb):
    lo = lax.bitcast_convert_type(hb[:, :HALF].astype(F32), U32)
    hi = lax.bitcast_convert_type(hb[:, HALF:].astype(F32), U32)
    return (hi & jnp.uint32(0xFFFF0000)) | (lo >> 16)


def _unpack_bf16_pairs(p):
    lo = lax.bitcast_convert_type(p << 16, F32).astype(BF16)
    hi = lax.bitcast_convert_type(p & jnp.uint32(0xFFFF0000), F32).astype(BF16)
    return lo, hi


def _max2(a):
    return jnp.max(jnp.max(a, axis=1, keepdims=True), axis=0, keepdims=True)


def _min2(a):
    return jnp.min(jnp.min(a, axis=1, keepdims=True), axis=0, keepdims=True)


def _sum2(a):
    return jnp.sum(jnp.sum(a, axis=1, keepdims=True), axis=0, keepdims=True)


def _ln1_route_kernel(mix_ref, wo_ref, x_ref, g_ref, b_ref, wr_ref, rb_ref,
                      h_ref, hp_ref, eidx_ref, rank_ref, wsel_ref, cnt_ref, cnt_sc, *, tm):
    i = pl.program_id(0)

    @pl.when(i == 0)
    def _():
        cnt_sc[...] = jnp.zeros_like(cnt_sc)

    y = DN_ALPHA * x_ref[...] + _dot(mix_ref[...], wo_ref[...])
    h = _layer_norm(y, g_ref[...], b_ref[...])
    h_ref[...] = h
    hb = h.astype(BF16)
    hp_ref[...] = _pack_bf16_pairs(hb)

    G, GS = N_GROUPS, GROUP_SIZE
    logits = lax.dot_general(wr_ref[...], hb, (((1,), (1,)), ((), ())), preferred_element_type=F32)
    scores = jax.nn.sigmoid(logits)
    s3 = scores.reshape(G, GS, tm)
    b3 = (scores + rb_ref[...]).reshape(G, GS, tm)
    neg = jnp.float32(-jnp.inf)

    in_grp = lax.broadcasted_iota(I32, (G, GS, tm), 1)
    m1 = jnp.max(b3, axis=1, keepdims=True)
    f1 = jnp.min(jnp.where(b3 == m1, in_grp, GS), axis=1, keepdims=True)
    m2 = jnp.max(jnp.where(in_grp == f1, neg, b3), axis=1, keepdims=True)
    grp = m1 + m2

    gi = lax.broadcasted_iota(I32, (G, 1, tm), 0)
    gkeep = jnp.zeros((G, 1, tm), F32)
    cur = grp
    for _ in range(TOPK_GROUPS):
        m = jnp.max(cur, axis=0, keepdims=True)
        f = jnp.min(jnp.where(cur == m, gi, G), axis=0, keepdims=True)
        hit = gi == f
        gkeep = jnp.where(hit, 1.0, gkeep)
        cur = jnp.where(hit, neg, cur)
    emask = jnp.broadcast_to(gkeep, (G, GS, tm)) > 0.0

    ei = lax.broadcasted_iota(I32, (G, GS, tm), 0) * GS + in_grp
    cur = jnp.where(emask, b3, neg)
    hits, idxs, ws = [], [], []
    for _ in range(TOP_K):
        m = _max2(cur)
        f = _min2(jnp.where(cur == m, ei, N_EXPERTS))
        hit = ei == f
        cur = jnp.where(hit, neg, cur)
        hits.append(hit)
        idxs.append(f)
        ws.append(_sum2(jnp.where(hit, s3, 0.0)))
    wsum = ws[0]
    for r in range(1, TOP_K):
        wsum = wsum + ws[r]

    sel = jnp.zeros((G, GS, tm), F32)
    for hit in hits:
        sel = jnp.where(hit, 1.0, sel)
    sel2 = sel.reshape(N_EXPERTS, tm)
    tri = (lax.broadcasted_iota(I32, (tm, tm), 0) < lax.broadcasted_iota(I32, (tm, tm), 1))
    rank_local = _dot(sel2.astype(BF16), jnp.where(tri, 1.0, 0.0).astype(BF16))
    cnt = cnt_sc[...]
    rank3 = (rank_local + cnt[:, 0:1]).reshape(G, GS, tm)
    cnt_new = cnt + jnp.sum(sel2, axis=1, keepdims=True)
    cnt_sc[...] = cnt_new
    cnt_ref[...] = cnt_new

    for r in range(TOP_K):
        eidx_ref[0, r:r + 1, :] = idxs[r].reshape(1, tm)
        rank_ref[0, r:r + 1, :] = _sum2(jnp.where(hits[r], rank3, 0.0)).reshape(1, tm).astype(I32)
        wsel_ref[r:r + 1, :] = (ws[r] / wsum * ROUTED_SCALE).reshape(1, tm)


def _ln1_route(mixed, w_o, x2, ln_g, ln_b, w_router_t, router_bias, tm):
    T = x2.shape[0]
    kern = functools.partial(_ln1_route_kernel, tm=tm)
    row = lambda i: (i, 0)
    fixed = lambda i: (0, 0)
    col = lambda i: (0, i)
    tile3 = lambda i: (i, 0, 0)
    return pl.pallas_call(
        kern,
        grid=(T // tm,),
        in_specs=[pl.BlockSpec((tm, D_MODEL), row),
                  pl.BlockSpec((D_MODEL, D_MODEL), fixed),
                  pl.BlockSpec((tm, D_MODEL), row),
                  pl.BlockSpec((1, D_MODEL), fixed),
                  pl.BlockSpec((1, D_MODEL), fixed),
                  pl.BlockSpec((N_EXPERTS, D_MODEL), fixed),
                  pl.BlockSpec((N_EXPERTS, 1), fixed)],
        out_specs=[pl.BlockSpec((tm, D_MODEL), row),
                   pl.BlockSpec((tm, HALF), row),
                   pl.BlockSpec((1, TOP_K, tm), tile3),
                   pl.BlockSpec((1, TOP_K, tm), tile3),
                   pl.BlockSpec((TOP_K, tm), col),
                   pl.BlockSpec((N_EXPERTS, LANES), fixed)],
        out_shape=[jax.ShapeDtypeStruct((T, D_MODEL), F32),
                   jax.ShapeDtypeStruct((T, HALF), U32),
                   jax.ShapeDtypeStruct((T // tm, TOP_K, tm), I32),
                   jax.ShapeDtypeStruct((T // tm, TOP_K, tm), I32),
                   jax.ShapeDtypeStruct((TOP_K, T), F32),
                   jax.ShapeDtypeStruct((N_EXPERTS, LANES), F32)],
        scratch_shapes=[pltpu.VMEM((N_EXPERTS, LANES), F32)],
        compiler_params=_params(("arbitrary",)),
        name="ln1_route",
    )(mixed, w_o, x2, ln_g.reshape(1, D_MODEL), ln_b.reshape(1, D_MODEL), w_router_t,
      router_bias.reshape(N_EXPERTS, 1))


def _dest_kernel(ps_ref, eidx_ref, rank_ref, dest_ref):
    eidx = eidx_ref[...]
    dest = rank_ref[...]
    for e in range(N_EXPERTS):
        dest = dest + jnp.where(eidx == e, ps_ref[e], 0)
    dest_ref[...] = dest


def _dest_rows(pad_start, eidx3, rank3):
    vmem = pl.BlockSpec(memory_space=pltpu.VMEM)
    return pl.pallas_call(
        _dest_kernel,
        in_specs=[pl.BlockSpec(memory_space=pltpu.SMEM), vmem, vmem],
        out_specs=vmem,
        out_shape=jax.ShapeDtypeStruct(eidx3.shape, I32),
        name="moe_dest",
    )(pad_start, eidx3, rank3)


def _row_copy(src_ref, src_row, dst_ref, dst_row, sem):
    return pltpu.make_async_copy(src_ref.at[pl.ds(src_row, 1)], dst_ref.at[pl.ds(dst_row, 1)], sem)


def _dispatch_kernel(dest_ref, fill_ref, pe_ref, hp_ref, xs_ref, zero_ref, sem, zsem, *, tm):
    @pl.when(pl.program_id(0) == 0)
    def _():
        zero_ref[...] = jnp.zeros_like(zero_ref)

        def fill_expert(e, n):
            lo = fill_ref[e]
            hi = pe_ref[e]

            def fill_row(r, c):
                _row_copy(zero_ref, 0, xs_ref, r, zsem).start()
                return c

            lax.fori_loop(lo, hi, fill_row, 0)
            return n + (hi - lo)

        n_fill = lax.fori_loop(0, N_EXPERTS, fill_expert, 0)

        def drain_fill(r, c):
            _row_copy(zero_ref, 0, xs_ref, 0, zsem).wait()
            return c

        lax.fori_loop(0, n_fill, drain_fill, 0)

    def issue(t, c):
        for k in range(TOP_K):
            _row_copy(hp_ref, t, xs_ref, dest_ref[0, k, t], sem).start()
        return c

    def drain(t, c):
        for k in range(TOP_K):
            _row_copy(hp_ref, 0, xs_ref, 0, sem).wait()
        return c

    lax.fori_loop(0, tm, issue, 0)
    lax.fori_loop(0, tm, drain, 0)


def _dispatch(dest3, fill_start, pad_end, hp, n_rows):
    T = hp.shape[0]
    tm = dest3.shape[2]
    kern = functools.partial(_dispatch_kernel, tm=tm)
    smem = pl.BlockSpec(memory_space=pltpu.SMEM)
    return pl.pallas_call(
        kern,
        grid=(T // tm,),
        in_specs=[pl.BlockSpec((1, TOP_K, tm), lambda i: (i, 0, 0), memory_space=pltpu.SMEM),
                  smem, smem,
                  pl.BlockSpec((tm, HALF), lambda i: (i, 0))],
        out_specs=pl.BlockSpec(memory_space=pl.ANY),
        out_shape=jax.ShapeDtypeStruct((n_rows, HALF), U32),
        scratch_shapes=[pltpu.VMEM((SUBLANES, HALF), U32), pltpu.SemaphoreType.DMA,
                        pltpu.SemaphoreType.DMA],
        compiler_params=_params(("arbitrary",)),
        name="moe_dispatch",
    )(dest3, fill_start, pad_end, hp)


def _experts_kernel(be_ref, nb_ref, xs_ref, wg_ref, wu_ref, wd_ref, ys_ref, wgb_ref, wub_ref, wdb_ref):
    b = pl.program_id(0)
    new_expert = jnp.logical_or(b == 0, be_ref[b] != be_ref[jnp.maximum(b - 1, 0)])

    @pl.when(jnp.logical_and(b < nb_ref[0], new_expert))
    def _():
        wgb_ref[...] = wg_ref[0].astype(BF16)
        wub_ref[...] = wu_ref[0].astype(BF16)
        wdb_ref[...] = wd_ref[0].astype(BF16)

    @pl.when(b < nb_ref[0])
    def _():
        lo, hi = _unpack_bf16_pairs(xs_ref[...])
        gate = _dot(lo, wgb_ref[:HALF, :]) + _dot(hi, wgb_ref[HALF:, :])
        up = _dot(lo, wub_ref[:HALF, :]) + _dot(hi, wub_ref[HALF:, :])
        hb = (jax.nn.silu(gate) * up).astype(BF16)
        ys_ref[...] = _dot(hb, wdb_ref[...])


def _experts(block_e, nb_used, xs, wg, wu, wd):
    n_rows = xs.shape[0]
    nb = n_rows // MOE_ROWS
    rows = lambda b, be, nbu: (jnp.minimum(b, nbu[0] - 1), 0)
    wsel = lambda b, be, nbu: (be[b], 0, 0)
    grid_spec = pltpu.PrefetchScalarGridSpec(
        num_scalar_prefetch=2,
        grid=(nb,),
        in_specs=[pl.BlockSpec((MOE_ROWS, HALF), rows),
                  pl.BlockSpec((1, D_MODEL, D_EXPERT), wsel),
                  pl.BlockSpec((1, D_MODEL, D_EXPERT), wsel),
                  pl.BlockSpec((1, D_EXPERT, D_MODEL), wsel)],
        out_specs=pl.BlockSpec((MOE_ROWS, D_MODEL), rows),
        scratch_shapes=[pltpu.VMEM((D_MODEL, D_EXPERT), BF16), pltpu.VMEM((D_MODEL, D_EXPERT), BF16),
                        pltpu.VMEM((D_EXPERT, D_MODEL), BF16)],
    )
    return pl.pallas_call(
        _experts_kernel,
        grid_spec=grid_spec,
        out_shape=jax.ShapeDtypeStruct((n_rows, D_MODEL), F32),
        compiler_params=_params(("arbitrary",)),
        name="moe_experts",
    )(block_e, nb_used, xs, wg, wu, wd)


def _combine_kernel(dest_ref, w_ref, h_ref, wgs_ref, wus_ref, wds_ref, g_ref, b_ref,
                    ys_ref, o_ref, buf_ref, sem, *, tm):
    def issue(t, c):
        for k in range(TOP_K):
            _row_copy(ys_ref, dest_ref[0, k, t], buf_ref.at[k], t, sem).start()
        return c

    def drain(t, c):
        for k in range(TOP_K):
            _row_copy(ys_ref, 0, buf_ref.at[k], 0, sem).wait()
        return c

    lax.fori_loop(0, tm, issue, 0)

    h = h_ref[...]
    hb = h.astype(BF16)
    hid = (jax.nn.silu(_dot(hb, wgs_ref[...])) * _dot(hb, wus_ref[...])).astype(BF16)
    shared = _dot(hid, wds_ref[...])

    lax.fori_loop(0, tm, drain, 0)

    w = w_ref[...]
    routed = buf_ref[0] * w[:, 0:1]
    for k in range(1, TOP_K):
        routed = routed + buf_ref[k] * w[:, k:k + 1]
    o_ref[...] = _layer_norm(DN_ALPHA * h + (routed + shared), g_ref[...], b_ref[...])


def _combine(dest3, w_tok, h, wgs, wus, wds, ln_g, ln_b, ys3, tm):
    T = h.shape[0]
    per = dest3.shape[2] // tm
    kern = functools.partial(_combine_kernel, tm=tm)
    row = lambda i: (i, 0)
    fixed = lambda i: (0, 0)
    return pl.pallas_call(
        kern,
        grid=(T // tm,),
        in_specs=[pl.BlockSpec((1, TOP_K, tm), lambda i: (i // per, 0, i % per),
                               memory_space=pltpu.SMEM),
                  pl.BlockSpec((tm, TOP_K), row),
                  pl.BlockSpec((tm, D_MODEL), row),
                  pl.BlockSpec((D_MODEL, D_SHARED), fixed),
                  pl.BlockSpec((D_MODEL, D_SHARED), fixed),
                  pl.BlockSpec((D_SHARED, D_MODEL), fixed),
                  pl.BlockSpec((1, D_MODEL), fixed),
                  pl.BlockSpec((1, D_MODEL), fixed),
                  pl.BlockSpec(memory_space=pl.ANY)],
        out_specs=pl.BlockSpec((tm, D_MODEL), row),
        out_shape=jax.ShapeDtypeStruct((T, D_MODEL), F32),
        scratch_shapes=[pltpu.VMEM((TOP_K, tm, D_MODEL), F32), pltpu.SemaphoreType.DMA],
        compiler_params=_params(("arbitrary",)),
        name="moe_combine",
    )(dest3, w_tok, h, wgs, wus, wds, ln_g.reshape(1, D_MODEL), ln_b.reshape(1, D_MODEL), ys3)


def _tile(n, pref):
    t = min(n, pref)
    assert n % t == 0, (n, pref)
    return t


def _layer(x, w_in, conv_w, conv_b, lru_wa, lru_ba, lru_wi, lru_bi, lru_lambda, ret_gn_gain,
           w_lru_out, w_ret_out, b_gate, w_o, ln1_g, ln1_b, w_router, router_bias,
           w_gate_e, w_up_e, w_down_e, w_gate_s, w_up_s, w_down_s, ln2_g, ln2_b):
    B, S, D = x.shape
    T = B * S
    x2 = x.reshape(T, D)
    xb = x2.astype(BF16)

    half = RET_DK // 2
    freq = ROPE_THETA ** (-jnp.arange(half, dtype=F32) / half)
    ang = jnp.arange(S, dtype=I32).astype(F32)[:, None] * freq
    cos, sin = jnp.cos(ang), jnp.sin(ang)

    tm_big = _tile(S, 1024)
    qk = _proj_qk(xb, w_in, cos, sin, S, tm_big, 1024)
    v = _proj_plain(xb, w_in, OFF_V, D_RV, BF16, tm_big, 1024, "v_proj")
    g = _proj_plain(xb, w_in, OFF_G, D_RV, F32, tm_big, 1024, "g_proj")

    ua = _lru_branch(xb, w_in, conv_w, conv_b, lru_wa.astype(BF16), lru_ba, lru_wi.astype(BF16),
                     lru_bi, lru_lambda, S, _tile(S, 256), 512)
    ub = _retention(qk, v, g, ret_gn_gain, B, S)

    mixed = _mix(ua, ub, xb, w_lru_out.astype(BF16), w_ret_out.astype(BF16),
                 w_in[:, OFF_GA:].astype(BF16), b_gate, _tile(T, 512), 512)

    tm_r = _tile(T, 256)
    h, hp3, eidx, rank, wsel, cnt = _ln1_route(mixed, w_o.astype(BF16), x2, ln1_g, ln1_b,
                                               w_router.T.astype(BF16), router_bias, tm_r)

    counts = cnt[:, 0].astype(I32)
    padded = (counts + MOE_ROWS - 1) // MOE_ROWS * MOE_ROWS
    pad_end = jnp.cumsum(padded)
    pad_start = pad_end - padded
    nb = (T * TOP_K + N_EXPERTS * (MOE_ROWS - 1) + MOE_ROWS - 1) // MOE_ROWS
    n_rows = nb * MOE_ROWS
    block_row0 = jnp.arange(nb, dtype=I32) * MOE_ROWS
    block_e = jnp.minimum(jnp.sum((pad_end[None, :] <= block_row0[:, None]).astype(I32), axis=1),
                          N_EXPERTS - 1)
    nb_used = (pad_end[-1:] // MOE_ROWS).astype(I32)

    dest = _dest_rows(pad_start, eidx, rank)
    xs3 = _dispatch(dest, pad_start + counts, pad_end, hp3, n_rows)
    ys3 = _experts(block_e, nb_used, xs3, w_gate_e, w_up_e, w_down_e)
    out = _combine(dest, wsel.T, h, w_gate_s.astype(BF16), w_up_s.astype(BF16),
                   w_down_s.astype(BF16), ln2_g, ln2_b, ys3, _tile(tm_r, 128))
    return out.reshape(B, S, D)


def kernel(x, w_in, conv_w, conv_b, lru_wa, lru_ba, lru_wi, lru_bi, lru_lambda, ret_gn_gain, w_lru_out, w_ret_out, b_gate, w_o, ln1_g, ln1_b, w_router, router_bias, w_gate_e, w_up_e, w_down_e, w_gate_s, w_up_s, w_down_s, ln2_g, ln2_b):
    assert DEPTH == 1 and w_in.shape[0] == DEPTH
    args = (w_in, conv_w, conv_b, lru_wa, lru_ba, lru_wi, lru_bi, lru_lambda, ret_gn_gain,
            w_lru_out, w_ret_out, b_gate, w_o, ln1_g, ln1_b, w_router, router_bias,
            w_gate_e, w_up_e, w_down_e, w_gate_s, w_up_s, w_down_s, ln2_g, ln2_b)
    return _layer(x, *[a[0] for a in args])
```

```python
import functools

import jax
import jax.numpy as jnp
from jax import lax
from jax.experimental import pallas as pl
from jax.experimental.pallas import tpu as pltpu

F32 = jnp.float32
BF16 = jnp.bfloat16
I32 = jnp.int32
U32 = jnp.uint32

D_MODEL = 2048
D_LRU = 2048
LRU_HEADS = 16
LRU_BLOCK = D_LRU // LRU_HEADS
CONV_WIDTH = 4
LRU_C = 8.0
RET_HEADS = 8
RET_DK = 256
RET_DV = 512
D_QK = RET_HEADS * RET_DK
D_RV = RET_HEADS * RET_DV
ROPE_THETA = 10000.0
N_EXPERTS = 64
TOP_K = 8
N_GROUPS = 8
GROUP_SIZE = N_EXPERTS // N_GROUPS
TOPK_GROUPS = 4
D_EXPERT = 512
D_SHARED = 512
ROUTED_SCALE = 2.5
DEPTH = 1
DN_ALPHA = (2.0 * DEPTH) ** 0.25
LN_EPS = 1e-5

OFF_LRU_X = 0
OFF_LRU_Y = OFF_LRU_X + D_LRU
OFF_Q = OFF_LRU_Y + D_LRU
OFF_K = OFF_Q + D_QK
OFF_V = OFF_K + D_QK
OFF_G = OFF_V + D_RV
OFF_GA = OFF_G + D_RV
OFF_GB = OFF_GA + D_MODEL

V7X_VMEM_LIMIT = 56 * 1024 * 1024
LANES = 128
SUBLANES = 8
MOE_ROWS = 512
RET_CHUNK = 256
HALF = D_MODEL // 2


def _params(sem):
    return pltpu.CompilerParams(dimension_semantics=sem, vmem_limit_bytes=V7X_VMEM_LIMIT)


def _dot(a, b):
    return jnp.dot(a, b, preferred_element_type=F32)


def _cast_weight(w_ref, wb_ref):
    @pl.when(pl.program_id(1) == 0)
    def _():
        wb_ref[...] = w_ref[...].astype(BF16)


def _mm_kernel(x_ref, w_ref, o_ref, wb_ref):
    _cast_weight(w_ref, wb_ref)
    o_ref[...] = _dot(x_ref[...], wb_ref[...]).astype(o_ref.dtype)


def _proj_plain(xb, w, col0, ncols, out_dtype, tm, tn, name):
    T, K = xb.shape
    jb0 = col0 // tn
    return pl.pallas_call(
        _mm_kernel,
        grid=(ncols // tn, T // tm),
        in_specs=[pl.BlockSpec((tm, K), lambda j, i: (i, 0)),
                  pl.BlockSpec((K, tn), lambda j, i: (0, jb0 + j))],
        out_specs=pl.BlockSpec((tm, tn), lambda j, i: (i, j)),
        out_shape=jax.ShapeDtypeStruct((T, ncols), out_dtype),
        scratch_shapes=[pltpu.VMEM((K, tn), BF16)],
        compiler_params=_params(("parallel", "arbitrary")),
        name=name,
    )(xb, w)


def _qk_kernel(x_ref, w_ref, cos_ref, sin_ref, o_ref, wb_ref, *, heads_per_tile, k_tile0):
    _cast_weight(w_ref, wb_ref)
    j = pl.program_id(0)
    acc = _dot(x_ref[...], wb_ref[...])
    scale = jnp.where(j >= k_tile0, RET_DK ** -0.5, 1.0).astype(F32)
    cos = cos_ref[...]
    sin = sin_ref[...]
    half = RET_DK // 2
    for hh in range(heads_per_tile):
        c0 = hh * RET_DK
        t1 = acc[:, c0:c0 + half]
        t2 = acc[:, c0 + half:c0 + RET_DK]
        o_ref[:, c0:c0 + half] = ((t1 * cos - t2 * sin) * scale).astype(o_ref.dtype)
        o_ref[:, c0 + half:c0 + RET_DK] = ((t1 * sin + t2 * cos) * scale).astype(o_ref.dtype)


def _proj_qk(xb, w, cos, sin, seq, tm, tn):
    T, K = xb.shape
    ncols = 2 * D_QK
    jb0 = OFF_Q // tn
    n_seq_tiles = seq // tm
    kern = functools.partial(_qk_kernel, heads_per_tile=tn // RET_DK, k_tile0=D_QK // tn)
    return pl.pallas_call(
        kern,
        grid=(ncols // tn, T // tm),
        in_specs=[pl.BlockSpec((tm, K), lambda j, i: (i, 0)),
                  pl.BlockSpec((K, tn), lambda j, i: (0, jb0 + j)),
                  pl.BlockSpec((tm, RET_DK // 2), lambda j, i: (i % n_seq_tiles, 0)),
                  pl.BlockSpec((tm, RET_DK // 2), lambda j, i: (i % n_seq_tiles, 0))],
        out_specs=pl.BlockSpec((tm, tn), lambda j, i: (i, j)),
        out_shape=jax.ShapeDtypeStruct((T, ncols), BF16),
        scratch_shapes=[pltpu.VMEM((K, tn), BF16)],
        compiler_params=_params(("parallel", "arbitrary")),
        name="qk_rope",
    )(xb, w, cos, sin)


def _lru_kernel(x_ref, wx_ref, wy_ref, cw_ref, cb_ref, wa_ref, ba_ref, wi_ref, bi_ref, lam_ref,
                o_ref, wxb_ref, wyb_ref, perm_ref, unperm_ref, tail_ref, hc_ref, a_ref, b_ref, gy_ref,
                *, tiles_per_seq, tm, width):
    i = pl.program_id(1)
    first = (i % tiles_per_seq) == 0
    seg = tm // SUBLANES
    halo = (CONV_WIDTH - 1) * SUBLANES

    @pl.when(i == 0)
    def _():
        wxb_ref[...] = wx_ref[...].astype(BF16)
        wyb_ref[...] = wy_ref[...].astype(BF16)
        r = lax.broadcasted_iota(I32, (tm, tm), 0)
        c = lax.broadcasted_iota(I32, (tm, tm), 1)
        perm_ref[...] = jnp.where(c == (r & (SUBLANES - 1)) * seg + (r >> 3), 1.0, 0.0).astype(BF16)
        unperm_ref[...] = jnp.where(r == (c & (SUBLANES - 1)) * seg + (c >> 3), 1.0, 0.0).astype(BF16)

    @pl.when(first)
    def _():
        tail_ref[...] = jnp.zeros_like(tail_ref)
        hc_ref[...] = jnp.zeros_like(hc_ref)

    xp = _dot(perm_ref[...], x_ref[...]).astype(BF16)
    px = _dot(xp, wxb_ref[...])
    gy_ref[...] = jax.nn.gelu(_dot(xp, wyb_ref[...]), approximate=True)

    cw = cw_ref[...]
    cb = cb_ref[...]
    taps = [cw[CONV_WIDTH - 1 - d:CONV_WIDTH - d] for d in range(CONV_WIDTH)]
    xa = cb + taps[0] * px
    for d in range(1, CONV_WIDTH):
        back = jnp.concatenate([px[tm - d * SUBLANES:], px[:tm - d * SUBLANES]], axis=0)
        xa = xa + taps[d] * back
    sub = lax.broadcasted_iota(I32, (SUBLANES, width), 0)
    groups = {}
    for g in range(CONV_WIDTH - 1):
        rows = slice(g * SUBLANES, (g + 1) * SUBLANES)
        cur_end = px[tm - halo + g * SUBLANES:tm - halo + (g + 1) * SUBLANES]
        groups[g - (CONV_WIDTH - 1)] = jnp.where(sub == 0, pltpu.roll(tail_ref[rows, :], 1, 0),
                                                 pltpu.roll(cur_end, 1, 0))
        groups[g] = px[rows, :]
    head = []
    for q in range(CONV_WIDTH - 1):
        acc = cb + taps[0] * groups[q]
        for d in range(1, CONV_WIDTH):
            acc = acc + taps[d] * groups[q - d]
        head.append(acc)
    xa = jnp.concatenate(head + [xa[halo:]], axis=0)
    tail_ref[...] = px[tm - halo:]

    xab = xa.astype(BF16)
    r_parts, i_parts = [], []
    for hh in range(width // LRU_BLOCK):
        xh = xab[:, hh * LRU_BLOCK:(hh + 1) * LRU_BLOCK]
        r_parts.append(_dot(xh, wa_ref[hh]))
        i_parts.append(_dot(xh, wi_ref[hh]))
    rg = jax.nn.sigmoid(jnp.concatenate(r_parts, axis=1) + ba_ref[...])
    ig = jax.nn.sigmoid(jnp.concatenate(i_parts, axis=1) + bi_ref[...])

    log_a = -LRU_C * rg * jax.nn.softplus(-lam_ref[...])
    a = jnp.exp(log_a)
    mult = jnp.sqrt(-jnp.tanh(log_a) * (a * a + 1.0))
    row = lax.broadcasted_iota(I32, (tm, width), 0)
    mult = jnp.where(jnp.logical_and(first, row == 0), 1.0, mult)
    a_ref[...] = a
    b_ref[...] = mult * ig * xa

    run_a = jnp.ones((SUBLANES, width), F32)
    run_h = jnp.zeros((SUBLANES, width), F32)
    for q in range(seg):
        rows = slice(q * SUBLANES, (q + 1) * SUBLANES)
        aq = a_ref[rows, :]
        run_h = aq * run_h + b_ref[rows, :]
        run_a = aq * run_a
        b_ref[rows, :] = run_h
        a_ref[rows, :] = run_a

    carry = hc_ref[...]
    carries = []
    for s in range(SUBLANES):
        carries.append(carry)
        carry = run_a[s:s + 1, :] * carry + run_h[s:s + 1, :]
    hc_ref[...] = carry
    carry_in = jnp.concatenate(carries, axis=0)
    for q in range(seg):
        rows = slice(q * SUBLANES, (q + 1) * SUBLANES)
        b_ref[rows, :] = (b_ref[rows, :] + a_ref[rows, :] * carry_in) * gy_ref[rows, :]

    o_ref[...] = _dot(unperm_ref[...], b_ref[...].astype(BF16)).astype(o_ref.dtype)


def _lru_branch(xb, w, conv_w, conv_b, wa, ba, wi, bi, lam, seq, tm, width):
    T, K = xb.shape
    nj = D_LRU // width
    jy0 = OFF_LRU_Y // width
    hp = width // LRU_BLOCK
    kern = functools.partial(_lru_kernel, tiles_per_seq=seq // tm, tm=tm, width=width)
    vec = lambda j, i: (0, j)
    return pl.pallas_call(
        kern,
        grid=(nj, T // tm),
        in_specs=[pl.BlockSpec((tm, K), lambda j, i: (i, 0)),
                  pl.BlockSpec((K, width), lambda j, i: (0, j)),
                  pl.BlockSpec((K, width), lambda j, i: (0, jy0 + j)),
                  pl.BlockSpec((CONV_WIDTH, width), vec),
                  pl.BlockSpec((1, width), vec),
                  pl.BlockSpec((hp, LRU_BLOCK, LRU_BLOCK), lambda j, i: (j, 0, 0)),
                  pl.BlockSpec((1, width), vec),
                  pl.BlockSpec((hp, LRU_BLOCK, LRU_BLOCK), lambda j, i: (j, 0, 0)),
                  pl.BlockSpec((1, width), vec),
                  pl.BlockSpec((1, width), vec)],
        out_specs=pl.BlockSpec((tm, width), lambda j, i: (i, j)),
        out_shape=jax.ShapeDtypeStruct((T, D_LRU), BF16),
        scratch_shapes=[pltpu.VMEM((K, width), BF16), pltpu.VMEM((K, width), BF16),
                        pltpu.VMEM((tm, tm), BF16), pltpu.VMEM((tm, tm), BF16),
                        pltpu.VMEM(((CONV_WIDTH - 1) * SUBLANES, width), F32),
                        pltpu.VMEM((1, width), F32),
                        pltpu.VMEM((tm, width), F32), pltpu.VMEM((tm, width), F32),
                        pltpu.VMEM((tm, width), F32)],
        compiler_params=_params(("parallel", "arbitrary")),
        name="lru_branch",
    )(xb, w, w, conv_w, conv_b.reshape(1, D_LRU), wa, ba.reshape(1, D_LRU), wi, bi.reshape(1, D_LRU),
      lam.reshape(1, D_LRU))


def _ret_kernel(gc_ref, q_ref, k_ref, v_ref, g_ref, gain_ref, dm_ref, xi_ref, zeta_ref, o_ref, st_ref):
    @pl.when(pl.program_id(1) == 0)
    def _():
        st_ref[...] = jnp.zeros_like(st_ref)

    for hd in range(RET_HEADS):
        qc = slice(hd * RET_DK, (hd + 1) * RET_DK)
        vc = slice(hd * RET_DV, (hd + 1) * RET_DV)
        q = q_ref[:, qc]
        k = k_ref[:, qc]
        v = v_ref[:, vc]
        st = st_ref[hd]
        s = lax.dot_general(q, k, (((1,), (1,)), ((), ())), preferred_element_type=F32) * dm_ref[hd]
        o = _dot(s.astype(BF16), v) + _dot(q, st.astype(BF16)) * xi_ref[hd]
        kz = (k.astype(F32) * zeta_ref[hd]).astype(BF16)
        st_ref[hd] = gc_ref[hd] * st + lax.dot_general(kz, v, (((0,), (0,)), ((), ())),
                                                       preferred_element_type=F32)
        mu = jnp.mean(o, axis=-1, keepdims=True)
        oc = o - mu
        var = jnp.mean(oc * oc, axis=-1, keepdims=True)
        oh = oc * lax.rsqrt(var + LN_EPS) * gain_ref[:, vc]
        o_ref[:, vc] = (jax.nn.silu(g_ref[:, vc]) * oh).astype(o_ref.dtype)


def _retention(qk, v, g, gain, batch, seq):
    T = batch * seq
    C = min(RET_CHUNK, seq)
    nchunk = seq // C
    H = RET_HEADS
    log_g = jnp.log1p(-jnp.exp2(-5.0 - jnp.arange(H, dtype=F32)))
    idx = jnp.arange(C, dtype=F32)
    diff = idx[:, None] - idx[None, :]
    dmask = jnp.where(diff >= 0, jnp.exp(jnp.maximum(diff, 0.0)[None] * log_g[:, None, None]), 0.0)
    xi = jnp.exp((idx[None] + 1.0) * log_g[:, None])[:, :, None]
    zeta = jnp.exp((C - 1.0 - idx[None]) * log_g[:, None])[:, :, None]
    g_c = jnp.exp(C * log_g)
    rows = lambda b, n: b * nchunk + n
    whole3 = lambda b, n: (0, 0, 0)
    return pl.pallas_call(
        _ret_kernel,
        grid=(batch, nchunk),
        in_specs=[pl.BlockSpec(memory_space=pltpu.SMEM),
                  pl.BlockSpec((C, D_QK), lambda b, n: (rows(b, n), 0)),
                  pl.BlockSpec((C, D_QK), lambda b, n: (rows(b, n), 1)),
                  pl.BlockSpec((C, D_RV), lambda b, n: (rows(b, n), 0)),
                  pl.BlockSpec((C, D_RV), lambda b, n: (rows(b, n), 0)),
                  pl.BlockSpec((1, D_RV), lambda b, n: (0, 0)),
                  pl.BlockSpec((H, C, C), whole3),
                  pl.BlockSpec((H, C, 1), whole3),
                  pl.BlockSpec((H, C, 1), whole3)],
        out_specs=pl.BlockSpec((C, D_RV), lambda b, n: (rows(b, n), 0)),
        out_shape=jax.ShapeDtypeStruct((T, D_RV), BF16),
        scratch_shapes=[pltpu.VMEM((H, RET_DK, RET_DV), F32)],
        compiler_params=_params(("parallel", "arbitrary")),
        name="retention",
    )(g_c, qk, qk, v, g, gain.reshape(1, D_RV), dmask, xi, zeta)


def _mix_kernel(ua_ref, ub_ref, x_ref, wlo_ref, wro_ref, wga_ref, wgb_ref, bga_ref, bgb_ref, o_ref):
    x = x_ref[...]
    ya = _dot(ua_ref[...], wlo_ref[...])
    yb = _dot(ub_ref[...], wro_ref[...])
    ga = jax.nn.sigmoid(_dot(x, wga_ref[...]) + bga_ref[...])
    gb = jax.nn.sigmoid(_dot(x, wgb_ref[...]) + bgb_ref[...])
    o_ref[...] = (ga * ya + gb * yb).astype(o_ref.dtype)


def _mix(ua, ub, xb, w_lru_out, w_ret_out, w_gates, b_gate, tm, tn):
    T = xb.shape[0]
    nj = D_MODEL // tn
    return pl.pallas_call(
        _mix_kernel,
        grid=(T // tm, nj),
        in_specs=[pl.BlockSpec((tm, D_LRU), lambda i, j: (i, 0)),
                  pl.BlockSpec((tm, D_RV), lambda i, j: (i, 0)),
                  pl.BlockSpec((tm, D_MODEL), lambda i, j: (i, 0)),
                  pl.BlockSpec((D_LRU, tn), lambda i, j: (0, j)),
                  pl.BlockSpec((D_RV, tn), lambda i, j: (0, j)),
                  pl.BlockSpec((D_MODEL, tn), lambda i, j: (0, j)),
                  pl.BlockSpec((D_MODEL, tn), lambda i, j: (0, nj + j)),
                  pl.BlockSpec((1, tn), lambda i, j: (0, j)),
                  pl.BlockSpec((1, tn), lambda i, j: (0, nj + j))],
        out_specs=pl.BlockSpec((tm, tn), lambda i, j: (i, j)),
        out_shape=jax.ShapeDtypeStruct((T, D_MODEL), BF16),
        compiler_params=_params(("parallel", "parallel")),
        name="gated_mix",
    )(ua, ub, xb, w_lru_out, w_ret_out, w_gates, w_gates, b_gate.reshape(1, 2 * D_MODEL),
      b_gate.reshape(1, 2 * D_MODEL))


def _layer_norm(y, g, b):
    mu = jnp.mean(y, axis=-1, keepdims=True)
    yc = y - mu
    var = jnp.mean(yc * yc, axis=-1, keepdims=True)
    return yc * lax.rsqrt(var + LN_EPS) * g + b


def _pack_bf16_pairs(hb):
    lo = lax.bitcast_convert_type(hb[:, :HALF].astype(F32), U32)
    hi = lax.bitcast_convert_type(hb[:, HALF:].astype(F32), U32)
    return (hi & jnp.uint32(0xFFFF0000)) | (lo >> 16)


def _unpack_bf16_pairs(p):
    lo = lax.bitcast_convert_type(p << 16, F32).astype(BF16)
    hi = lax.bitcast_convert_type(p & jnp.uint32(0xFFFF0000), F32).astype(BF16)
    return lo, hi


def _max2(a):
    return jnp.max(jnp.max(a, axis=1, keepdims=True), axis=0, keepdims=True)


def _min2(a):
    return jnp.min(jnp.min(a, axis=1, keepdims=True), axis=0, keepdims=True)


def _sum2(a):
    return jnp.sum(jnp.sum(a, axis=1, keepdims=True), axis=0, keepdims=True)


def _ln1_route_kernel(mix_ref, wo_ref, x_ref, g_ref, b_ref, wr_ref, rb_ref,
                      h_ref, hp_ref, eidx_ref, rank_ref, wsel_ref, cnt_ref, cnt_sc, *, tm):
    i = pl.program_id(0)

    @pl.when(i == 0)
    def _():
        cnt_sc[...] = jnp.zeros_like(cnt_sc)

    y = DN_ALPHA * x_ref[...] + _dot(mix_ref[...], wo_ref[...])
    h = _layer_norm(y, g_ref[...], b_ref[...])
    h_ref[...] = h
    hb = h.astype(BF16)
    hp_ref[...] = _pack_bf16_pairs(hb)

    G, GS = N_GROUPS, GROUP_SIZE
    logits = lax.dot_general(wr_ref[...], hb, (((1,), (1,)), ((), ())), preferred_element_type=F32)
    scores = jax.nn.sigmoid(logits)
    s3 = scores.reshape(G, GS, tm)
    b3 = (scores + rb_ref[...]).reshape(G, GS, tm)
    neg = jnp.float32(-jnp.inf)

    in_grp = lax.broadcasted_iota(I32, (G, GS, tm), 1)
    m1 = jnp.max(b3, axis=1, keepdims=True)
    f1 = jnp.min(jnp.where(b3 == m1, in_grp, GS), axis=1, keepdims=True)
    m2 = jnp.max(jnp.where(in_grp == f1, neg, b3), axis=1, keepdims=True)
    grp = m1 + m2

    gi = lax.broadcasted_iota(I32, (G, 1, tm), 0)
    gkeep = jnp.zeros((G, 1, tm), F32)
    cur = grp
    for _ in range(TOPK_GROUPS):
        m = jnp.max(cur, axis=0, keepdims=True)
        f = jnp.min(jnp.where(cur == m, gi, G), axis=0, keepdims=True)
        hit = gi == f
        gkeep = jnp.where(hit, 1.0, gkeep)
        cur = jnp.where(hit, neg, cur)
    emask = jnp.broadcast_to(gkeep, (G, GS, tm)) > 0.0

    ei = lax.broadcasted_iota(I32, (G, GS, tm), 0) * GS + in_grp
    cur = jnp.where(emask, b3, neg)
    hits, idxs, ws = [], [], []
    for _ in range(TOP_K):
        m = _max2(cur)
        f = _min2(jnp.where(cur == m, ei, N_EXPERTS))
        hit = ei == f
        cur = jnp.where(hit, neg, cur)
        hits.append(hit)
        idxs.append(f)
        ws.append(_sum2(jnp.where(hit, s3, 0.0)))
    wsum = ws[0]
    for r in range(1, TOP_K):
        wsum = wsum + ws[r]

    sel = jnp.zeros((G, GS, tm), F32)
    for hit in hits:
        sel = jnp.where(hit, 1.0, sel)
    sel2 = sel.reshape(N_EXPERTS, tm)
    tri = (lax.broadcasted_iota(I32, (tm, tm), 0) < lax.broadcasted_iota(I32, (tm, tm), 1))
    rank_local = _dot(sel2.astype(BF16), jnp.where(tri, 1.0, 0.0).astype(BF16))
    cnt = cnt_sc[...]
    rank3 = (rank_local + cnt[:, 0:1]).reshape(G, GS, tm)
    cnt_new = cnt + jnp.sum(sel2, axis=1, keepdims=True)
    cnt_sc[...] = cnt_new
    cnt_ref[...] = cnt_new

    for r in range(TOP_K):
        eidx_ref[0, r:r + 1, :] = idxs[r].reshape(1, tm)
        rank_ref[0, r:r + 1, :] = _sum2(jnp.where(hits[r], rank3, 0.0)).reshape(1, tm).astype(I32)
        wsel_ref[r:r + 1, :] = (ws[r] / wsum * ROUTED_SCALE).reshape(1, tm)


def _ln1_route(mixed, w_o, x2, ln_g, ln_b, w_router_t, router_bias, tm):
    T = x2.shape[0]
    kern = functools.partial(_ln1_route_kernel, tm=tm)
    row = lambda i: (i, 0)
    fixed = lambda i: (0, 0)
    col = lambda i: (0, i)
    tile3 = lambda i: (i, 0, 0)
    return pl.pallas_call(
        kern,
        grid=(T // tm,),
        in_specs=[pl.BlockSpec((tm, D_MODEL), row),
                  pl.BlockSpec((D_MODEL, D_MODEL), fixed),
                  pl.BlockSpec((tm, D_MODEL), row),
                  pl.BlockSpec((1, D_MODEL), fixed),
                  pl.BlockSpec((1, D_MODEL), fixed),
                  pl.BlockSpec((N_EXPERTS, D_MODEL), fixed),
                  pl.BlockSpec((N_EXPERTS, 1), fixed)],
        out_specs=[pl.BlockSpec((tm, D_MODEL), row),
                   pl.BlockSpec((tm, HALF), row),
                   pl.BlockSpec((1, TOP_K, tm), tile3),
                   pl.BlockSpec((1, TOP_K, tm), tile3),
                   pl.BlockSpec((TOP_K, tm), col),
                   pl.BlockSpec((N_EXPERTS, LANES), fixed)],
        out_shape=[jax.ShapeDtypeStruct((T, D_MODEL), F32),
                   jax.ShapeDtypeStruct((T, HALF), U32),
                   jax.ShapeDtypeStruct((T // tm, TOP_K, tm), I32),
                   jax.ShapeDtypeStruct((T // tm, TOP_K, tm), I32),
                   jax.ShapeDtypeStruct((TOP_K, T), F32),
                   jax.ShapeDtypeStruct((N_EXPERTS, LANES), F32)],
        scratch_shapes=[pltpu.VMEM((N_EXPERTS, LANES), F32)],
        compiler_params=_params(("arbitrary",)),
        name="ln1_route",
    )(mixed, w_o, x2, ln_g.reshape(1, D_MODEL), ln_b.reshape(1, D_MODEL), w_router_t,
      router_bias.reshape(N_EXPERTS, 1))


def _dest_kernel(ps_ref, eidx_ref, rank_ref, dest_ref):
    eidx = eidx_ref[...]
    dest = rank_ref[...]
    for e in range(N_EXPERTS):
        dest = dest + jnp.where(eidx == e, ps_ref[e], 0)
    dest_ref[...] = dest


def _dest_rows(pad_start, eidx3, rank3):
    vmem = pl.BlockSpec(memory_space=pltpu.VMEM)
    return pl.pallas_call(
        _dest_kernel,
        in_specs=[pl.BlockSpec(memory_space=pltpu.SMEM), vmem, vmem],
        out_specs=vmem,
        out_shape=jax.ShapeDtypeStruct(eidx3.shape, I32),
        name="moe_dest",
    )(pad_start, eidx3, rank3)


def _row_copy(src_ref, src_row, dst_ref, dst_row, sem):
    return pltpu.make_async_copy(src_ref.at[pl.ds(src_row, 1)], dst_ref.at[pl.ds(dst_row, 1)], sem)


def _dispatch_kernel(dest_ref, fill_ref, pe_ref, hp_ref, xs_ref, zero_ref, sem, zsem, *, tm):
    @pl.when(pl.program_id(0) == 0)
    def _():
        zero_ref[...] = jnp.zeros_like(zero_ref)

        def fill_expert(e, n):
            lo = fill_ref[e]
            hi = pe_ref[e]

            def fill_row(r, c):
                _row_copy(zero_ref, 0, xs_ref, r, zsem).start()
                return c

            lax.fori_loop(lo, hi, fill_row, 0)
            return n + (hi - lo)

        n_fill = lax.fori_loop(0, N_EXPERTS, fill_expert, 0)

        def drain_fill(r, c):
            _row_copy(zero_ref, 0, xs_ref, 0, zsem).wait()
            return c

        lax.fori_loop(0, n_fill, drain_fill, 0)

    def issue(t, c):
        for k in range(TOP_K):
            _row_copy(hp_ref, t, xs_ref, dest_ref[0, k, t], sem).start(priority=k % 2)
        return c

    def drain(t, c):
        for k in range(TOP_K):
            _row_copy(hp_ref, 0, xs_ref, 0, sem).wait()
        return c

    lax.fori_loop(0, tm, issue, 0)
    lax.fori_loop(0, tm, drain, 0)


def _dispatch(dest3, fill_start, pad_end, hp, n_rows):
    T = hp.shape[0]
    tm = dest3.shape[2]
    kern = functools.partial(_dispatch_kernel, tm=tm)
    smem = pl.BlockSpec(memory_space=pltpu.SMEM)
    return pl.pallas_call(
        kern,
        grid=(T // tm,),
        in_specs=[pl.BlockSpec((1, TOP_K, tm), lambda i: (i, 0, 0), memory_space=pltpu.SMEM),
                  smem, smem,
                  pl.BlockSpec((tm, HALF), lambda i: (i, 0))],
        out_specs=pl.BlockSpec(memory_space=pl.ANY),
        out_shape=jax.ShapeDtypeStruct((n_rows, HALF), U32),
        scratch_shapes=[pltpu.VMEM((SUBLANES, HALF), U32), pltpu.SemaphoreType.DMA,
                        pltpu.SemaphoreType.DMA],
        compiler_params=_params(("arbitrary",)),
        name="moe_dispatch",
    )(dest3, fill_start, pad_end, hp)


def _experts_kernel(be_ref, nb_ref, xs_ref, wg_ref, wu_ref, wd_ref, ys_ref, wgb_ref, wub_ref, wdb_ref):
    b = pl.program_id(0)
    new_expert = jnp.logical_or(b == 0, be_ref[b] != be_ref[jnp.maximum(b - 1, 0)])

    @pl.when(jnp.logical_and(b < nb_ref[0], new_expert))
    def _():
        wgb_ref[...] = wg_ref[0].astype(BF16)
        wub_ref[...] = wu_ref[0].astype(BF16)
        wdb_ref[...] = wd_ref[0].astype(BF16)

    @pl.when(b < nb_ref[0])
    def _():
        lo, hi = _unpack_bf16_pairs(xs_ref[...])
        gate = _dot(lo, wgb_ref[:HALF, :]) + _dot(hi, wgb_ref[HALF:, :])
        up = _dot(lo, wub_ref[:HALF, :]) + _dot(hi, wub_ref[HALF:, :])
        hb = (jax.nn.silu(gate) * up).astype(BF16)
        ys_ref[...] = _dot(hb, wdb_ref[...])


def _experts(block_e, nb_used, xs, wg, wu, wd):
    n_rows = xs.shape[0]
    nb = n_rows // MOE_ROWS
    rows = lambda b, be, nbu: (jnp.minimum(b, nbu[0] - 1), 0)
    wsel = lambda b, be, nbu: (be[b], 0, 0)
    grid_spec = pltpu.PrefetchScalarGridSpec(
        num_scalar_prefetch=2,
        grid=(nb,),
        in_specs=[pl.BlockSpec((MOE_ROWS, HALF), rows),
                  pl.BlockSpec((1, D_MODEL, D_EXPERT), wsel),
                  pl.BlockSpec((1, D_MODEL, D_EXPERT), wsel),
                  pl.BlockSpec((1, D_EXPERT, D_MODEL), wsel)],
        out_specs=pl.BlockSpec((MOE_ROWS, D_MODEL), rows),
        scratch_shapes=[pltpu.VMEM((D_MODEL, D_EXPERT), BF16), pltpu.VMEM((D_MODEL, D_EXPERT), BF16),
                        pltpu.VMEM((D_EXPERT, D_MODEL), BF16)],
    )
    return pl.pallas_call(
        _experts_kernel,
        grid_spec=grid_spec,
        out_shape=jax.ShapeDtypeStruct((n_rows, D_MODEL), F32),
        compiler_params=_params(("arbitrary",)),
        name="moe_experts",
    )(block_e, nb_used, xs, wg, wu, wd)


def _combine_kernel(dest_ref, w_ref, h_ref, wgs_ref, wus_ref, wds_ref, g_ref, b_ref,
                    ys_ref, o_ref, buf_ref, sem, *, tm):
    def issue(t, c):
        for k in range(TOP_K):
            _row_copy(ys_ref, dest_ref[0, k, t], buf_ref.at[k], t, sem).start(priority=k % 2)
        return c

    def drain(t, c):
        for k in range(TOP_K):
            _row_copy(ys_ref, 0, buf_ref.at[k], 0, sem).wait()
        return c

    lax.fori_loop(0, tm, issue, 0)

    h = h_ref[...]
    hb = h.astype(BF16)
    hid = (jax.nn.silu(_dot(hb, wgs_ref[...])) * _dot(hb, wus_ref[...])).astype(BF16)
    shared = _dot(hid, wds_ref[...])

    lax.fori_loop(0, tm, drain, 0)

    w = w_ref[...]
    routed = buf_ref[0] * w[:, 0:1]
    for k in range(1, TOP_K):
        routed = routed + buf_ref[k] * w[:, k:k + 1]
    o_ref[...] = _layer_norm(DN_ALPHA * h + (routed + shared), g_ref[...], b_ref[...])


def _combine(dest3, w_tok, h, wgs, wus, wds, ln_g, ln_b, ys3, tm):
    T = h.shape[0]
    per = dest3.shape[2] // tm
    kern = functools.partial(_combine_kernel, tm=tm)
    row = lambda i: (i, 0)
    fixed = lambda i: (0, 0)
    return pl.pallas_call(
        kern,
        grid=(T // tm,),
        in_specs=[pl.BlockSpec((1, TOP_K, tm), lambda i: (i // per, 0, i % per),
                               memory_space=pltpu.SMEM),
                  pl.BlockSpec((tm, TOP_K), row),
                  pl.BlockSpec((tm, D_MODEL), row),
                  pl.BlockSpec((D_MODEL, D_SHARED), fixed),
                  pl.BlockSpec((D_MODEL, D_SHARED), fixed),
                  pl.BlockSpec((D_SHARED, D_MODEL), fixed),
                  pl.BlockSpec((1, D_MODEL), fixed),
                  pl.BlockSpec((1, D_MODEL), fixed),
                  pl.BlockSpec(memory_space=pl.ANY)],
        out_specs=pl.BlockSpec((tm, D_MODEL), row),
        out_shape=jax.ShapeDtypeStruct((T, D_MODEL), F32),
        scratch_shapes=[pltpu.VMEM((TOP_K, tm, D_MODEL), F32), pltpu.SemaphoreType.DMA],
        compiler_params=_params(("arbitrary",)),
        name="moe_combine",
    )(dest3, w_tok, h, wgs, wus, wds, ln_g.reshape(1, D_MODEL), ln_b.reshape(1, D_MODEL), ys3)


def _tile(n, pref):
    t = min(n, pref)
    assert n % t == 0, (n, pref)
    return t


def _layer(x, w_in, conv_w, conv_b, lru_wa, lru_ba, lru_wi, lru_bi, lru_lambda, ret_gn_gain,
           w_lru_out, w_ret_out, b_gate, w_o, ln1_g, ln1_b, w_router, router_bias,
           w_gate_e, w_up_e, w_down_e, w_gate_s, w_up_s, w_down_s, ln2_g, ln2_b):
    B, S, D = x.shape
    T = B * S
    x2 = x.reshape(T, D)
    xb = x2.astype(BF16)

    half = RET_DK // 2
    freq = ROPE_THETA ** (-jnp.arange(half, dtype=F32) / half)
    ang = jnp.arange(S, dtype=I32).astype(F32)[:, None] * freq
    cos, sin = jnp.cos(ang), jnp.sin(ang)

    tm_big = _tile(S, 1024)
    qk = _proj_qk(xb, w_in, cos, sin, S, tm_big, 1024)
    v = _proj_plain(xb, w_in, OFF_V, D_RV, BF16, tm_big, 1024, "v_proj")
    g = _proj_plain(xb, w_in, OFF_G, D_RV, F32, tm_big, 1024, "g_proj")

    ua = _lru_branch(xb, w_in, conv_w, conv_b, lru_wa.astype(BF16), lru_ba, lru_wi.astype(BF16),
                     lru_bi, lru_lambda, S, _tile(S, 256), 512)
    ub = _retention(qk, v, g, ret_gn_gain, B, S)

    mixed = _mix(ua, ub, xb, w_lru_out.astype(BF16), w_ret_out.astype(BF16),
                 w_in[:, OFF_GA:].astype(BF16), b_gate, _tile(T, 512), 512)

    tm_r = _tile(T, 256)
    h, hp3, eidx, rank, wsel, cnt = _ln1_route(mixed, w_o.astype(BF16), x2, ln1_g, ln1_b,
                                               w_router.T.astype(BF16), router_bias, tm_r)

    counts = cnt[:, 0].astype(I32)
    padded = (counts + MOE_ROWS - 1) // MOE_ROWS * MOE_ROWS
    pad_end = jnp.cumsum(padded)
    pad_start = pad_end - padded
    nb = (T * TOP_K + N_EXPERTS * (MOE_ROWS - 1) + MOE_ROWS - 1) // MOE_ROWS
    n_rows = nb * MOE_ROWS
    block_row0 = jnp.arange(nb, dtype=I32) * MOE_ROWS
    block_e = jnp.minimum(jnp.sum((pad_end[None, :] <= block_row0[:, None]).astype(I32), axis=1),
                          N_EXPERTS - 1)
    nb_used = (pad_end[-1:] // MOE_ROWS).astype(I32)

    dest = _dest_rows(pad_start, eidx, rank)
    xs3 = _dispatch(dest, pad_start + counts, pad_end, hp3, n_rows)
    ys3 = _experts(block_e, nb_used, xs3, w_gate_e, w_up_e, w_down_e)
    out = _combine(dest, wsel.T, h, w_gate_s.astype(BF16), w_up_s.astype(BF16),
                   w_down_s.astype(BF16), ln2_g, ln2_b, ys3, _tile(tm_r, 128))
    return out.reshape(B, S, D)


def kernel(x, w_in, conv_w, conv_b, lru_wa, lru_ba, lru_wi, lru_bi, lru_lambda, ret_gn_gain, w_lru_out, w_ret_out, b_gate, w_o, ln1_g, ln1_b, w_router, router_bias, w_gate_e, w_up_e, w_down_e, w_gate_s, w_up_s, w_down_s, ln2_g, ln2_b):
    assert DEPTH == 1 and w_in.shape[0] == DEPTH
    args = (w_in, conv_w, conv_b, lru_wa, lru_ba, lru_wi, lru_bi, lru_lambda, ret_gn_gain,
            w_lru_out, w_ret_out, b_gate, w_o, ln1_g, ln1_b, w_router, router_bias,
            w_gate_e, w_up_e, w_down_e, w_gate_s, w_up_s, w_down_s, ln2_g, ln2_b)
    return _layer(x, *[a[0] for a in args])
```

```python
import functools

import jax
import jax.numpy as jnp
from jax import lax
from jax.experimental import pallas as pl
from jax.experimental.pallas import tpu as pltpu

F32 = jnp.float32
BF16 = jnp.bfloat16
I32 = jnp.int32
U32 = jnp.uint32

D_MODEL = 2048
D_LRU = 2048
LRU_HEADS = 16
LRU_BLOCK = D_LRU // LRU_HEADS
CONV_WIDTH = 4
LRU_C = 8.0
RET_HEADS = 8
RET_DK = 256
RET_DV = 512
D_QK = RET_HEADS * RET_DK
D_RV = RET_HEADS * RET_DV
ROPE_THETA = 10000.0
N_EXPERTS = 64
TOP_K = 8
N_GROUPS = 8
GROUP_SIZE = N_EXPERTS // N_GROUPS
TOPK_GROUPS = 4
D_EXPERT = 512
D_SHARED = 512
ROUTED_SCALE = 2.5
DEPTH = 1
DN_ALPHA = (2.0 * DEPTH) ** 0.25
LN_EPS = 1e-5

OFF_LRU_X = 0
OFF_LRU_Y = OFF_LRU_X + D_LRU
OFF_Q = OFF_LRU_Y + D_LRU
OFF_K = OFF_Q + D_QK
OFF_V = OFF_K + D_QK
OFF_G = OFF_V + D_RV
OFF_GA = OFF_G + D_RV
OFF_GB = OFF_GA + D_MODEL

V7X_VMEM_LIMIT = 56 * 1024 * 1024
LANES = 128
SUBLANES = 8
MOE_ROWS = 512
RET_CHUNK = 256
HALF = D_MODEL // 2


def _params(sem):
    return pltpu.CompilerParams(dimension_semantics=sem, vmem_limit_bytes=V7X_VMEM_LIMIT)


def _dot(a, b):
    return jnp.dot(a, b, preferred_element_type=F32)


def _cast_weight(w_ref, wb_ref):
    @pl.when(pl.program_id(1) == 0)
    def _():
        wb_ref[...] = w_ref[...].astype(BF16)


def _mm_kernel(x_ref, w_ref, o_ref, wb_ref):
    _cast_weight(w_ref, wb_ref)
    o_ref[...] = _dot(x_ref[...], wb_ref[...]).astype(o_ref.dtype)


def _proj_plain(xb, w, col0, ncols, out_dtype, tm, tn, name):
    T, K = xb.shape
    jb0 = col0 // tn
    return pl.pallas_call(
        _mm_kernel,
        grid=(ncols // tn, T // tm),
        in_specs=[pl.BlockSpec((tm, K), lambda j, i: (i, 0)),
                  pl.BlockSpec((K, tn), lambda j, i: (0, jb0 + j))],
        out_specs=pl.BlockSpec((tm, tn), lambda j, i: (i, j)),
        out_shape=jax.ShapeDtypeStruct((T, ncols), out_dtype),
        scratch_shapes=[pltpu.VMEM((K, tn), BF16)],
        compiler_params=_params(("parallel", "arbitrary")),
        name=name,
    )(xb, w)


def _qk_kernel(x_ref, w_ref, cos_ref, sin_ref, o_ref, wb_ref, *, heads_per_tile, k_tile0):
    _cast_weight(w_ref, wb_ref)
    j = pl.program_id(0)
    acc = _dot(x_ref[...], wb_ref[...])
    scale = jnp.where(j >= k_tile0, RET_DK ** -0.5, 1.0).astype(F32)
    cos = cos_ref[...]
    sin = sin_ref[...]
    half = RET_DK // 2
    for hh in range(heads_per_tile):
        c0 = hh * RET_DK
        t1 = acc[:, c0:c0 + half]
        t2 = acc[:, c0 + half:c0 + RET_DK]
        o_ref[:, c0:c0 + half] = ((t1 * cos - t2 * sin) * scale).astype(o_ref.dtype)
        o_ref[:, c0 + half:c0 + RET_DK] = ((t1 * sin + t2 * cos) * scale).astype(o_ref.dtype)


def _proj_qk(xb, w, cos, sin, seq, tm, tn):
    T, K = xb.shape
    ncols = 2 * D_QK
    jb0 = OFF_Q // tn
    n_seq_tiles = seq // tm
    kern = functools.partial(_qk_kernel, heads_per_tile=tn // RET_DK, k_tile0=D_QK // tn)
    return pl.pallas_call(
        kern,
        grid=(ncols // tn, T // tm),
        in_specs=[pl.BlockSpec((tm, K), lambda j, i: (i, 0)),
                  pl.BlockSpec((K, tn), lambda j, i: (0, jb0 + j)),
                  pl.BlockSpec((tm, RET_DK // 2), lambda j, i: (i % n_seq_tiles, 0)),
                  pl.BlockSpec((tm, RET_DK // 2), lambda j, i: (i % n_seq_tiles, 0))],
        out_specs=pl.BlockSpec((tm, tn), lambda j, i: (i, j)),
        out_shape=jax.ShapeDtypeStruct((T, ncols), BF16),
        scratch_shapes=[pltpu.VMEM((K, tn), BF16)],
        compiler_params=_params(("parallel", "arbitrary")),
        name="qk_rope",
    )(xb, w, cos, sin)


def _lru_kernel(x_ref, wx_ref, wy_ref, cw_ref, cb_ref, wa_ref, ba_ref, wi_ref, bi_ref, lam_ref,
                o_ref, wxb_ref, wyb_ref, perm_ref, unperm_ref, tail_ref, hc_ref, a_ref, b_ref, gy_ref,
                *, tiles_per_seq, tm, width):
    i = pl.program_id(1)
    first = (i % tiles_per_seq) == 0
    seg = tm // SUBLANES
    halo = (CONV_WIDTH - 1) * SUBLANES

    @pl.when(i == 0)
    def _():
        wxb_ref[...] = wx_ref[...].astype(BF16)
        wyb_ref[...] = wy_ref[...].astype(BF16)
        r = lax.broadcasted_iota(I32, (tm, tm), 0)
        c = lax.broadcasted_iota(I32, (tm, tm), 1)
        perm_ref[...] = jnp.where(c == (r & (SUBLANES - 1)) * seg + (r >> 3), 1.0, 0.0).astype(BF16)
        unperm_ref[...] = jnp.where(r == (c & (SUBLANES - 1)) * seg + (c >> 3), 1.0, 0.0).astype(BF16)

    @pl.when(first)
    def _():
        tail_ref[...] = jnp.zeros_like(tail_ref)
        hc_ref[...] = jnp.zeros_like(hc_ref)

    xp = _dot(perm_ref[...], x_ref[...]).astype(BF16)
    px = _dot(xp, wxb_ref[...])
    gy_ref[...] = jax.nn.gelu(_dot(xp, wyb_ref[...]), approximate=True)

    cw = cw_ref[...]
    cb = cb_ref[...]
    taps = [cw[CONV_WIDTH - 1 - d:CONV_WIDTH - d] for d in range(CONV_WIDTH)]
    xa = cb + taps[0] * px
    for d in range(1, CONV_WIDTH):
        back = jnp.concatenate([px[tm - d * SUBLANES:], px[:tm - d * SUBLANES]], axis=0)
        xa = xa + taps[d] * back
    sub = lax.broadcasted_iota(I32, (SUBLANES, width), 0)
    groups = {}
    for g in range(CONV_WIDTH - 1):
        rows = slice(g * SUBLANES, (g + 1) * SUBLANES)
        cur_end = px[tm - halo + g * SUBLANES:tm - halo + (g + 1) * SUBLANES]
        groups[g - (CONV_WIDTH - 1)] = jnp.where(sub == 0, pltpu.roll(tail_ref[rows, :], 1, 0),
                                                 pltpu.roll(cur_end, 1, 0))
        groups[g] = px[rows, :]
    head = []
    for q in range(CONV_WIDTH - 1):
        acc = cb + taps[0] * groups[q]
        for d in range(1, CONV_WIDTH):
            acc = acc + taps[d] * groups[q - d]
        head.append(acc)
    xa = jnp.concatenate(head + [xa[halo:]], axis=0)
    tail_ref[...] = px[tm - halo:]

    xab = xa.astype(BF16)
    r_parts, i_parts = [], []
    for hh in range(width // LRU_BLOCK):
        xh = xab[:, hh * LRU_BLOCK:(hh + 1) * LRU_BLOCK]
        r_parts.append(_dot(xh, wa_ref[hh]))
        i_parts.append(_dot(xh, wi_ref[hh]))
    rg = jax.nn.sigmoid(jnp.concatenate(r_parts, axis=1) + ba_ref[...])
    ig = jax.nn.sigmoid(jnp.concatenate(i_parts, axis=1) + bi_ref[...])

    log_a = -LRU_C * rg * jax.nn.softplus(-lam_ref[...])
    a = jnp.exp(log_a)
    mult = jnp.sqrt(-jnp.tanh(log_a) * (a * a + 1.0))
    row = lax.broadcasted_iota(I32, (tm, width), 0)
    mult = jnp.where(jnp.logical_and(first, row == 0), 1.0, mult)
    a_ref[...] = a
    b_ref[...] = mult * ig * xa

    run_a = jnp.ones((SUBLANES, width), F32)
    run_h = jnp.zeros((SUBLANES, width), F32)
    for q in range(seg):
        rows = slice(q * SUBLANES, (q + 1) * SUBLANES)
        aq = a_ref[rows, :]
        run_h = aq * run_h + b_ref[rows, :]
        run_a = aq * run_a
        b_ref[rows, :] = run_h
        a_ref[rows, :] = run_a

    carry = hc_ref[...]
    carries = []
    for s in range(SUBLANES):
        carries.append(carry)
        carry = run_a[s:s + 1, :] * carry + run_h[s:s + 1, :]
    hc_ref[...] = carry
    carry_in = jnp.concatenate(carries, axis=0)
    for q in range(seg):
        rows = slice(q * SUBLANES, (q + 1) * SUBLANES)
        b_ref[rows, :] = (b_ref[rows, :] + a_ref[rows, :] * carry_in) * gy_ref[rows, :]

    o_ref[...] = _dot(unperm_ref[...], b_ref[...].astype(BF16)).astype(o_ref.dtype)


def _lru_branch(xb, w, conv_w, conv_b, wa, ba, wi, bi, lam, seq, tm, width):
    T, K = xb.shape
    nj = D_LRU // width
    jy0 = OFF_LRU_Y // width
    hp = width // LRU_BLOCK
    kern = functools.partial(_lru_kernel, tiles_per_seq=seq // tm, tm=tm, width=width)
    vec = lambda j, i: (0, j)
    return pl.pallas_call(
        kern,
        grid=(nj, T // tm),
        in_specs=[pl.BlockSpec((tm, K), lambda j, i: (i, 0)),
                  pl.BlockSpec((K, width), lambda j, i: (0, j)),
                  pl.BlockSpec((K, width), lambda j, i: (0, jy0 + j)),
                  pl.BlockSpec((CONV_WIDTH, width), vec),
                  pl.BlockSpec((1, width), vec),
                  pl.BlockSpec((hp, LRU_BLOCK, LRU_BLOCK), lambda j, i: (j, 0, 0)),
                  pl.BlockSpec((1, width), vec),
                  pl.BlockSpec((hp, LRU_BLOCK, LRU_BLOCK), lambda j, i: (j, 0, 0)),
                  pl.BlockSpec((1, width), vec),
                  pl.BlockSpec((1, width), vec)],
        out_specs=pl.BlockSpec((tm, width), lambda j, i: (i, j)),
        out_shape=jax.ShapeDtypeStruct((T, D_LRU), BF16),
        scratch_shapes=[pltpu.VMEM((K, width), BF16), pltpu.VMEM((K, width), BF16),
                        pltpu.VMEM((tm, tm), BF16), pltpu.VMEM((tm, tm), BF16),
                        pltpu.VMEM(((CONV_WIDTH - 1) * SUBLANES, width), F32),
                        pltpu.VMEM((1, width), F32),
                        pltpu.VMEM((tm, width), F32), pltpu.VMEM((tm, width), F32),
                        pltpu.VMEM((tm, width), F32)],
        compiler_params=_params(("parallel", "arbitrary")),
        name="lru_branch",
    )(xb, w, w, conv_w, conv_b.reshape(1, D_LRU), wa, ba.reshape(1, D_LRU), wi, bi.reshape(1, D_LRU),
      lam.reshape(1, D_LRU))


def _ret_kernel(gc_ref, q_ref, k_ref, v_ref, g_ref, gain_ref, dm_ref, xi_ref, zeta_ref, o_ref, st_ref):
    @pl.when(pl.program_id(1) == 0)
    def _():
        st_ref[...] = jnp.zeros_like(st_ref)

    for hd in range(RET_HEADS):
        qc = slice(hd * RET_DK, (hd + 1) * RET_DK)
        vc = slice(hd * RET_DV, (hd + 1) * RET_DV)
        q = q_ref[:, qc]
        k = k_ref[:, qc]
        v = v_ref[:, vc]
        st = st_ref[hd]
        s = lax.dot_general(q, k, (((1,), (1,)), ((), ())), preferred_element_type=F32) * dm_ref[hd]
        o = _dot(s.astype(BF16), v) + _dot(q, st.astype(BF16)) * xi_ref[hd]
        kz = (k.astype(F32) * zeta_ref[hd]).astype(BF16)
        st_ref[hd] = gc_ref[hd] * st + lax.dot_general(kz, v, (((0,), (0,)), ((), ())),
                                                       preferred_element_type=F32)
        mu = jnp.mean(o, axis=-1, keepdims=True)
        oc = o - mu
        var = jnp.mean(oc * oc, axis=-1, keepdims=True)
        oh = oc * lax.rsqrt(var + LN_EPS) * gain_ref[:, vc]
        o_ref[:, vc] = (jax.nn.silu(g_ref[:, vc]) * oh).astype(o_ref.dtype)


def _retention(qk, v, g, gain, batch, seq):
    T = batch * seq
    C = min(RET_CHUNK, seq)
    nchunk = seq // C
    H = RET_HEADS
    log_g = jnp.log1p(-jnp.exp2(-5.0 - jnp.arange(H, dtype=F32)))
    idx = jnp.arange(C, dtype=F32)
    diff = idx[:, None] - idx[None, :]
    dmask = jnp.where(diff >= 0, jnp.exp(jnp.maximum(diff, 0.0)[None] * log_g[:, None, None]), 0.0)
    xi = jnp.exp((idx[None] + 1.0) * log_g[:, None])[:, :, None]
    zeta = jnp.exp((C - 1.0 - idx[None]) * log_g[:, None])[:, :, None]
    g_c = jnp.exp(C * log_g)
    rows = lambda b, n: b * nchunk + n
    whole3 = lambda b, n: (0, 0, 0)
    return pl.pallas_call(
        _ret_kernel,
        grid=(batch, nchunk),
        in_specs=[pl.BlockSpec(memory_space=pltpu.SMEM),
                  pl.BlockSpec((C, D_QK), lambda b, n: (rows(b, n), 0)),
                  pl.BlockSpec((C, D_QK), lambda b, n: (rows(b, n), 1)),
                  pl.BlockSpec((C, D_RV), lambda b, n: (rows(b, n), 0)),
                  pl.BlockSpec((C, D_RV), lambda b, n: (rows(b, n), 0)),
                  pl.BlockSpec((1, D_RV), lambda b, n: (0, 0)),
                  pl.BlockSpec((H, C, C), whole3),
                  pl.BlockSpec((H, C, 1), whole3),
                  pl.BlockSpec((H, C, 1), whole3)],
        out_specs=pl.BlockSpec((C, D_RV), lambda b, n: (rows(b, n), 0)),
        out_shape=jax.ShapeDtypeStruct((T, D_RV), BF16),
        scratch_shapes=[pltpu.VMEM((H, RET_DK, RET_DV), F32)],
        compiler_params=_params(("parallel", "arbitrary")),
        name="retention",
    )(g_c, qk, qk, v, g, gain.reshape(1, D_RV), dmask, xi, zeta)


def _mix_kernel(ua_ref, ub_ref, x_ref, wlo_ref, wro_ref, wga_ref, wgb_ref, bga_ref, bgb_ref, o_ref):
    x = x_ref[...]
    ya = _dot(ua_ref[...], wlo_ref[...])
    yb = _dot(ub_ref[...], wro_ref[...])
    ga = jax.nn.sigmoid(_dot(x, wga_ref[...]) + bga_ref[...])
    gb = jax.nn.sigmoid(_dot(x, wgb_ref[...]) + bgb_ref[...])
    o_ref[...] = (ga * ya + gb * yb).astype(o_ref.dtype)


def _mix(ua, ub, xb, w_lru_out, w_ret_out, w_gates, b_gate, tm, tn):
    T = xb.shape[0]
    nj = D_MODEL // tn
    return pl.pallas_call(
        _mix_kernel,
        grid=(T // tm, nj),
        in_specs=[pl.BlockSpec((tm, D_LRU), lambda i, j: (i, 0)),
                  pl.BlockSpec((tm, D_RV), lambda i, j: (i, 0)),
                  pl.BlockSpec((tm, D_MODEL), lambda i, j: (i, 0)),
                  pl.BlockSpec((D_LRU, tn), lambda i, j: (0, j)),
                  pl.BlockSpec((D_RV, tn), lambda i, j: (0, j)),
                  pl.BlockSpec((D_MODEL, tn), lambda i, j: (0, j)),
                  pl.BlockSpec((D_MODEL, tn), lambda i, j: (0, nj + j)),
                  pl.BlockSpec((1, tn), lambda i, j: (0, j)),
                  pl.BlockSpec((1, tn), lambda i, j: (0, nj + j))],
        out_specs=pl.BlockSpec((tm, tn), lambda i, j: (i, j)),
        out_shape=jax.ShapeDtypeStruct((T, D_MODEL), BF16),
        compiler_params=_params(("parallel", "parallel")),
        name="gated_mix",
    )(ua, ub, xb, w_lru_out, w_ret_out, w_gates, w_gates, b_gate.reshape(1, 2 * D_MODEL),
      b_gate.reshape(1, 2 * D_MODEL))


def _layer_norm(y, g, b):
    mu = jnp.mean(y, axis=-1, keepdims=True)
    yc = y - mu
    var = jnp.mean(yc * yc, axis=-1, keepdims=True)
    return yc * lax.rsqrt(var + LN_EPS) * g + b


def _pack_bf16_pairs(hb):
    lo = lax.bitcast_convert_type(hb[:, :HALF].astype(F32), U32)
    hi = lax.bitcast_convert_type(hb[:, HALF:].astype(F32), U32)
    return (hi & jnp.uint32(0xFFFF0000)) | (lo >> 16)


def _unpack_bf16_pairs(p):
    lo = lax.bitcast_convert_type(p << 16, F32).astype(BF16)
    hi = lax.bitcast_convert_type(p & jnp.uint32(0xFFFF0000), F32).astype(BF16)
    return lo, hi


def _max2(a):
    return jnp.max(jnp.max(a, axis=1, keepdims=True), axis=0, keepdims=True)


def _min2(a):
    return jnp.min(jnp.min(a, axis=1, keepdims=True), axis=0, keepdims=True)


def _sum2(a):
    return jnp.sum(jnp.sum(a, axis=1, keepdims=True), axis=0, keepdims=True)


def _ln1_route_kernel(mix_ref, wo_ref, x_ref, g_ref, b_ref, wr_ref, rb_ref,
                      h_ref, hp_ref, eidx_ref, rank_ref, wsel_ref, cnt_ref, cnt_sc, *, tm):
    i = pl.program_id(0)

    @pl.when(i == 0)
    def _():
        cnt_sc[...] = jnp.zeros_like(cnt_sc)

    y = DN_ALPHA * x_ref[...] + _dot(mix_ref[...], wo_ref[...])
    h = _layer_norm(y, g_ref[...], b_ref[...])
    h_ref[...] = h
    hb = h.astype(BF16)
    hp_ref[...] = _pack_bf16_pairs(hb)

    G, GS = N_GROUPS, GROUP_SIZE
    logits = lax.dot_general(wr_ref[...], hb, (((1,), (1,)), ((), ())), preferred_element_type=F32)
    scores = jax.nn.sigmoid(logits)
    s3 = scores.reshape(G, GS, tm)
    b3 = (scores + rb_ref[...]).reshape(G, GS, tm)
    neg = jnp.float32(-jnp.inf)

    in_grp = lax.broadcasted_iota(I32, (G, GS, tm), 1)
    m1 = jnp.max(b3, axis=1, keepdims=True)
    f1 = jnp.min(jnp.where(b3 == m1, in_grp, GS), axis=1, keepdims=True)
    m2 = jnp.max(jnp.where(in_grp == f1, neg, b3), axis=1, keepdims=True)
    grp = m1 + m2

    gi = lax.broadcasted_iota(I32, (G, 1, tm), 0)
    gkeep = jnp.zeros((G, 1, tm), F32)
    cur = grp
    for _ in range(TOPK_GROUPS):
        m = jnp.max(cur, axis=0, keepdims=True)
        f = jnp.min(jnp.where(cur == m, gi, G), axis=0, keepdims=True)
        hit = gi == f
        gkeep = jnp.where(hit, 1.0, gkeep)
        cur = jnp.where(hit, neg, cur)
    emask = jnp.broadcast_to(gkeep, (G, GS, tm)) > 0.0

    ei = lax.broadcasted_iota(I32, (G, GS, tm), 0) * GS + in_grp
    cur = jnp.where(emask, b3, neg)
    hits, idxs, ws = [], [], []
    for _ in range(TOP_K):
        m = _max2(cur)
        f = _min2(jnp.where(cur == m, ei, N_EXPERTS))
        hit = ei == f
        cur = jnp.where(hit, neg, cur)
        hits.append(hit)
        idxs.append(f)
        ws.append(_sum2(jnp.where(hit, s3, 0.0)))
    wsum = ws[0]
    for r in range(1, TOP_K):
        wsum = wsum + ws[r]

    sel = jnp.zeros((G, GS, tm), F32)
    for hit in hits:
        sel = jnp.where(hit, 1.0, sel)
    sel2 = sel.reshape(N_EXPERTS, tm)
    tri = (lax.broadcasted_iota(I32, (tm, tm), 0) < lax.broadcasted_iota(I32, (tm, tm), 1))
    rank_local = _dot(sel2.astype(BF16), jnp.where(tri, 1.0, 0.0).astype(BF16))
    cnt = cnt_sc[...]
    rank3 = (rank_local + cnt[:, 0:1]).reshape(G, GS, tm)
    cnt_new = cnt + jnp.sum(sel2, axis=1, keepdims=True)
    cnt_sc[...] = cnt_new
    cnt_ref[...] = cnt_new

    for r in range(TOP_K):
        eidx_ref[0, r:r + 1, :] = idxs[r].reshape(1, tm)
        rank_ref[0, r:r + 1, :] = _sum2(jnp.where(hits[r], rank3, 0.0)).reshape(1, tm).astype(I32)
        wsel_ref[r:r + 1, :] = (ws[r] / wsum * ROUTED_SCALE).reshape(1, tm)


def _ln1_route(mixed, w_o, x2, ln_g, ln_b, w_router_t, router_bias, tm):
    T = x2.shape[0]
    kern = functools.partial(_ln1_route_kernel, tm=tm)
    row = lambda i: (i, 0)
    fixed = lambda i: (0, 0)
    col = lambda i: (0, i)
    tile3 = lambda i: (i, 0, 0)
    return pl.pallas_call(
        kern,
        grid=(T // tm,),
        in_specs=[pl.BlockSpec((tm, D_MODEL), row),
                  pl.BlockSpec((D_MODEL, D_MODEL), fixed),
                  pl.BlockSpec((tm, D_MODEL), row),
                  pl.BlockSpec((1, D_MODEL), fixed),
                  pl.BlockSpec((1, D_MODEL), fixed),
                  pl.BlockSpec((N_EXPERTS, D_MODEL), fixed),
                  pl.BlockSpec((N_EXPERTS, 1), fixed)],
        out_specs=[pl.BlockSpec((tm, D_MODEL), row),
                   pl.BlockSpec((tm, HALF), row),
                   pl.BlockSpec((1, TOP_K, tm), tile3),
                   pl.BlockSpec((1, TOP_K, tm), tile3),
                   pl.BlockSpec((TOP_K, tm), col),
                   pl.BlockSpec((N_EXPERTS, LANES), fixed)],
        out_shape=[jax.ShapeDtypeStruct((T, D_MODEL), F32),
                   jax.ShapeDtypeStruct((T, HALF), U32),
                   jax.ShapeDtypeStruct((T // tm, TOP_K, tm), I32),
                   jax.ShapeDtypeStruct((T // tm, TOP_K, tm), I32),
                   jax.ShapeDtypeStruct((TOP_K, T), F32),
                   jax.ShapeDtypeStruct((N_EXPERTS, LANES), F32)],
        scratch_shapes=[pltpu.VMEM((N_EXPERTS, LANES), F32)],
        compiler_params=_params(("arbitrary",)),
        name="ln1_route",
    )(mixed, w_o, x2, ln_g.reshape(1, D_MODEL), ln_b.reshape(1, D_MODEL), w_router_t,
      router_bias.reshape(N_EXPERTS, 1))


def _dest_kernel(ps_ref, eidx_ref, rank_ref, dest_ref):
    eidx = eidx_ref[...]
    dest = rank_ref[...]
    for e in range(N_EXPERTS):
        dest = dest + jnp.where(eidx == e, ps_ref[e], 0)
    dest_ref[...] = dest


def _dest_rows(pad_start, eidx3, rank3):
    vmem = pl.BlockSpec(memory_space=pltpu.VMEM)
    return pl.pallas_call(
        _dest_kernel,
        in_specs=[pl.BlockSpec(memory_space=pltpu.SMEM), vmem, vmem],
        out_specs=vmem,
        out_shape=jax.ShapeDtypeStruct(eidx3.shape, I32),
        name="moe_dest",
    )(pad_start, eidx3, rank3)


def _row_copy(src_ref, src_row, dst_ref, dst_row, sem):
    return pltpu.make_async_copy(src_ref.at[pl.ds(src_row, 1)], dst_ref.at[pl.ds(dst_row, 1)], sem)


def _dispatch_kernel(dest_ref, fill_ref, pe_ref, hp_ref, xs_ref, zero_ref, sem, zsem, *, tm):
    @pl.when(pl.program_id(0) == 0)
    def _():
        zero_ref[...] = jnp.zeros_like(zero_ref)

        def group_copy(r):
            return pltpu.make_async_copy(zero_ref, xs_ref.at[pl.ds(pl.multiple_of(r, SUBLANES), SUBLANES)],
                                         zsem)

        def fill_expert(e, counts):
            n_rows, n_groups = counts
            lo = fill_ref[e]
            hi = pe_ref[e]
            mid = jnp.minimum((lo + SUBLANES - 1) // SUBLANES * SUBLANES, hi)
            groups = (hi - mid) // SUBLANES

            def fill_row(r, c):
                _row_copy(zero_ref, 0, xs_ref, r, zsem).start()
                return c

            def fill_group(g, c):
                group_copy(mid + g * SUBLANES).start()
                return c

            lax.fori_loop(lo, mid, fill_row, 0)
            lax.fori_loop(0, groups, fill_group, 0)
            return n_rows + (mid - lo), n_groups + groups

        n_rows, n_groups = lax.fori_loop(0, N_EXPERTS, fill_expert, (0, 0))

        def drain_row(r, c):
            _row_copy(zero_ref, 0, xs_ref, 0, zsem).wait()
            return c

        def drain_group(g, c):
            group_copy(0).wait()
            return c

        lax.fori_loop(0, n_rows, drain_row, 0)
        lax.fori_loop(0, n_groups, drain_group, 0)

    def issue(t, c):
        for k in range(TOP_K):
            _row_copy(hp_ref, t, xs_ref, dest_ref[0, k, t], sem).start()
        return c

    def drain(t, c):
        for k in range(TOP_K):
            _row_copy(hp_ref, 0, xs_ref, 0, sem).wait()
        return c

    lax.fori_loop(0, tm, issue, 0)
    lax.fori_loop(0, tm, drain, 0)


def _dispatch(dest3, fill_start, pad_end, hp, n_rows):
    T = hp.shape[0]
    tm = dest3.shape[2]
    kern = functools.partial(_dispatch_kernel, tm=tm)
    smem = pl.BlockSpec(memory_space=pltpu.SMEM)
    return pl.pallas_call(
        kern,
        grid=(T // tm,),
        in_specs=[pl.BlockSpec((1, TOP_K, tm), lambda i: (i, 0, 0), memory_space=pltpu.SMEM),
                  smem, smem,
                  pl.BlockSpec((tm, HALF), lambda i: (i, 0))],
        out_specs=pl.BlockSpec(memory_space=pl.ANY),
        out_shape=jax.ShapeDtypeStruct((n_rows, HALF), U32),
        scratch_shapes=[pltpu.VMEM((SUBLANES, HALF), U32), pltpu.SemaphoreType.DMA,
                        pltpu.SemaphoreType.DMA],
        compiler_params=_params(("arbitrary",)),
        name="moe_dispatch",
    )(dest3, fill_start, pad_end, hp)


def _experts_kernel(be_ref, nb_ref, xs_ref, wg_ref, wu_ref, wd_ref, ys_ref, wgb_ref, wub_ref, wdb_ref):
    b = pl.program_id(0)
    new_expert = jnp.logical_or(b == 0, be_ref[b] != be_ref[jnp.maximum(b - 1, 0)])

    @pl.when(jnp.logical_and(b < nb_ref[0], new_expert))
    def _():
        wgb_ref[...] = wg_ref[0].astype(BF16)
        wub_ref[...] = wu_ref[0].astype(BF16)
        wdb_ref[...] = wd_ref[0].astype(BF16)

    @pl.when(b < nb_ref[0])
    def _():
        lo, hi = _unpack_bf16_pairs(xs_ref[...])
        gate = _dot(lo, wgb_ref[:HALF, :]) + _dot(hi, wgb_ref[HALF:, :])
        up = _dot(lo, wub_ref[:HALF, :]) + _dot(hi, wub_ref[HALF:, :])
        hb = (jax.nn.silu(gate) * up).astype(BF16)
        ys_ref[...] = _pack_bf16_pairs(_dot(hb, wdb_ref[...]).astype(BF16))


def _experts(block_e, nb_used, xs, wg, wu, wd):
    n_rows = xs.shape[0]
    nb = n_rows // MOE_ROWS
    rows = lambda b, be, nbu: (jnp.minimum(b, nbu[0] - 1), 0)
    wsel = lambda b, be, nbu: (be[b], 0, 0)
    grid_spec = pltpu.PrefetchScalarGridSpec(
        num_scalar_prefetch=2,
        grid=(nb,),
        in_specs=[pl.BlockSpec((MOE_ROWS, HALF), rows),
                  pl.BlockSpec((1, D_MODEL, D_EXPERT), wsel),
                  pl.BlockSpec((1, D_MODEL, D_EXPERT), wsel),
                  pl.BlockSpec((1, D_EXPERT, D_MODEL), wsel)],
        out_specs=pl.BlockSpec((MOE_ROWS, HALF), rows),
        scratch_shapes=[pltpu.VMEM((D_MODEL, D_EXPERT), BF16), pltpu.VMEM((D_MODEL, D_EXPERT), BF16),
                        pltpu.VMEM((D_EXPERT, D_MODEL), BF16)],
    )
    return pl.pallas_call(
        _experts_kernel,
        grid_spec=grid_spec,
        out_shape=jax.ShapeDtypeStruct((n_rows, HALF), U32),
        compiler_params=_params(("arbitrary",)),
        name="moe_experts",
    )(block_e, nb_used, xs, wg, wu, wd)


def _combine_kernel(dest_ref, w_ref, h_ref, wgs_ref, wus_ref, wds_ref, g_ref, b_ref,
                    ys_ref, o_ref, buf_ref, sem, *, tm):
    def issue(t, c):
        for k in range(TOP_K):
            _row_copy(ys_ref, dest_ref[0, k, t], buf_ref.at[k], t, sem).start()
        return c

    def drain(t, c):
        for k in range(TOP_K):
            _row_copy(ys_ref, 0, buf_ref.at[k], 0, sem).wait()
        return c

    lax.fori_loop(0, tm, issue, 0)

    h = h_ref[...]
    hb = h.astype(BF16)
    hid = (jax.nn.silu(_dot(hb, wgs_ref[...])) * _dot(hb, wus_ref[...])).astype(BF16)
    shared = _dot(hid, wds_ref[...])

    lax.fori_loop(0, tm, drain, 0)

    w = w_ref[...]
    lo_sum = jnp.zeros((tm, HALF), F32)
    hi_sum = jnp.zeros((tm, HALF), F32)
    for k in range(TOP_K):
        p = buf_ref[k]
        lo_sum = lo_sum + lax.bitcast_convert_type(p << 16, F32) * w[:, k:k + 1]
        hi_sum = hi_sum + lax.bitcast_convert_type(p & jnp.uint32(0xFFFF0000), F32) * w[:, k:k + 1]
    routed = jnp.concatenate([lo_sum, hi_sum], axis=1)
    o_ref[...] = _layer_norm(DN_ALPHA * h + (routed + shared), g_ref[...], b_ref[...])


def _combine(dest3, w_tok, h, wgs, wus, wds, ln_g, ln_b, ys3, tm):
    T = h.shape[0]
    per = dest3.shape[2] // tm
    kern = functools.partial(_combine_kernel, tm=tm)
    row = lambda i: (i, 0)
    fixed = lambda i: (0, 0)
    return pl.pallas_call(
        kern,
        grid=(T // tm,),
        in_specs=[pl.BlockSpec((1, TOP_K, tm), lambda i: (i // per, 0, i % per),
                               memory_space=pltpu.SMEM),
                  pl.BlockSpec((tm, TOP_K), row),
                  pl.BlockSpec((tm, D_MODEL), row),
                  pl.BlockSpec((D_MODEL, D_SHARED), fixed),
                  pl.BlockSpec((D_MODEL, D_SHARED), fixed),
                  pl.BlockSpec((D_SHARED, D_MODEL), fixed),
                  pl.BlockSpec((1, D_MODEL), fixed),
                  pl.BlockSpec((1, D_MODEL), fixed),
                  pl.BlockSpec(memory_space=pl.ANY)],
        out_specs=pl.BlockSpec((tm, D_MODEL), row),
        out_shape=jax.ShapeDtypeStruct((T, D_MODEL), F32),
        scratch_shapes=[pltpu.VMEM((TOP_K, tm, HALF), U32), pltpu.SemaphoreType.DMA],
        compiler_params=_params(("arbitrary",)),
        name="moe_combine",
    )(dest3, w_tok, h, wgs, wus, wds, ln_g.reshape(1, D_MODEL), ln_b.reshape(1, D_MODEL), ys3)


def _tile(n, pref):
    t = min(n, pref)
    assert n % t == 0, (n, pref)
    return t


def _layer(x, w_in, conv_w, conv_b, lru_wa, lru_ba, lru_wi, lru_bi, lru_lambda, ret_gn_gain,
           w_lru_out, w_ret_out, b_gate, w_o, ln1_g, ln1_b, w_router, router_bias,
           w_gate_e, w_up_e, w_down_e, w_gate_s, w_up_s, w_down_s, ln2_g, ln2_b):
    B, S, D = x.shape
    T = B * S
    x2 = x.reshape(T, D)
    xb = x2.astype(BF16)

    half = RET_DK // 2
    freq = ROPE_THETA ** (-jnp.arange(half, dtype=F32) / half)
    ang = jnp.arange(S, dtype=I32).astype(F32)[:, None] * freq
    cos, sin = jnp.cos(ang), jnp.sin(ang)

    tm_big = _tile(S, 1024)
    qk = _proj_qk(xb, w_in, cos, sin, S, tm_big, 1024)
    v = _proj_plain(xb, w_in, OFF_V, D_RV, BF16, tm_big, 1024, "v_proj")
    g = _proj_plain(xb, w_in, OFF_G, D_RV, F32, tm_big, 1024, "g_proj")

    ua = _lru_branch(xb, w_in, conv_w, conv_b, lru_wa.astype(BF16), lru_ba, lru_wi.astype(BF16),
                     lru_bi, lru_lambda, S, _tile(S, 256), 512)
    ub = _retention(qk, v, g, ret_gn_gain, B, S)

    mixed = _mix(ua, ub, xb, w_lru_out.astype(BF16), w_ret_out.astype(BF16),
                 w_in[:, OFF_GA:].astype(BF16), b_gate, _tile(T, 512), 512)

    tm_r = _tile(T, 256)
    h, hp3, eidx, rank, wsel, cnt = _ln1_route(mixed, w_o.astype(BF16), x2, ln1_g, ln1_b,
                                               w_router.T.astype(BF16), router_bias, tm_r)

    counts = cnt[:, 0].astype(I32)
    padded = (counts + MOE_ROWS - 1) // MOE_ROWS * MOE_ROWS
    pad_end = jnp.cumsum(padded)
    pad_start = pad_end - padded
    nb = (T * TOP_K + N_EXPERTS * (MOE_ROWS - 1) + MOE_ROWS - 1) // MOE_ROWS
    n_rows = nb * MOE_ROWS
    block_row0 = jnp.arange(nb, dtype=I32) * MOE_ROWS
    block_e = jnp.minimum(jnp.sum((pad_end[None, :] <= block_row0[:, None]).astype(I32), axis=1),
                          N_EXPERTS - 1)
    nb_used = (pad_end[-1:] // MOE_ROWS).astype(I32)

    dest = _dest_rows(pad_start, eidx, rank)
    xs3 = _dispatch(dest, pad_start + counts, pad_end, hp3, n_rows)
    ys3 = _experts(block_e, nb_used, xs3, w_gate_e, w_up_e, w_down_e)
    out = _combine(dest, wsel.T, h, w_gate_s.astype(BF16), w_up_s.astype(BF16),
                   w_down_s.astype(BF16), ln2_g, ln2_b, ys3, tm_r)
    return out.reshape(B, S, D)


def kernel(x, w_in, conv_w, conv_b, lru_wa, lru_ba, lru_wi, lru_bi, lru_lambda, ret_gn_gain, w_lru_out, w_ret_out, b_gate, w_o, ln1_g, ln1_b, w_router, router_bias, w_gate_e, w_up_e, w_down_e, w_gate_s, w_up_s, w_down_s, ln2_g, ln2_b):
    assert DEPTH == 1 and w_in.shape[0] == DEPTH
    args = (w_in, conv_w, conv_b, lru_wa, lru_ba, lru_wi, lru_bi, lru_lambda, ret_gn_gain,
            w_lru_out, w_ret_out, b_gate, w_o, ln1_g, ln1_b, w_router, router_bias,
            w_gate_e, w_up_e, w_down_e, w_gate_s, w_up_s, w_down_s, ln2_g, ln2_b)
    return _layer(x, *[a[0] for a in args])
```

```python
import functools

import jax
import jax.numpy as jnp
from jax import lax
from jax.experimental import pallas as pl
from jax.experimental.pallas import tpu as pltpu

F32 = jnp.float32
BF16 = jnp.bfloat16
I32 = jnp.int32
U32 = jnp.uint32

D_MODEL = 2048
D_LRU = 2048
LRU_HEADS = 16
LRU_BLOCK = D_LRU // LRU_HEADS
CONV_WIDTH = 4
LRU_C = 8.0
RET_HEADS = 8
RET_DK = 256
RET_DV = 512
D_QK = RET_HEADS * RET_DK
D_RV = RET_HEADS * RET_DV
ROPE_THETA = 10000.0
N_EXPERTS = 64
TOP_K = 8
N_GROUPS = 8
GROUP_SIZE = N_EXPERTS // N_GROUPS
TOPK_GROUPS = 4
D_EXPERT = 512
D_SHARED = 512
ROUTED_SCALE = 2.5
DEPTH = 1
DN_ALPHA = (2.0 * DEPTH) ** 0.25
LN_EPS = 1e-5

OFF_LRU_X = 0
OFF_LRU_Y = OFF_LRU_X + D_LRU
OFF_Q = OFF_LRU_Y + D_LRU
OFF_K = OFF_Q + D_QK
OFF_V = OFF_K + D_QK
OFF_G = OFF_V + D_RV
OFF_GA = OFF_G + D_RV
OFF_GB = OFF_GA + D_MODEL

V7X_VMEM_LIMIT = 56 * 1024 * 1024
LANES = 128
SUBLANES = 8
MOE_ROWS = 512
RET_CHUNK = 256
HALF = D_MODEL // 2


def _params(sem):
    return pltpu.CompilerParams(dimension_semantics=sem, vmem_limit_bytes=V7X_VMEM_LIMIT)


def _dot(a, b):
    return jnp.dot(a, b, preferred_element_type=F32)


def _cast_weight(w_ref, wb_ref):
    @pl.when(pl.program_id(1) == 0)
    def _():
        wb_ref[...] = w_ref[...].astype(BF16)


def _mm_kernel(x_ref, w_ref, o_ref, wb_ref):
    _cast_weight(w_ref, wb_ref)
    o_ref[...] = _dot(x_ref[...], wb_ref[...]).astype(o_ref.dtype)


def _proj_plain(xb, w, col0, ncols, out_dtype, tm, tn, name):
    T, K = xb.shape
    jb0 = col0 // tn
    return pl.pallas_call(
        _mm_kernel,
        grid=(ncols // tn, T // tm),
        in_specs=[pl.BlockSpec((tm, K), lambda j, i: (i, 0)),
                  pl.BlockSpec((K, tn), lambda j, i: (0, jb0 + j))],
        out_specs=pl.BlockSpec((tm, tn), lambda j, i: (i, j)),
        out_shape=jax.ShapeDtypeStruct((T, ncols), out_dtype),
        scratch_shapes=[pltpu.VMEM((K, tn), BF16)],
        compiler_params=_params(("parallel", "arbitrary")),
        name=name,
    )(xb, w)


def _qk_kernel(x_ref, w_ref, cos_ref, sin_ref, o_ref, wb_ref, *, heads_per_tile, k_tile0):
    _cast_weight(w_ref, wb_ref)
    j = pl.program_id(0)
    acc = _dot(x_ref[...], wb_ref[...])
    scale = jnp.where(j >= k_tile0, RET_DK ** -0.5, 1.0).astype(F32)
    cos = cos_ref[...]
    sin = sin_ref[...]
    half = RET_DK // 2
    for hh in range(heads_per_tile):
        c0 = hh * RET_DK
        t1 = acc[:, c0:c0 + half]
        t2 = acc[:, c0 + half:c0 + RET_DK]
        o_ref[:, c0:c0 + half] = ((t1 * cos - t2 * sin) * scale).astype(o_ref.dtype)
        o_ref[:, c0 + half:c0 + RET_DK] = ((t1 * sin + t2 * cos) * scale).astype(o_ref.dtype)


def _proj_qk(xb, w, cos, sin, seq, tm, tn):
    T, K = xb.shape
    ncols = 2 * D_QK
    jb0 = OFF_Q // tn
    n_seq_tiles = seq // tm
    kern = functools.partial(_qk_kernel, heads_per_tile=tn // RET_DK, k_tile0=D_QK // tn)
    return pl.pallas_call(
        kern,
        grid=(ncols // tn, T // tm),
        in_specs=[pl.BlockSpec((tm, K), lambda j, i: (i, 0)),
                  pl.BlockSpec((K, tn), lambda j, i: (0, jb0 + j)),
                  pl.BlockSpec((tm, RET_DK // 2), lambda j, i: (i % n_seq_tiles, 0)),
                  pl.BlockSpec((tm, RET_DK // 2), lambda j, i: (i % n_seq_tiles, 0))],
        out_specs=pl.BlockSpec((tm, tn), lambda j, i: (i, j)),
        out_shape=jax.ShapeDtypeStruct((T, ncols), BF16),
        scratch_shapes=[pltpu.VMEM((K, tn), BF16)],
        compiler_params=_params(("parallel", "arbitrary")),
        name="qk_rope",
    )(xb, w, cos, sin)


def _lru_kernel(x_ref, wx_ref, wy_ref, cw_ref, cb_ref, wa_ref, ba_ref, wi_ref, bi_ref, lam_ref,
                o_ref, wxb_ref, wyb_ref, perm_ref, unperm_ref, tail_ref, hc_ref, a_ref, b_ref, gy_ref,
                *, tiles_per_seq, tm, width):
    i = pl.program_id(1)
    first = (i % tiles_per_seq) == 0
    seg = tm // SUBLANES
    halo = (CONV_WIDTH - 1) * SUBLANES

    @pl.when(i == 0)
    def _():
        wxb_ref[...] = wx_ref[...].astype(BF16)
        wyb_ref[...] = wy_ref[...].astype(BF16)
        r = lax.broadcasted_iota(I32, (tm, tm), 0)
        c = lax.broadcasted_iota(I32, (tm, tm), 1)
        perm_ref[...] = jnp.where(c == (r & (SUBLANES - 1)) * seg + (r >> 3), 1.0, 0.0).astype(BF16)
        unperm_ref[...] = jnp.where(r == (c & (SUBLANES - 1)) * seg + (c >> 3), 1.0, 0.0).astype(BF16)

    @pl.when(first)
    def _():
        tail_ref[...] = jnp.zeros_like(tail_ref)
        hc_ref[...] = jnp.zeros_like(hc_ref)

    xp = _dot(perm_ref[...], x_ref[...]).astype(BF16)
    px = _dot(xp, wxb_ref[...])
    gy_ref[...] = jax.nn.gelu(_dot(xp, wyb_ref[...]), approximate=True)

    cw = cw_ref[...]
    cb = cb_ref[...]
    taps = [cw[CONV_WIDTH - 1 - d:CONV_WIDTH - d] for d in range(CONV_WIDTH)]
    xa = cb + taps[0] * px
    for d in range(1, CONV_WIDTH):
        back = jnp.concatenate([px[tm - d * SUBLANES:], px[:tm - d * SUBLANES]], axis=0)
        xa = xa + taps[d] * back
    sub = lax.broadcasted_iota(I32, (SUBLANES, width), 0)
    groups = {}
    for g in range(CONV_WIDTH - 1):
        rows = slice(g * SUBLANES, (g + 1) * SUBLANES)
        cur_end = px[tm - halo + g * SUBLANES:tm - halo + (g + 1) * SUBLANES]
        groups[g - (CONV_WIDTH - 1)] = jnp.where(sub == 0, pltpu.roll(tail_ref[rows, :], 1, 0),
                                                 pltpu.roll(cur_end, 1, 0))
        groups[g] = px[rows, :]
    head = []
    for q in range(CONV_WIDTH - 1):
        acc = cb + taps[0] * groups[q]
        for d in range(1, CONV_WIDTH):
            acc = acc + taps[d] * groups[q - d]
        head.append(acc)
    xa = jnp.concatenate(head + [xa[halo:]], axis=0)
    tail_ref[...] = px[tm - halo:]

    xab = xa.astype(BF16)
    r_parts, i_parts = [], []
    for hh in range(width // LRU_BLOCK):
        xh = xab[:, hh * LRU_BLOCK:(hh + 1) * LRU_BLOCK]
        r_parts.append(_dot(xh, wa_ref[hh]))
        i_parts.append(_dot(xh, wi_ref[hh]))
    rg = jax.nn.sigmoid(jnp.concatenate(r_parts, axis=1) + ba_ref[...])
    ig = jax.nn.sigmoid(jnp.concatenate(i_parts, axis=1) + bi_ref[...])

    log_a = -LRU_C * rg * jax.nn.softplus(-lam_ref[...])
    a = jnp.exp(log_a)
    mult = jnp.sqrt(-jnp.tanh(log_a) * (a * a + 1.0))
    row = lax.broadcasted_iota(I32, (tm, width), 0)
    mult = jnp.where(jnp.logical_and(first, row == 0), 1.0, mult)
    a_ref[...] = a
    b_ref[...] = mult * ig * xa

    run_a = jnp.ones((SUBLANES, width), F32)
    run_h = jnp.zeros((SUBLANES, width), F32)
    for q in range(seg):
        rows = slice(q * SUBLANES, (q + 1) * SUBLANES)
        aq = a_ref[rows, :]
        run_h = aq * run_h + b_ref[rows, :]
        run_a = aq * run_a
        b_ref[rows, :] = run_h
        a_ref[rows, :] = run_a

    carry = hc_ref[...]
    carries = []
    for s in range(SUBLANES):
        carries.append(carry)
        carry = run_a[s:s + 1, :] * carry + run_h[s:s + 1, :]
    hc_ref[...] = carry
    carry_in = jnp.concatenate(carries, axis=0)
    for q in range(seg):
        rows = slice(q * SUBLANES, (q + 1) * SUBLANES)
        b_ref[rows, :] = (b_ref[rows, :] + a_ref[rows, :] * carry_in) * gy_ref[rows, :]

    o_ref[...] = _dot(unperm_ref[...], b_ref[...].astype(BF16)).astype(o_ref.dtype)


def _lru_branch(xb, w, conv_w, conv_b, wa, ba, wi, bi, lam, seq, tm, width):
    T, K = xb.shape
    nj = D_LRU // width
    jy0 = OFF_LRU_Y // width
    hp = width // LRU_BLOCK
    kern = functools.partial(_lru_kernel, tiles_per_seq=seq // tm, tm=tm, width=width)
    vec = lambda j, i: (0, j)
    return pl.pallas_call(
        kern,
        grid=(nj, T // tm),
        in_specs=[pl.BlockSpec((tm, K), lambda j, i: (i, 0)),
                  pl.BlockSpec((K, width), lambda j, i: (0, j)),
                  pl.BlockSpec((K, width), lambda j, i: (0, jy0 + j)),
                  pl.BlockSpec((CONV_WIDTH, width), vec),
                  pl.BlockSpec((1, width), vec),
                  pl.BlockSpec((hp, LRU_BLOCK, LRU_BLOCK), lambda j, i: (j, 0, 0)),
                  pl.BlockSpec((1, width), vec),
                  pl.BlockSpec((hp, LRU_BLOCK, LRU_BLOCK), lambda j, i: (j, 0, 0)),
                  pl.BlockSpec((1, width), vec),
                  pl.BlockSpec((1, width), vec)],
        out_specs=pl.BlockSpec((tm, width), lambda j, i: (i, j)),
        out_shape=jax.ShapeDtypeStruct((T, D_LRU), BF16),
        scratch_shapes=[pltpu.VMEM((K, width), BF16), pltpu.VMEM((K, width), BF16),
                        pltpu.VMEM((tm, tm), BF16), pltpu.VMEM((tm, tm), BF16),
                        pltpu.VMEM(((CONV_WIDTH - 1) * SUBLANES, width), F32),
                        pltpu.VMEM((1, width), F32),
                        pltpu.VMEM((tm, width), F32), pltpu.VMEM((tm, width), F32),
                        pltpu.VMEM((tm, width), F32)],
        compiler_params=_params(("parallel", "arbitrary")),
        name="lru_branch",
    )(xb, w, w, conv_w, conv_b.reshape(1, D_LRU), wa, ba.reshape(1, D_LRU), wi, bi.reshape(1, D_LRU),
      lam.reshape(1, D_LRU))


def _ret_kernel(gc_ref, q_ref, k_ref, v_ref, g_ref, gain_ref, dm_ref, xi_ref, zeta_ref, o_ref, st_ref):
    @pl.when(pl.program_id(1) == 0)
    def _():
        st_ref[...] = jnp.zeros_like(st_ref)

    for hd in range(RET_HEADS):
        qc = slice(hd * RET_DK, (hd + 1) * RET_DK)
        vc = slice(hd * RET_DV, (hd + 1) * RET_DV)
        q = q_ref[:, qc]
        k = k_ref[:, qc]
        v = v_ref[:, vc]
        st = st_ref[hd]
        s = lax.dot_general(q, k, (((1,), (1,)), ((), ())), preferred_element_type=F32) * dm_ref[hd]
        o = _dot(s.astype(BF16), v) + _dot(q, st.astype(BF16)) * xi_ref[hd]
        kz = (k.astype(F32) * zeta_ref[hd]).astype(BF16)
        st_ref[hd] = gc_ref[hd] * st + lax.dot_general(kz, v, (((0,), (0,)), ((), ())),
                                                       preferred_element_type=F32)
        mu = jnp.mean(o, axis=-1, keepdims=True)
        oc = o - mu
        var = jnp.mean(oc * oc, axis=-1, keepdims=True)
        oh = oc * lax.rsqrt(var + LN_EPS) * gain_ref[:, vc]
        o_ref[:, vc] = (jax.nn.silu(g_ref[:, vc]) * oh).astype(o_ref.dtype)


def _retention(qk, v, g, gain, batch, seq):
    T = batch * seq
    C = min(RET_CHUNK, seq)
    nchunk = seq // C
    H = RET_HEADS
    log_g = jnp.log1p(-jnp.exp2(-5.0 - jnp.arange(H, dtype=F32)))
    idx = jnp.arange(C, dtype=F32)
    diff = idx[:, None] - idx[None, :]
    dmask = jnp.where(diff >= 0, jnp.exp(jnp.maximum(diff, 0.0)[None] * log_g[:, None, None]), 0.0)
    xi = jnp.exp((idx[None] + 1.0) * log_g[:, None])[:, :, None]
    zeta = jnp.exp((C - 1.0 - idx[None]) * log_g[:, None])[:, :, None]
    g_c = jnp.exp(C * log_g)
    rows = lambda b, n: b * nchunk + n
    whole3 = lambda b, n: (0, 0, 0)
    return pl.pallas_call(
        _ret_kernel,
        grid=(batch, nchunk),
        in_specs=[pl.BlockSpec(memory_space=pltpu.SMEM),
                  pl.BlockSpec((C, D_QK), lambda b, n: (rows(b, n), 0)),
                  pl.BlockSpec((C, D_QK), lambda b, n: (rows(b, n), 1)),
                  pl.BlockSpec((C, D_RV), lambda b, n: (rows(b, n), 0)),
                  pl.BlockSpec((C, D_RV), lambda b, n: (rows(b, n), 0)),
                  pl.BlockSpec((1, D_RV), lambda b, n: (0, 0)),
                  pl.BlockSpec((H, C, C), whole3),
                  pl.BlockSpec((H, C, 1), whole3),
                  pl.BlockSpec((H, C, 1), whole3)],
        out_specs=pl.BlockSpec((C, D_RV), lambda b, n: (rows(b, n), 0)),
        out_shape=jax.ShapeDtypeStruct((T, D_RV), BF16),
        scratch_shapes=[pltpu.VMEM((H, RET_DK, RET_DV), F32)],
        compiler_params=_params(("parallel", "arbitrary")),
        name="retention",
    )(g_c, qk, qk, v, g, gain.reshape(1, D_RV), dmask, xi, zeta)


def _mix_kernel(ua_ref, ub_ref, x_ref, wlo_ref, wro_ref, wga_ref, wgb_ref, bga_ref, bgb_ref, o_ref):
    x = x_ref[...]
    ya = _dot(ua_ref[...], wlo_ref[...])
    yb = _dot(ub_ref[...], wro_ref[...])
    ga = jax.nn.sigmoid(_dot(x, wga_ref[...]) + bga_ref[...])
    gb = jax.nn.sigmoid(_dot(x, wgb_ref[...]) + bgb_ref[...])
    o_ref[...] = (ga * ya + gb * yb).astype(o_ref.dtype)


def _mix(ua, ub, xb, w_lru_out, w_ret_out, w_gates, b_gate, tm, tn):
    T = xb.shape[0]
    nj = D_MODEL // tn
    return pl.pallas_call(
        _mix_kernel,
        grid=(T // tm, nj),
        in_specs=[pl.BlockSpec((tm, D_LRU), lambda i, j: (i, 0)),
                  pl.BlockSpec((tm, D_RV), lambda i, j: (i, 0)),
                  pl.BlockSpec((tm, D_MODEL), lambda i, j: (i, 0)),
                  pl.BlockSpec((D_LRU, tn), lambda i, j: (0, j)),
                  pl.BlockSpec((D_RV, tn), lambda i, j: (0, j)),
                  pl.BlockSpec((D_MODEL, tn), lambda i, j: (0, j)),
                  pl.BlockSpec((D_MODEL, tn), lambda i, j: (0, nj + j)),
                  pl.BlockSpec((1, tn), lambda i, j: (0, j)),
                  pl.BlockSpec((1, tn), lambda i, j: (0, nj + j))],
        out_specs=pl.BlockSpec((tm, tn), lambda i, j: (i, j)),
        out_shape=jax.ShapeDtypeStruct((T, D_MODEL), BF16),
        compiler_params=_params(("parallel", "parallel")),
        name="gated_mix",
    )(ua, ub, xb, w_lru_out, w_ret_out, w_gates, w_gates, b_gate.reshape(1, 2 * D_MODEL),
      b_gate.reshape(1, 2 * D_MODEL))


def _layer_norm(y, g, b):
    mu = jnp.mean(y, axis=-1, keepdims=True)
    yc = y - mu
    var = jnp.mean(yc * yc, axis=-1, keepdims=True)
    return yc * lax.rsqrt(var + LN_EPS) * g + b


def _pack_bf16_pairs(hb):
    lo = lax.bitcast_convert_type(hb[:, :HALF].astype(F32), U32)
    hi = lax.bitcast_convert_type(hb[:, HALF:].astype(F32), U32)
    return (hi & jnp.uint32(0xFFFF0000)) | (lo >> 16)


def _unpack_bf16_pairs(p):
    lo = lax.bitcast_convert_type(p << 16, F32).astype(BF16)
    hi = lax.bitcast_convert_type(p & jnp.uint32(0xFFFF0000), F32).astype(BF16)
    return lo, hi


def _max2(a):
    return jnp.max(jnp.max(a, axis=1, keepdims=True), axis=0, keepdims=True)


def _min2(a):
    return jnp.min(jnp.min(a, axis=1, keepdims=True), axis=0, keepdims=True)


def _sum2(a):
    return jnp.sum(jnp.sum(a, axis=1, keepdims=True), axis=0, keepdims=True)


def _ln1_route_kernel(mix_ref, wo_ref, x_ref, g_ref, b_ref, wr_ref, rb_ref,
                      h_ref, hp_ref, eidx_ref, rank_ref, wsel_ref, cnt_ref, cnt_sc, *, tm):
    i = pl.program_id(0)

    @pl.when(i == 0)
    def _():
        cnt_sc[...] = jnp.zeros_like(cnt_sc)

    y = DN_ALPHA * x_ref[...] + _dot(mix_ref[...], wo_ref[...])
    h = _layer_norm(y, g_ref[...], b_ref[...])
    h_ref[...] = h
    hb = h.astype(BF16)
    hp_ref[...] = _pack_bf16_pairs(hb)

    G, GS = N_GROUPS, GROUP_SIZE
    logits = lax.dot_general(wr_ref[...], hb, (((1,), (1,)), ((), ())), preferred_element_type=F32)
    scores = jax.nn.sigmoid(logits)
    s3 = scores.reshape(G, GS, tm)
    b3 = (scores + rb_ref[...]).reshape(G, GS, tm)
    neg = jnp.float32(-jnp.inf)

    in_grp = lax.broadcasted_iota(I32, (G, GS, tm), 1)
    m1 = jnp.max(b3, axis=1, keepdims=True)
    f1 = jnp.min(jnp.where(b3 == m1, in_grp, GS), axis=1, keepdims=True)
    m2 = jnp.max(jnp.where(in_grp == f1, neg, b3), axis=1, keepdims=True)
    grp = m1 + m2

    gi = lax.broadcasted_iota(I32, (G, 1, tm), 0)
    gkeep = jnp.zeros((G, 1, tm), F32)
    cur = grp
    for _ in range(TOPK_GROUPS):
        m = jnp.max(cur, axis=0, keepdims=True)
        f = jnp.min(jnp.where(cur == m, gi, G), axis=0, keepdims=True)
        hit = gi == f
        gkeep = jnp.where(hit, 1.0, gkeep)
        cur = jnp.where(hit, neg, cur)
    emask = jnp.broadcast_to(gkeep, (G, GS, tm)) > 0.0

    ei = lax.broadcasted_iota(I32, (G, GS, tm), 0) * GS + in_grp
    cur = jnp.where(emask, b3, neg)
    hits, idxs, ws = [], [], []
    for _ in range(TOP_K):
        m = _max2(cur)
        f = _min2(jnp.where(cur == m, ei, N_EXPERTS))
        hit = ei == f
        cur = jnp.where(hit, neg, cur)
        hits.append(hit)
        idxs.append(f)
        ws.append(_sum2(jnp.where(hit, s3, 0.0)))
    wsum = ws[0]
    for r in range(1, TOP_K):
        wsum = wsum + ws[r]

    sel = jnp.zeros((G, GS, tm), F32)
    for hit in hits:
        sel = jnp.where(hit, 1.0, sel)
    sel2 = sel.reshape(N_EXPERTS, tm)
    tri = (lax.broadcasted_iota(I32, (tm, tm), 0) < lax.broadcasted_iota(I32, (tm, tm), 1))
    rank_local = _dot(sel2.astype(BF16), jnp.where(tri, 1.0, 0.0).astype(BF16))
    cnt = cnt_sc[...]
    rank3 = (rank_local + cnt[:, 0:1]).reshape(G, GS, tm)
    cnt_new = cnt + jnp.sum(sel2, axis=1, keepdims=True)
    cnt_sc[...] = cnt_new
    cnt_ref[...] = cnt_new

    for r in range(TOP_K):
        eidx_ref[0, r:r + 1, :] = idxs[r].reshape(1, tm)
        rank_ref[0, r:r + 1, :] = _sum2(jnp.where(hits[r], rank3, 0.0)).reshape(1, tm).astype(I32)
        wsel_ref[r:r + 1, :] = (ws[r] / wsum * ROUTED_SCALE).reshape(1, tm)


def _ln1_route(mixed, w_o, x2, ln_g, ln_b, w_router_t, router_bias, tm):
    T = x2.shape[0]
    kern = functools.partial(_ln1_route_kernel, tm=tm)
    row = lambda i: (i, 0)
    fixed = lambda i: (0, 0)
    col = lambda i: (0, i)
    tile3 = lambda i: (i, 0, 0)
    return pl.pallas_call(
        kern,
        grid=(T // tm,),
        in_specs=[pl.BlockSpec((tm, D_MODEL), row),
                  pl.BlockSpec((D_MODEL, D_MODEL), fixed),
                  pl.BlockSpec((tm, D_MODEL), row),
                  pl.BlockSpec((1, D_MODEL), fixed),
                  pl.BlockSpec((1, D_MODEL), fixed),
                  pl.BlockSpec((N_EXPERTS, D_MODEL), fixed),
                  pl.BlockSpec((N_EXPERTS, 1), fixed)],
        out_specs=[pl.BlockSpec((tm, D_MODEL), row),
                   pl.BlockSpec((tm, HALF), row),
                   pl.BlockSpec((1, TOP_K, tm), tile3),
                   pl.BlockSpec((1, TOP_K, tm), tile3),
                   pl.BlockSpec((TOP_K, tm), col),
                   pl.BlockSpec((N_EXPERTS, LANES), fixed)],
        out_shape=[jax.ShapeDtypeStruct((T, D_MODEL), F32),
                   jax.ShapeDtypeStruct((T, HALF), U32),
                   jax.ShapeDtypeStruct((T // tm, TOP_K, tm), I32),
                   jax.ShapeDtypeStruct((T // tm, TOP_K, tm), I32),
                   jax.ShapeDtypeStruct((TOP_K, T), F32),
                   jax.ShapeDtypeStruct((N_EXPERTS, LANES), F32)],
        scratch_shapes=[pltpu.VMEM((N_EXPERTS, LANES), F32)],
        compiler_params=_params(("arbitrary",)),
        name="ln1_route",
    )(mixed, w_o, x2, ln_g.reshape(1, D_MODEL), ln_b.reshape(1, D_MODEL), w_router_t,
      router_bias.reshape(N_EXPERTS, 1))


def _dest_kernel(ps_ref, eidx_ref, rank_ref, dest_ref):
    eidx = eidx_ref[...]
    dest = rank_ref[...]
    for e in range(N_EXPERTS):
        dest = dest + jnp.where(eidx == e, ps_ref[e], 0)
    dest_ref[...] = dest


def _dest_rows(pad_start, eidx3, rank3):
    vmem = pl.BlockSpec(memory_space=pltpu.VMEM)
    return pl.pallas_call(
        _dest_kernel,
        in_specs=[pl.BlockSpec(memory_space=pltpu.SMEM), vmem, vmem],
        out_specs=vmem,
        out_shape=jax.ShapeDtypeStruct(eidx3.shape, I32),
        name="moe_dest",
    )(pad_start, eidx3, rank3)


def _row_copy(src_ref, src_row, dst_ref, dst_row, sem):
    return pltpu.make_async_copy(src_ref.at[pl.ds(src_row, 1)], dst_ref.at[pl.ds(dst_row, 1)], sem)


def _dispatch_kernel(dest_ref, fill_ref, pe_ref, hp_ref, xs_ref, zero_ref, sem, zsem, *, tm):
    @pl.when(pl.program_id(0) == 0)
    def _():
        zero_ref[...] = jnp.zeros_like(zero_ref)

        def group_copy(r):
            return pltpu.make_async_copy(zero_ref, xs_ref.at[pl.ds(pl.multiple_of(r, SUBLANES), SUBLANES)],
                                         zsem)

        def fill_expert(e, counts):
            n_rows, n_groups = counts
            lo = fill_ref[e]
            hi = pe_ref[e]
            mid = jnp.minimum((lo + SUBLANES - 1) // SUBLANES * SUBLANES, hi)
            groups = (hi - mid) // SUBLANES

            def fill_row(r, c):
                _row_copy(zero_ref, 0, xs_ref, r, zsem).start()
                return c

            def fill_group(g, c):
                group_copy(mid + g * SUBLANES).start()
                return c

            lax.fori_loop(lo, mid, fill_row, 0)
            lax.fori_loop(0, groups, fill_group, 0)
            return n_rows + (mid - lo), n_groups + groups

        n_rows, n_groups = lax.fori_loop(0, N_EXPERTS, fill_expert, (0, 0))

        def drain_row(r, c):
            _row_copy(zero_ref, 0, xs_ref, 0, zsem).wait()
            return c

        def drain_group(g, c):
            group_copy(0).wait()
            return c

        lax.fori_loop(0, n_rows, drain_row, 0)
        lax.fori_loop(0, n_groups, drain_group, 0)

    def issue(t, c):
        for k in range(TOP_K):
            _row_copy(hp_ref, t, xs_ref, dest_ref[0, k, t], sem).start()
        return c

    def drain(t, c):
        for k in range(TOP_K):
            _row_copy(hp_ref, 0, xs_ref, 0, sem).wait()
        return c

    lax.fori_loop(0, tm, issue, 0)
    lax.fori_loop(0, tm, drain, 0)


def _dispatch(dest3, fill_start, pad_end, hp, n_rows):
    T = hp.shape[0]
    tm = dest3.shape[2]
    kern = functools.partial(_dispatch_kernel, tm=tm)
    smem = pl.BlockSpec(memory_space=pltpu.SMEM)
    return pl.pallas_call(
        kern,
        grid=(T // tm,),
        in_specs=[pl.BlockSpec((1, TOP_K, tm), lambda i: (i, 0, 0), memory_space=pltpu.SMEM),
                  smem, smem,
                  pl.BlockSpec((tm, HALF), lambda i: (i, 0))],
        out_specs=pl.BlockSpec(memory_space=pl.ANY),
        out_shape=jax.ShapeDtypeStruct((n_rows, HALF), U32),
        scratch_shapes=[pltpu.VMEM((SUBLANES, HALF), U32), pltpu.SemaphoreType.DMA,
                        pltpu.SemaphoreType.DMA],
        compiler_params=_params(("arbitrary",)),
        name="moe_dispatch",
    )(dest3, fill_start, pad_end, hp)


def _experts_kernel(be_ref, nb_ref, xs_ref, wg_ref, wu_ref, wd_ref, ys_ref, wgb_ref, wub_ref, wdb_ref):
    b = pl.program_id(0)
    new_expert = jnp.logical_or(b == 0, be_ref[b] != be_ref[jnp.maximum(b - 1, 0)])

    @pl.when(jnp.logical_and(b < nb_ref[0], new_expert))
    def _():
        wgb_ref[...] = wg_ref[0].astype(BF16)
        wub_ref[...] = wu_ref[0].astype(BF16)
        wdb_ref[...] = wd_ref[0].astype(BF16)

    @pl.when(b < nb_ref[0])
    def _():
        lo, hi = _unpack_bf16_pairs(xs_ref[...])
        gate = _dot(lo, wgb_ref[:HALF, :]) + _dot(hi, wgb_ref[HALF:, :])
        up = _dot(lo, wub_ref[:HALF, :]) + _dot(hi, wub_ref[HALF:, :])
        hb = (jax.nn.silu(gate) * up).astype(BF16)
        ys_ref[...] = _pack_bf16_pairs(_dot(hb, wdb_ref[...]).astype(BF16))


def _experts(block_e, nb_used, xs, wg, wu, wd):
    n_rows = xs.shape[0]
    nb = n_rows // MOE_ROWS
    rows = lambda b, be, nbu: (jnp.minimum(b, nbu[0] - 1), 0)
    wsel = lambda b, be, nbu: (be[b], 0, 0)
    grid_spec = pltpu.PrefetchScalarGridSpec(
        num_scalar_prefetch=2,
        grid=(nb,),
        in_specs=[pl.BlockSpec((MOE_ROWS, HALF), rows),
                  pl.BlockSpec((1, D_MODEL, D_EXPERT), wsel),
                  pl.BlockSpec((1, D_MODEL, D_EXPERT), wsel),
                  pl.BlockSpec((1, D_EXPERT, D_MODEL), wsel)],
        out_specs=pl.BlockSpec((MOE_ROWS, HALF), rows),
        scratch_shapes=[pltpu.VMEM((D_MODEL, D_EXPERT), BF16), pltpu.VMEM((D_MODEL, D_EXPERT), BF16),
                        pltpu.VMEM((D_EXPERT, D_MODEL), BF16)],
    )
    return pl.pallas_call(
        _experts_kernel,
        grid_spec=grid_spec,
        out_shape=jax.ShapeDtypeStruct((n_rows, HALF), U32),
        compiler_params=_params(("arbitrary",)),
        name="moe_experts",
    )(block_e, nb_used, xs, wg, wu, wd)


def _combine_kernel(dest_ref, next_ref, w_ref, h_ref, wgs_ref, wus_ref, wds_ref, g_ref, b_ref,
                    ys_ref, o_ref, buf_ref, sem, *, tm):
    i = pl.program_id(0)
    slot = i % 2

    def gather(idx_ref, dst_slot, t):
        for k in range(TOP_K):
            _row_copy(ys_ref, idx_ref[0, k, t], buf_ref.at[dst_slot, k], t, sem.at[dst_slot]).start()

    def wait_tile(s):
        pltpu.make_async_copy(buf_ref.at[s], buf_ref.at[s], sem.at[s]).wait()

    @pl.when(i == 0)
    def _():
        def first(t, c):
            gather(dest_ref, 0, t)
            return c

        lax.fori_loop(0, tm, first, 0)

    wait_tile(slot)
    for t in range(tm):
        gather(next_ref, 1 - slot, t)

    h = h_ref[...]
    hb = h.astype(BF16)
    hid = (jax.nn.silu(_dot(hb, wgs_ref[...])) * _dot(hb, wus_ref[...])).astype(BF16)
    shared = _dot(hid, wds_ref[...])

    w = w_ref[...]
    lo_sum = jnp.zeros((tm, HALF), F32)
    hi_sum = jnp.zeros((tm, HALF), F32)
    for k in range(TOP_K):
        p = buf_ref[slot, k]
        lo_sum = lo_sum + lax.bitcast_convert_type(p << 16, F32) * w[:, k:k + 1]
        hi_sum = hi_sum + lax.bitcast_convert_type(p & jnp.uint32(0xFFFF0000), F32) * w[:, k:k + 1]
    routed = jnp.concatenate([lo_sum, hi_sum], axis=1)
    o_ref[...] = _layer_norm(DN_ALPHA * h + (routed + shared), g_ref[...], b_ref[...])

    @pl.when(i == pl.num_programs(0) - 1)
    def _():
        wait_tile(1 - slot)


def _combine(dest3, w_tok, h, wgs, wus, wds, ln_g, ln_b, ys3):
    T = h.shape[0]
    n_tiles, _, tm = dest3.shape
    kern = functools.partial(_combine_kernel, tm=tm)
    row = lambda i: (i, 0)
    fixed = lambda i: (0, 0)
    return pl.pallas_call(
        kern,
        grid=(n_tiles,),
        in_specs=[pl.BlockSpec((1, TOP_K, tm), lambda i: (i, 0, 0), memory_space=pltpu.SMEM),
                  pl.BlockSpec((1, TOP_K, tm), lambda i: (jnp.minimum(i + 1, n_tiles - 1), 0, 0),
                               memory_space=pltpu.SMEM),
                  pl.BlockSpec((tm, TOP_K), row),
                  pl.BlockSpec((tm, D_MODEL), row),
                  pl.BlockSpec((D_MODEL, D_SHARED), fixed),
                  pl.BlockSpec((D_MODEL, D_SHARED), fixed),
                  pl.BlockSpec((D_SHARED, D_MODEL), fixed),
                  pl.BlockSpec((1, D_MODEL), fixed),
                  pl.BlockSpec((1, D_MODEL), fixed),
                  pl.BlockSpec(memory_space=pl.ANY)],
        out_specs=pl.BlockSpec((tm, D_MODEL), row),
        out_shape=jax.ShapeDtypeStruct((T, D_MODEL), F32),
        scratch_shapes=[pltpu.VMEM((2, TOP_K, tm, HALF), U32), pltpu.SemaphoreType.DMA((2,))],
        compiler_params=_params(("arbitrary",)),
        name="moe_combine",
    )(dest3, dest3, w_tok, h, wgs, wus, wds, ln_g.reshape(1, D_MODEL), ln_b.reshape(1, D_MODEL), ys3)


def _tile(n, pref):
    t = min(n, pref)
    assert n % t == 0, (n, pref)
    return t


def _layer(x, w_in, conv_w, conv_b, lru_wa, lru_ba, lru_wi, lru_bi, lru_lambda, ret_gn_gain,
           w_lru_out, w_ret_out, b_gate, w_o, ln1_g, ln1_b, w_router, router_bias,
           w_gate_e, w_up_e, w_down_e, w_gate_s, w_up_s, w_down_s, ln2_g, ln2_b):
    B, S, D = x.shape
    T = B * S
    x2 = x.reshape(T, D)
    xb = x2.astype(BF16)

    half = RET_DK // 2
    freq = ROPE_THETA ** (-jnp.arange(half, dtype=F32) / half)
    ang = jnp.arange(S, dtype=I32).astype(F32)[:, None] * freq
    cos, sin = jnp.cos(ang), jnp.sin(ang)

    tm_big = _tile(S, 1024)
    qk = _proj_qk(xb, w_in, cos, sin, S, tm_big, 1024)
    v = _proj_plain(xb, w_in, OFF_V, D_RV, BF16, tm_big, 1024, "v_proj")
    g = _proj_plain(xb, w_in, OFF_G, D_RV, F32, tm_big, 1024, "g_proj")

    ua = _lru_branch(xb, w_in, conv_w, conv_b, lru_wa.astype(BF16), lru_ba, lru_wi.astype(BF16),
                     lru_bi, lru_lambda, S, _tile(S, 256), 512)
    ub = _retention(qk, v, g, ret_gn_gain, B, S)

    mixed = _mix(ua, ub, xb, w_lru_out.astype(BF16), w_ret_out.astype(BF16),
                 w_in[:, OFF_GA:].astype(BF16), b_gate, _tile(T, 512), 512)

    tm_r = _tile(T, 256)
    h, hp3, eidx, rank, wsel, cnt = _ln1_route(mixed, w_o.astype(BF16), x2, ln1_g, ln1_b,
                                               w_router.T.astype(BF16), router_bias, tm_r)

    counts = cnt[:, 0].astype(I32)
    padded = (counts + MOE_ROWS - 1) // MOE_ROWS * MOE_ROWS
    pad_end = jnp.cumsum(padded)
    pad_start = pad_end - padded
    nb = (T * TOP_K + N_EXPERTS * (MOE_ROWS - 1) + MOE_ROWS - 1) // MOE_ROWS
    n_rows = nb * MOE_ROWS
    block_row0 = jnp.arange(nb, dtype=I32) * MOE_ROWS
    block_e = jnp.minimum(jnp.sum((pad_end[None, :] <= block_row0[:, None]).astype(I32), axis=1),
                          N_EXPERTS - 1)
    nb_used = (pad_end[-1:] // MOE_ROWS).astype(I32)

    dest = _dest_rows(pad_start, eidx, rank)
    xs3 = _dispatch(dest, pad_start + counts, pad_end, hp3, n_rows)
    ys3 = _experts(block_e, nb_used, xs3, w_gate_e, w_up_e, w_down_e)
    out = _combine(dest, wsel.T, h, w_gate_s.astype(BF16), w_up_s.astype(BF16),
                   w_down_s.astype(BF16), ln2_g, ln2_b, ys3)
    return out.reshape(B, S, D)


def kernel(x, w_in, conv_w, conv_b, lru_wa, lru_ba, lru_wi, lru_bi, lru_lambda, ret_gn_gain, w_lru_out, w_ret_out, b_gate, w_o, ln1_g, ln1_b, w_router, router_bias, w_gate_e, w_up_e, w_down_e, w_gate_s, w_up_s, w_down_s, ln2_g, ln2_b):
    assert DEPTH == 1 and w_in.shape[0] == DEPTH
    args = (w_in, conv_w, conv_b, lru_wa, lru_ba, lru_wi, lru_bi, lru_lambda, ret_gn_gain,
            w_lru_out, w_ret_out, b_gate, w_o, ln1_g, ln1_b, w_router, router_bias,
            w_gate_e, w_up_e, w_down_e, w_gate_s, w_up_s, w_down_s, ln2_g, ln2_b)
    return _layer(x, *[a[0] for a in args])
```

```python
import functools

import jax
import jax.numpy as jnp
from jax import lax
from jax.experimental import pallas as pl
from jax.experimental.pallas import tpu as pltpu

F32 = jnp.float32
BF16 = jnp.bfloat16
I32 = jnp.int32
U32 = jnp.uint32

D_MODEL = 2048
D_LRU = 2048
LRU_HEADS = 16
LRU_BLOCK = D_LRU // LRU_HEADS
CONV_WIDTH = 4
LRU_C = 8.0
RET_HEADS = 8
RET_DK = 256
RET_DV = 512
D_QK = RET_HEADS * RET_DK
D_RV = RET_HEADS * RET_DV
ROPE_THETA = 10000.0
N_EXPERTS = 64
TOP_K = 8
N_GROUPS = 8
GROUP_SIZE = N_EXPERTS // N_GROUPS
TOPK_GROUPS = 4
D_EXPERT = 512
D_SHARED = 512
ROUTED_SCALE = 2.5
DEPTH = 1
DN_ALPHA = (2.0 * DEPTH) ** 0.25
LN_EPS = 1e-5

OFF_LRU_X = 0
OFF_LRU_Y = OFF_LRU_X + D_LRU
OFF_Q = OFF_LRU_Y + D_LRU
OFF_K = OFF_Q + D_QK
OFF_V = OFF_K + D_QK
OFF_G = OFF_V + D_RV
OFF_GA = OFF_G + D_RV
OFF_GB = OFF_GA + D_MODEL

V7X_VMEM_LIMIT = 56 * 1024 * 1024
LANES = 128
SUBLANES = 8
MOE_ROWS = 512
RET_CHUNK = 256
HALF = D_MODEL // 2


def _params(sem):
    return pltpu.CompilerParams(dimension_semantics=sem, vmem_limit_bytes=V7X_VMEM_LIMIT)


def _dot(a, b):
    return jnp.dot(a, b, preferred_element_type=F32)


def _cast_weight(w_ref, wb_ref):
    @pl.when(pl.program_id(1) == 0)
    def _():
        wb_ref[...] = w_ref[...].astype(BF16)


def _mm_kernel(x_ref, w_ref, o_ref, wb_ref):
    _cast_weight(w_ref, wb_ref)
    o_ref[...] = _dot(x_ref[...], wb_ref[...]).astype(o_ref.dtype)


def _proj_plain(xb, w, col0, ncols, out_dtype, tm, tn, name):
    T, K = xb.shape
    jb0 = col0 // tn
    return pl.pallas_call(
        _mm_kernel,
        grid=(ncols // tn, T // tm),
        in_specs=[pl.BlockSpec((tm, K), lambda j, i: (i, 0)),
                  pl.BlockSpec((K, tn), lambda j, i: (0, jb0 + j))],
        out_specs=pl.BlockSpec((tm, tn), lambda j, i: (i, j)),
        out_shape=jax.ShapeDtypeStruct((T, ncols), out_dtype),
        scratch_shapes=[pltpu.VMEM((K, tn), BF16)],
        compiler_params=_params(("parallel", "arbitrary")),
        name=name,
    )(xb, w)


def _qk_kernel(x_ref, w_ref, cos_ref, sin_ref, o_ref, wb_ref, *, heads_per_tile, k_tile0):
    _cast_weight(w_ref, wb_ref)
    j = pl.program_id(0)
    acc = _dot(x_ref[...], wb_ref[...])
    scale = jnp.where(j >= k_tile0, RET_DK ** -0.5, 1.0).astype(F32)
    cos = cos_ref[...]
    sin = sin_ref[...]
    half = RET_DK // 2
    for hh in range(heads_per_tile):
        c0 = hh * RET_DK
        t1 = acc[:, c0:c0 + half]
        t2 = acc[:, c0 + half:c0 + RET_DK]
        o_ref[:, c0:c0 + half] = ((t1 * cos - t2 * sin) * scale).astype(o_ref.dtype)
        o_ref[:, c0 + half:c0 + RET_DK] = ((t1 * sin + t2 * cos) * scale).astype(o_ref.dtype)


def _proj_qk(xb, w, cos, sin, seq, tm, tn):
    T, K = xb.shape
    ncols = 2 * D_QK
    jb0 = OFF_Q // tn
    n_seq_tiles = seq // tm
    kern = functools.partial(_qk_kernel, heads_per_tile=tn // RET_DK, k_tile0=D_QK // tn)
    return pl.pallas_call(
        kern,
        grid=(ncols // tn, T // tm),
        in_specs=[pl.BlockSpec((tm, K), lambda j, i: (i, 0)),
                  pl.BlockSpec((K, tn), lambda j, i: (0, jb0 + j)),
                  pl.BlockSpec((tm, RET_DK // 2), lambda j, i: (i % n_seq_tiles, 0)),
                  pl.BlockSpec((tm, RET_DK // 2), lambda j, i: (i % n_seq_tiles, 0))],
        out_specs=pl.BlockSpec((tm, tn), lambda j, i: (i, j)),
        out_shape=jax.ShapeDtypeStruct((T, ncols), BF16),
        scratch_shapes=[pltpu.VMEM((K, tn), BF16)],
        compiler_params=_params(("parallel", "arbitrary")),
        name="qk_rope",
    )(xb, w, cos, sin)


def _lru_kernel(x_ref, wx_ref, wy_ref, cw_ref, cb_ref, wa_ref, ba_ref, wi_ref, bi_ref, lam_ref,
                o_ref, wxb_ref, wyb_ref, perm_ref, unperm_ref, tail_ref, hc_ref, a_ref, b_ref, gy_ref,
                *, tiles_per_seq, tm, width):
    i = pl.program_id(1)
    first = (i % tiles_per_seq) == 0
    seg = tm // SUBLANES
    halo = (CONV_WIDTH - 1) * SUBLANES

    @pl.when(i == 0)
    def _():
        wxb_ref[...] = wx_ref[...].astype(BF16)
        wyb_ref[...] = wy_ref[...].astype(BF16)
        r = lax.broadcasted_iota(I32, (tm, tm), 0)
        c = lax.broadcasted_iota(I32, (tm, tm), 1)
        perm_ref[...] = jnp.where(c == (r & (SUBLANES - 1)) * seg + (r >> 3), 1.0, 0.0).astype(BF16)
        unperm_ref[...] = jnp.where(r == (c & (SUBLANES - 1)) * seg + (c >> 3), 1.0, 0.0).astype(BF16)

    @pl.when(first)
    def _():
        tail_ref[...] = jnp.zeros_like(tail_ref)
        hc_ref[...] = jnp.zeros_like(hc_ref)

    xp = _dot(perm_ref[...], x_ref[...]).astype(BF16)
    px = _dot(xp, wxb_ref[...])
    gy_ref[...] = jax.nn.gelu(_dot(xp, wyb_ref[...]), approximate=True)

    cw = cw_ref[...]
    cb = cb_ref[...]
    taps = [cw[CONV_WIDTH - 1 - d:CONV_WIDTH - d] for d in range(CONV_WIDTH)]
    xa = cb + taps[0] * px
    for d in range(1, CONV_WIDTH):
        back = jnp.concatenate([px[tm - d * SUBLANES:], px[:tm - d * SUBLANES]], axis=0)
        xa = xa + taps[d] * back
    sub = lax.broadcasted_iota(I32, (SUBLANES, width), 0)
    groups = {}
    for g in range(CONV_WIDTH - 1):
        rows = slice(g * SUBLANES, (g + 1) * SUBLANES)
        cur_end = px[tm - halo + g * SUBLANES:tm - halo + (g + 1) * SUBLANES]
        groups[g - (CONV_WIDTH - 1)] = jnp.where(sub == 0, pltpu.roll(tail_ref[rows, :], 1, 0),
                                                 pltpu.roll(cur_end, 1, 0))
        groups[g] = px[rows, :]
    head = []
    for q in range(CONV_WIDTH - 1):
        acc = cb + taps[0] * groups[q]
        for d in range(1, CONV_WIDTH):
            acc = acc + taps[d] * groups[q - d]
        head.append(acc)
    xa = jnp.concatenate(head + [xa[halo:]], axis=0)
    tail_ref[...] = px[tm - halo:]

    xab = xa.astype(BF16)
    r_parts, i_parts = [], []
    for hh in range(width // LRU_BLOCK):
        xh = xab[:, hh * LRU_BLOCK:(hh + 1) * LRU_BLOCK]
        r_parts.append(_dot(xh, wa_ref[hh]))
        i_parts.append(_dot(xh, wi_ref[hh]))
    rg = jax.nn.sigmoid(jnp.concatenate(r_parts, axis=1) + ba_ref[...])
    ig = jax.nn.sigmoid(jnp.concatenate(i_parts, axis=1) + bi_ref[...])

    log_a = -LRU_C * rg * jax.nn.softplus(-lam_ref[...])
    a = jnp.exp(log_a)
    mult = jnp.sqrt(-jnp.tanh(log_a) * (a * a + 1.0))
    row = lax.broadcasted_iota(I32, (tm, width), 0)
    mult = jnp.where(jnp.logical_and(first, row == 0), 1.0, mult)
    a_ref[...] = a
    b_ref[...] = mult * ig * xa

    run_a = jnp.ones((SUBLANES, width), F32)
    run_h = jnp.zeros((SUBLANES, width), F32)
    for q in range(seg):
        rows = slice(q * SUBLANES, (q + 1) * SUBLANES)
        aq = a_ref[rows, :]
        run_h = aq * run_h + b_ref[rows, :]
        run_a = aq * run_a
        b_ref[rows, :] = run_h
        a_ref[rows, :] = run_a

    carry = hc_ref[...]
    carries = []
    for s in range(SUBLANES):
        carries.append(carry)
        carry = run_a[s:s + 1, :] * carry + run_h[s:s + 1, :]
    hc_ref[...] = carry
    carry_in = jnp.concatenate(carries, axis=0)
    for q in range(seg):
        rows = slice(q * SUBLANES, (q + 1) * SUBLANES)
        b_ref[rows, :] = (b_ref[rows, :] + a_ref[rows, :] * carry_in) * gy_ref[rows, :]

    o_ref[...] = _dot(unperm_ref[...], b_ref[...].astype(BF16)).astype(o_ref.dtype)


def _lru_branch(xb, w, conv_w, conv_b, wa, ba, wi, bi, lam, seq, tm, width):
    T, K = xb.shape
    nj = D_LRU // width
    jy0 = OFF_LRU_Y // width
    hp = width // LRU_BLOCK
    kern = functools.partial(_lru_kernel, tiles_per_seq=seq // tm, tm=tm, width=width)
    vec = lambda j, i: (0, j)
    return pl.pallas_call(
        kern,
        grid=(nj, T // tm),
        in_specs=[pl.BlockSpec((tm, K), lambda j, i: (i, 0)),
                  pl.BlockSpec((K, width), lambda j, i: (0, j)),
                  pl.BlockSpec((K, width), lambda j, i: (0, jy0 + j)),
                  pl.BlockSpec((CONV_WIDTH, width), vec),
                  pl.BlockSpec((1, width), vec),
                  pl.BlockSpec((hp, LRU_BLOCK, LRU_BLOCK), lambda j, i: (j, 0, 0)),
                  pl.BlockSpec((1, width), vec),
                  pl.BlockSpec((hp, LRU_BLOCK, LRU_BLOCK), lambda j, i: (j, 0, 0)),
                  pl.BlockSpec((1, width), vec),
                  pl.BlockSpec((1, width), vec)],
        out_specs=pl.BlockSpec((tm, width), lambda j, i: (i, j)),
        out_shape=jax.ShapeDtypeStruct((T, D_LRU), BF16),
        scratch_shapes=[pltpu.VMEM((K, width), BF16), pltpu.VMEM((K, width), BF16),
                        pltpu.VMEM((tm, tm), BF16), pltpu.VMEM((tm, tm), BF16),
                        pltpu.VMEM(((CONV_WIDTH - 1) * SUBLANES, width), F32),
                        pltpu.VMEM((1, width), F32),
                        pltpu.VMEM((tm, width), F32), pltpu.VMEM((tm, width), F32),
                        pltpu.VMEM((tm, width), F32)],
        compiler_params=_params(("parallel", "arbitrary")),
        name="lru_branch",
    )(xb, w, w, conv_w, conv_b.reshape(1, D_LRU), wa, ba.reshape(1, D_LRU), wi, bi.reshape(1, D_LRU),
      lam.reshape(1, D_LRU))


def _ret_kernel(gc_ref, q_ref, k_ref, v_ref, g_ref, gain_ref, dm_ref, xi_ref, zeta_ref, o_ref, st_ref):
    @pl.when(pl.program_id(1) == 0)
    def _():
        st_ref[...] = jnp.zeros_like(st_ref)

    for hd in range(RET_HEADS):
        qc = slice(hd * RET_DK, (hd + 1) * RET_DK)
        vc = slice(hd * RET_DV, (hd + 1) * RET_DV)
        q = q_ref[:, qc]
        k = k_ref[:, qc]
        v = v_ref[:, vc]
        st = st_ref[hd]
        s = lax.dot_general(q, k, (((1,), (1,)), ((), ())), preferred_element_type=F32) * dm_ref[hd]
        o = _dot(s.astype(BF16), v) + _dot(q, st.astype(BF16)) * xi_ref[hd]
        kz = (k.astype(F32) * zeta_ref[hd]).astype(BF16)
        st_ref[hd] = gc_ref[hd] * st + lax.dot_general(kz, v, (((0,), (0,)), ((), ())),
                                                       preferred_element_type=F32)
        mu = jnp.mean(o, axis=-1, keepdims=True)
        oc = o - mu
        var = jnp.mean(oc * oc, axis=-1, keepdims=True)
        oh = oc * lax.rsqrt(var + LN_EPS) * gain_ref[:, vc]
        o_ref[:, vc] = (jax.nn.silu(g_ref[:, vc]) * oh).astype(o_ref.dtype)


def _retention(qk, v, g, gain, batch, seq):
    T = batch * seq
    C = min(RET_CHUNK, seq)
    nchunk = seq // C
    H = RET_HEADS
    log_g = jnp.log1p(-jnp.exp2(-5.0 - jnp.arange(H, dtype=F32)))
    idx = jnp.arange(C, dtype=F32)
    diff = idx[:, None] - idx[None, :]
    dmask = jnp.where(diff >= 0, jnp.exp(jnp.maximum(diff, 0.0)[None] * log_g[:, None, None]), 0.0)
    xi = jnp.exp((idx[None] + 1.0) * log_g[:, None])[:, :, None]
    zeta = jnp.exp((C - 1.0 - idx[None]) * log_g[:, None])[:, :, None]
    g_c = jnp.exp(C * log_g)
    rows = lambda b, n: b * nchunk + n
    whole3 = lambda b, n: (0, 0, 0)
    return pl.pallas_call(
        _ret_kernel,
        grid=(batch, nchunk),
        in_specs=[pl.BlockSpec(memory_space=pltpu.SMEM),
                  pl.BlockSpec((C, D_QK), lambda b, n: (rows(b, n), 0)),
                  pl.BlockSpec((C, D_QK), lambda b, n: (rows(b, n), 1)),
                  pl.BlockSpec((C, D_RV), lambda b, n: (rows(b, n), 0)),
                  pl.BlockSpec((C, D_RV), lambda b, n: (rows(b, n), 0)),
                  pl.BlockSpec((1, D_RV), lambda b, n: (0, 0)),
                  pl.BlockSpec((H, C, C), whole3),
                  pl.BlockSpec((H, C, 1), whole3),
                  pl.BlockSpec((H, C, 1), whole3)],
        out_specs=pl.BlockSpec((C, D_RV), lambda b, n: (rows(b, n), 0)),
        out_shape=jax.ShapeDtypeStruct((T, D_RV), BF16),
        scratch_shapes=[pltpu.VMEM((H, RET_DK, RET_DV), F32)],
        compiler_params=_params(("parallel", "arbitrary")),
        name="retention",
    )(g_c, qk, qk, v, g, gain.reshape(1, D_RV), dmask, xi, zeta)


def _mix_kernel(ua_ref, ub_ref, x_ref, wlo_ref, wro_ref, wga_ref, wgb_ref, bga_ref, bgb_ref, o_ref):
    x = x_ref[...]
    ya = _dot(ua_ref[...], wlo_ref[...])
    yb = _dot(ub_ref[...], wro_ref[...])
    ga = jax.nn.sigmoid(_dot(x, wga_ref[...]) + bga_ref[...])
    gb = jax.nn.sigmoid(_dot(x, wgb_ref[...]) + bgb_ref[...])
    o_ref[...] = (ga * ya + gb * yb).astype(o_ref.dtype)


def _mix(ua, ub, xb, w_lru_out, w_ret_out, w_gates, b_gate, tm, tn):
    T = xb.shape[0]
    nj = D_MODEL // tn
    return pl.pallas_call(
        _mix_kernel,
        grid=(T // tm, nj),
        in_specs=[pl.BlockSpec((tm, D_LRU), lambda i, j: (i, 0)),
                  pl.BlockSpec((tm, D_RV), lambda i, j: (i, 0)),
                  pl.BlockSpec((tm, D_MODEL), lambda i, j: (i, 0)),
                  pl.BlockSpec((D_LRU, tn), lambda i, j: (0, j)),
                  pl.BlockSpec((D_RV, tn), lambda i, j: (0, j)),
                  pl.BlockSpec((D_MODEL, tn), lambda i, j: (0, j)),
                  pl.BlockSpec((D_MODEL, tn), lambda i, j: (0, nj + j)),
                  pl.BlockSpec((1, tn), lambda i, j: (0, j)),
                  pl.BlockSpec((1, tn), lambda i, j: (0, nj + j))],
        out_specs=pl.BlockSpec((tm, tn), lambda i, j: (i, j)),
        out_shape=jax.ShapeDtypeStruct((T, D_MODEL), BF16),
        compiler_params=_params(("parallel", "parallel")),
        name="gated_mix",
    )(ua, ub, xb, w_lru_out, w_ret_out, w_gates, w_gates, b_gate.reshape(1, 2 * D_MODEL),
      b_gate.reshape(1, 2 * D_MODEL))


def _layer_norm(y, g, b):
    mu = jnp.mean(y, axis=-1, keepdims=True)
    yc = y - mu
    var = jnp.mean(yc * yc, axis=-1, keepdims=True)
    return yc * lax.rsqrt(var + LN_EPS) * g + b


def _pack_bf16_pairs(hb):
    lo = lax.bitcast_convert_type(hb[:, :HALF].astype(F32), U32)
    hi = lax.bitcast_convert_type(hb[:, HALF:].astype(F32), U32)
    return (hi & jnp.uint32(0xFFFF0000)) | (lo >> 16)


def _unpack_bf16_pairs(p):
    lo = lax.bitcast_convert_type(p << 16, F32).astype(BF16)
    hi = lax.bitcast_convert_type(p & jnp.uint32(0xFFFF0000), F32).astype(BF16)
    return lo, hi


def _max2(a):
    return jnp.max(jnp.max(a, axis=1, keepdims=True), axis=0, keepdims=True)


def _min2(a):
    return jnp.min(jnp.min(a, axis=1, keepdims=True), axis=0, keepdims=True)


def _sum2(a):
    return jnp.sum(jnp.sum(a, axis=1, keepdims=True), axis=0, keepdims=True)


def _ln1_route_kernel(mix_ref, wo_ref, x_ref, g_ref, b_ref, wr_ref, rb_ref,
                      h_ref, dest_ref, wsel_ref, cnt_ref, xs_ref,
                      cnt_sc, base_sc, hp_sc, dest_v, dest_s, row_sem, idx_sem, *, tm, n_tiles, cap):
    i = pl.program_id(0)
    slot = i % 2
    prev = 1 - slot

    def wait_rows(s):
        for _ in range(TOP_K):
            pltpu.make_async_copy(hp_sc.at[s], hp_sc.at[s], row_sem).wait()

    @pl.when(i == 0)
    def _():
        cnt_sc[...] = jnp.zeros_like(cnt_sc)
        base_sc[...] = jnp.zeros_like(base_sc)
        hp_sc[1] = jnp.zeros((tm, HALF), U32)

        def spare(t, c):
            for k in range(TOP_K):
                dest_s[1, k, t] = N_EXPERTS * cap + k
            return c

        lax.fori_loop(0, tm, spare, 0)

    @pl.when(i > 0)
    def _():
        wait_rows(slot)

    for t in range(tm):
        for k in range(TOP_K):
            _row_copy(hp_sc.at[prev], t, xs_ref, dest_s[prev, k, t], row_sem).start()

    y = DN_ALPHA * x_ref[...] + _dot(mix_ref[...], wo_ref[...])
    h = _layer_norm(y, g_ref[...], b_ref[...])
    h_ref[...] = h
    hb = h.astype(BF16)
    hp_sc[slot] = _pack_bf16_pairs(hb)

    G, GS = N_GROUPS, GROUP_SIZE
    logits = lax.dot_general(wr_ref[...], hb, (((1,), (1,)), ((), ())), preferred_element_type=F32)
    scores = jax.nn.sigmoid(logits)
    s3 = scores.reshape(G, GS, tm)
    b3 = (scores + rb_ref[...]).reshape(G, GS, tm)
    neg = jnp.float32(-jnp.inf)

    in_grp = lax.broadcasted_iota(I32, (G, GS, tm), 1)
    m1 = jnp.max(b3, axis=1, keepdims=True)
    f1 = jnp.min(jnp.where(b3 == m1, in_grp, GS), axis=1, keepdims=True)
    m2 = jnp.max(jnp.where(in_grp == f1, neg, b3), axis=1, keepdims=True)
    grp = m1 + m2

    gi = lax.broadcasted_iota(I32, (G, 1, tm), 0)
    gkeep = jnp.zeros((G, 1, tm), F32)
    cur = grp
    for _ in range(TOPK_GROUPS):
        m = jnp.max(cur, axis=0, keepdims=True)
        f = jnp.min(jnp.where(cur == m, gi, G), axis=0, keepdims=True)
        hit = gi == f
        gkeep = jnp.where(hit, 1.0, gkeep)
        cur = jnp.where(hit, neg, cur)
    emask = jnp.broadcast_to(gkeep, (G, GS, tm)) > 0.0

    ei = lax.broadcasted_iota(I32, (G, GS, tm), 0) * GS + in_grp
    cur = jnp.where(emask, b3, neg)
    hits, idxs, ws = [], [], []
    for _ in range(TOP_K):
        m = _max2(cur)
        f = _min2(jnp.where(cur == m, ei, N_EXPERTS))
        hit = ei == f
        cur = jnp.where(hit, neg, cur)
        hits.append(hit)
        idxs.append(f)
        ws.append(_sum2(jnp.where(hit, s3, 0.0)))
    wsum = ws[0]
    for r in range(1, TOP_K):
        wsum = wsum + ws[r]

    sel = jnp.zeros((G, GS, tm), F32)
    for hit in hits:
        sel = jnp.where(hit, 1.0, sel)
    sel2 = sel.reshape(N_EXPERTS, tm)
    tri = (lax.broadcasted_iota(I32, (tm, tm), 0) < lax.broadcasted_iota(I32, (tm, tm), 1))
    rank_local = _dot(sel2.astype(BF16), jnp.where(tri, 1.0, 0.0).astype(BF16))
    base = jnp.where(i < n_tiles, cnt_sc[...], base_sc[...])
    base_sc[...] = base
    rank3 = (rank_local + base[:, 0:1]).reshape(G, GS, tm)
    cnt_new = base + jnp.sum(sel2, axis=1, keepdims=True)
    cnt_sc[...] = cnt_new
    cnt_ref[...] = cnt_new

    for r in range(TOP_K):
        rank_r = _sum2(jnp.where(hits[r], rank3, 0.0)).reshape(1, tm).astype(I32)
        dest_v[r:r + 1, :] = idxs[r].reshape(1, tm) * cap + rank_r
        wsel_ref[r:r + 1, :] = (ws[r] / wsum * ROUTED_SCALE).reshape(1, tm)
    dest_ref[0] = dest_v[...]

    to_smem = pltpu.make_async_copy(dest_v, dest_s.at[slot], idx_sem)
    to_smem.start()
    to_smem.wait()

    @pl.when(i == n_tiles)
    def _():
        wait_rows(prev)


def _ln1_route(mixed, w_o, x2, ln_g, ln_b, w_router_t, router_bias, tm, cap):
    T = x2.shape[0]
    n_tiles = T // tm
    kern = functools.partial(_ln1_route_kernel, tm=tm, n_tiles=n_tiles, cap=cap)
    tile = lambda i: jnp.minimum(i, n_tiles - 1)
    row = lambda i: (tile(i), 0)
    fixed = lambda i: (0, 0)
    return pl.pallas_call(
        kern,
        grid=(n_tiles + 1,),
        in_specs=[pl.BlockSpec((tm, D_MODEL), row),
                  pl.BlockSpec((D_MODEL, D_MODEL), fixed),
                  pl.BlockSpec((tm, D_MODEL), row),
                  pl.BlockSpec((1, D_MODEL), fixed),
                  pl.BlockSpec((1, D_MODEL), fixed),
                  pl.BlockSpec((N_EXPERTS, D_MODEL), fixed),
                  pl.BlockSpec((N_EXPERTS, 1), fixed)],
        out_specs=[pl.BlockSpec((tm, D_MODEL), row),
                   pl.BlockSpec((1, TOP_K, tm), lambda i: (tile(i), 0, 0)),
                   pl.BlockSpec((TOP_K, tm), lambda i: (0, tile(i))),
                   pl.BlockSpec((N_EXPERTS, LANES), fixed),
                   pl.BlockSpec(memory_space=pl.ANY)],
        out_shape=[jax.ShapeDtypeStruct((T, D_MODEL), F32),
                   jax.ShapeDtypeStruct((n_tiles, TOP_K, tm), I32),
                   jax.ShapeDtypeStruct((TOP_K, T), F32),
                   jax.ShapeDtypeStruct((N_EXPERTS, LANES), F32),
                   jax.ShapeDtypeStruct((N_EXPERTS * cap + SUBLANES, HALF), U32)],
        scratch_shapes=[pltpu.VMEM((N_EXPERTS, LANES), F32), pltpu.VMEM((N_EXPERTS, LANES), F32),
                        pltpu.VMEM((2, tm, HALF), U32),
                        pltpu.VMEM((TOP_K, tm), I32),
                        pltpu.SMEM((2, TOP_K, tm), I32),
                        pltpu.SemaphoreType.DMA, pltpu.SemaphoreType.DMA],
        compiler_params=_params(("arbitrary",)),
        name="ln1_route",
    )(mixed, w_o, x2, ln_g.reshape(1, D_MODEL), ln_b.reshape(1, D_MODEL), w_router_t,
      router_bias.reshape(N_EXPERTS, 1))


def _row_copy(src_ref, src_row, dst_ref, dst_row, sem):
    return pltpu.make_async_copy(src_ref.at[pl.ds(src_row, 1)], dst_ref.at[pl.ds(dst_row, 1)], sem)


def _pad_fill_kernel(lo_ref, hi_ref, xs_in_ref, xs_ref, zero_ref, sem):
    del xs_in_ref
    zero_ref[...] = jnp.zeros_like(zero_ref)

    def group_copy(r):
        return pltpu.make_async_copy(zero_ref, xs_ref.at[pl.ds(pl.multiple_of(r, SUBLANES), SUBLANES)], sem)

    def fill_expert(e, counts):
        n_rows, n_groups = counts
        lo = lo_ref[e]
        hi = hi_ref[e]
        mid = jnp.minimum((lo + SUBLANES - 1) // SUBLANES * SUBLANES, hi)
        groups = (hi - mid) // SUBLANES

        def fill_row(r, c):
            _row_copy(zero_ref, 0, xs_ref, r, sem).start()
            return c

        def fill_group(g, c):
            group_copy(mid + g * SUBLANES).start()
            return c

        lax.fori_loop(lo, mid, fill_row, 0)
        lax.fori_loop(0, groups, fill_group, 0)
        return n_rows + (mid - lo), n_groups + groups

    n_rows, n_groups = lax.fori_loop(0, N_EXPERTS, fill_expert, (0, 0))

    def drain_row(r, c):
        _row_copy(zero_ref, 0, xs_ref, 0, sem).wait()
        return c

    def drain_group(g, c):
        group_copy(0).wait()
        return c

    lax.fori_loop(0, n_rows, drain_row, 0)
    lax.fori_loop(0, n_groups, drain_group, 0)


def _pad_fill(fill_lo, fill_hi, xs):
    smem = pl.BlockSpec(memory_space=pltpu.SMEM)
    return pl.pallas_call(
        _pad_fill_kernel,
        in_specs=[smem, smem, pl.BlockSpec(memory_space=pl.ANY)],
        out_specs=pl.BlockSpec(memory_space=pl.ANY),
        out_shape=jax.ShapeDtypeStruct(xs.shape, xs.dtype),
        scratch_shapes=[pltpu.VMEM((SUBLANES, HALF), U32), pltpu.SemaphoreType.DMA],
        input_output_aliases={2: 0},
        name="moe_pad_fill",
    )(fill_lo, fill_hi, xs)


def _experts_kernel(be_ref, nb_ref, brow_ref, xs_ref, wg_ref, wu_ref, wd_ref, ys_ref,
                    wgb_ref, wub_ref, wdb_ref):
    del brow_ref
    b = pl.program_id(0)
    new_expert = jnp.logical_or(b == 0, be_ref[b] != be_ref[jnp.maximum(b - 1, 0)])

    @pl.when(jnp.logical_and(b < nb_ref[0], new_expert))
    def _():
        wgb_ref[...] = wg_ref[0].astype(BF16)
        wub_ref[...] = wu_ref[0].astype(BF16)
        wdb_ref[...] = wd_ref[0].astype(BF16)

    @pl.when(b < nb_ref[0])
    def _():
        lo, hi = _unpack_bf16_pairs(xs_ref[...])
        gate = _dot(lo, wgb_ref[:HALF, :]) + _dot(hi, wgb_ref[HALF:, :])
        up = _dot(lo, wub_ref[:HALF, :]) + _dot(hi, wub_ref[HALF:, :])
        hb = (jax.nn.silu(gate) * up).astype(BF16)
        ys_ref[...] = _pack_bf16_pairs(_dot(hb, wdb_ref[...]).astype(BF16))


def _experts(block_e, nb_used, block_row, xs, wg, wu, wd):
    n_rows = xs.shape[0] // MOE_ROWS * MOE_ROWS
    nb = block_e.shape[0]
    rows = lambda b, be, nbu, brow: (brow[b], 0)
    wsel = lambda b, be, nbu, brow: (be[b], 0, 0)
    grid_spec = pltpu.PrefetchScalarGridSpec(
        num_scalar_prefetch=3,
        grid=(nb,),
        in_specs=[pl.BlockSpec((MOE_ROWS, HALF), rows),
                  pl.BlockSpec((1, D_MODEL, D_EXPERT), wsel),
                  pl.BlockSpec((1, D_MODEL, D_EXPERT), wsel),
                  pl.BlockSpec((1, D_EXPERT, D_MODEL), wsel)],
        out_specs=pl.BlockSpec((MOE_ROWS, HALF), rows),
        scratch_shapes=[pltpu.VMEM((D_MODEL, D_EXPERT), BF16), pltpu.VMEM((D_MODEL, D_EXPERT), BF16),
                        pltpu.VMEM((D_EXPERT, D_MODEL), BF16)],
    )
    return pl.pallas_call(
        _experts_kernel,
        grid_spec=grid_spec,
        out_shape=jax.ShapeDtypeStruct((n_rows, HALF), U32),
        compiler_params=_params(("arbitrary",)),
        name="moe_experts",
    )(block_e, nb_used, block_row, xs, wg, wu, wd)


def _combine_kernel(dest_ref, next_ref, w_ref, h_ref, wgs_ref, wus_ref, wds_ref, g_ref, b_ref,
                    ys_ref, o_ref, buf_ref, sem, *, tm):
    i = pl.program_id(0)
    slot = i % 2

    def gather(idx_ref, dst_slot, t):
        for k in range(TOP_K):
            _row_copy(ys_ref, idx_ref[0, k, t], buf_ref.at[dst_slot, k], t, sem.at[dst_slot]).start()

    def wait_tile(s):
        pltpu.make_async_copy(buf_ref.at[s], buf_ref.at[s], sem.at[s]).wait()

    @pl.when(i == 0)
    def _():
        def first(t, c):
            gather(dest_ref, 0, t)
            return c

        lax.fori_loop(0, tm, first, 0)

    wait_tile(slot)
    for t in range(tm):
        gather(next_ref, 1 - slot, t)

    h = h_ref[...]
    hb = h.astype(BF16)
    hid = (jax.nn.silu(_dot(hb, wgs_ref[...])) * _dot(hb, wus_ref[...])).astype(BF16)
    shared = _dot(hid, wds_ref[...])

    w = w_ref[...]
    lo_sum = jnp.zeros((tm, HALF), F32)
    hi_sum = jnp.zeros((tm, HALF), F32)
    for k in range(TOP_K):
        p = buf_ref[slot, k]
        lo_sum = lo_sum + lax.bitcast_convert_type(p << 16, F32) * w[:, k:k + 1]
        hi_sum = hi_sum + lax.bitcast_convert_type(p & jnp.uint32(0xFFFF0000), F32) * w[:, k:k + 1]
    routed = jnp.concatenate([lo_sum, hi_sum], axis=1)
    o_ref[...] = _layer_norm(DN_ALPHA * h + (routed + shared), g_ref[...], b_ref[...])

    @pl.when(i == pl.num_programs(0) - 1)
    def _():
        wait_tile(1 - slot)


def _combine(dest3, w_tok, h, wgs, wus, wds, ln_g, ln_b, ys3):
    T = h.shape[0]
    n_tiles, _, tm = dest3.shape
    kern = functools.partial(_combine_kernel, tm=tm)
    row = lambda i: (i, 0)
    fixed = lambda i: (0, 0)
    return pl.pallas_call(
        kern,
        grid=(n_tiles,),
        in_specs=[pl.BlockSpec((1, TOP_K, tm), lambda i: (i, 0, 0), memory_space=pltpu.SMEM),
                  pl.BlockSpec((1, TOP_K, tm), lambda i: (jnp.minimum(i + 1, n_tiles - 1), 0, 0),
                               memory_space=pltpu.SMEM),
                  pl.BlockSpec((tm, TOP_K), row),
                  pl.BlockSpec((tm, D_MODEL), row),
                  pl.BlockSpec((D_MODEL, D_SHARED), fixed),
                  pl.BlockSpec((D_MODEL, D_SHARED), fixed),
                  pl.BlockSpec((D_SHARED, D_MODEL), fixed),
                  pl.BlockSpec((1, D_MODEL), fixed),
                  pl.BlockSpec((1, D_MODEL), fixed),
                  pl.BlockSpec(memory_space=pl.ANY)],
        out_specs=pl.BlockSpec((tm, D_MODEL), row),
        out_shape=jax.ShapeDtypeStruct((T, D_MODEL), F32),
        scratch_shapes=[pltpu.VMEM((2, TOP_K, tm, HALF), U32), pltpu.SemaphoreType.DMA((2,))],
        compiler_params=_params(("arbitrary",)),
        name="moe_combine",
    )(dest3, dest3, w_tok, h, wgs, wus, wds, ln_g.reshape(1, D_MODEL), ln_b.reshape(1, D_MODEL), ys3)


def _tile(n, pref):
    t = min(n, pref)
    assert n % t == 0, (n, pref)
    return t


def _layer(x, w_in, conv_w, conv_b, lru_wa, lru_ba, lru_wi, lru_bi, lru_lambda, ret_gn_gain,
           w_lru_out, w_ret_out, b_gate, w_o, ln1_g, ln1_b, w_router, router_bias,
           w_gate_e, w_up_e, w_down_e, w_gate_s, w_up_s, w_down_s, ln2_g, ln2_b):
    B, S, D = x.shape
    T = B * S
    x2 = x.reshape(T, D)
    xb = x2.astype(BF16)

    half = RET_DK // 2
    freq = ROPE_THETA ** (-jnp.arange(half, dtype=F32) / half)
    ang = jnp.arange(S, dtype=I32).astype(F32)[:, None] * freq
    cos, sin = jnp.cos(ang), jnp.sin(ang)

    tm_big = _tile(S, 1024)
    qk = _proj_qk(xb, w_in, cos, sin, S, tm_big, 1024)
    v = _proj_plain(xb, w_in, OFF_V, D_RV, BF16, tm_big, 1024, "v_proj")
    g = _proj_plain(xb, w_in, OFF_G, D_RV, F32, tm_big, 1024, "g_proj")

    ua = _lru_branch(xb, w_in, conv_w, conv_b, lru_wa.astype(BF16), lru_ba, lru_wi.astype(BF16),
                     lru_bi, lru_lambda, S, _tile(S, 256), 512)
    ub = _retention(qk, v, g, ret_gn_gain, B, S)

    mixed = _mix(ua, ub, xb, w_lru_out.astype(BF16), w_ret_out.astype(BF16),
                 w_in[:, OFF_GA:].astype(BF16), b_gate, _tile(T, 512), 512)

    tm_r = _tile(T, 256)
    cap = (T + MOE_ROWS - 1) // MOE_ROWS * MOE_ROWS
    h, dest, wsel, cnt, xs = _ln1_route(mixed, w_o.astype(BF16), x2, ln1_g, ln1_b,
                                        w_router.T.astype(BF16), router_bias, tm_r, cap)

    counts = cnt[:, 0].astype(I32)
    e_blocks = (counts + MOE_ROWS - 1) // MOE_ROWS
    blocks_end = jnp.cumsum(e_blocks)
    nb = (T * TOP_K + N_EXPERTS * (MOE_ROWS - 1) + MOE_ROWS - 1) // MOE_ROWS
    nb_used = blocks_end[-1:]
    blk = jnp.minimum(jnp.arange(nb, dtype=I32), nb_used - 1)
    block_e = jnp.minimum(jnp.sum((blocks_end[None, :] <= blk[:, None]).astype(I32), axis=1),
                          N_EXPERTS - 1)
    blocks_start = blocks_end - e_blocks
    e_onehot = block_e[:, None] == jnp.arange(N_EXPERTS, dtype=I32)[None, :]
    block_row = block_e * (cap // MOE_ROWS) + blk - jnp.sum(jnp.where(e_onehot, blocks_start[None, :], 0),
                                                            axis=1)
    e_row0 = jnp.arange(N_EXPERTS, dtype=I32) * cap

    xs = _pad_fill(e_row0 + counts, e_row0 + e_blocks * MOE_ROWS, xs)
    ys = _experts(block_e, nb_used, block_row, xs, w_gate_e, w_up_e, w_down_e)
    out = _combine(dest, wsel.T, h, w_gate_s.astype(BF16), w_up_s.astype(BF16),
                   w_down_s.astype(BF16), ln2_g, ln2_b, ys)
    return out.reshape(B, S, D)


def kernel(x, w_in, conv_w, conv_b, lru_wa, lru_ba, lru_wi, lru_bi, lru_lambda, ret_gn_gain, w_lru_out, w_ret_out, b_gate, w_o, ln1_g, ln1_b, w_router, router_bias, w_gate_e, w_up_e, w_down_e, w_gate_s, w_up_s, w_down_s, ln2_g, ln2_b):
    assert DEPTH == 1 and w_in.shape[0] == DEPTH
    args = (w_in, conv_w, conv_b, lru_wa, lru_ba, lru_wi, lru_bi, lru_lambda, ret_gn_gain,
            w_lru_out, w_ret_out, b_gate, w_o, ln1_g, ln1_b, w_router, router_bias,
            w_gate_e, w_up_e, w_down_e, w_gate_s, w_up_s, w_down_s, ln2_g, ln2_b)
    return _layer(x, *[a[0] for a in args])
```

```python
import functools

import jax
import jax.numpy as jnp
from jax import lax
from jax.experimental import pallas as pl
from jax.experimental.pallas import tpu as pltpu

F32 = jnp.float32
BF16 = jnp.bfloat16
I32 = jnp.int32
U32 = jnp.uint32

D_MODEL = 2048
D_LRU = 2048
LRU_HEADS = 16
LRU_BLOCK = D_LRU // LRU_HEADS
CONV_WIDTH = 4
LRU_C = 8.0
RET_HEADS = 8
RET_DK = 256
RET_DV = 512
D_QK = RET_HEADS * RET_DK
D_RV = RET_HEADS * RET_DV
ROPE_THETA = 10000.0
N_EXPERTS = 64
TOP_K = 8
N_GROUPS = 8
GROUP_SIZE = N_EXPERTS // N_GROUPS
TOPK_GROUPS = 4
D_EXPERT = 512
D_SHARED = 512
ROUTED_SCALE = 2.5
DEPTH = 1
DN_ALPHA = (2.0 * DEPTH) ** 0.25
LN_EPS = 1e-5

OFF_LRU_X = 0
OFF_LRU_Y = OFF_LRU_X + D_LRU
OFF_Q = OFF_LRU_Y + D_LRU
OFF_K = OFF_Q + D_QK
OFF_V = OFF_K + D_QK
OFF_G = OFF_V + D_RV
OFF_GA = OFF_G + D_RV
OFF_GB = OFF_GA + D_MODEL

V7X_VMEM_LIMIT = 56 * 1024 * 1024
LANES = 128
SUBLANES = 8
MOE_ROWS = 512
RET_CHUNK = 256
HALF = D_MODEL // 2


def _params(sem):
    return pltpu.CompilerParams(dimension_semantics=sem, vmem_limit_bytes=V7X_VMEM_LIMIT)


def _dot(a, b):
    return jnp.dot(a, b, preferred_element_type=F32)


def _cast_weight(w_ref, wb_ref):
    @pl.when(pl.program_id(1) == 0)
    def _():
        wb_ref[...] = w_ref[...].astype(BF16)


def _qk_kernel(x_ref, w_ref, cos_ref, sin_ref, o_ref, wb_ref, *, heads_per_tile, k_tile0):
    _cast_weight(w_ref, wb_ref)
    j = pl.program_id(0)
    acc = _dot(x_ref[...], wb_ref[...])
    scale = jnp.where(j >= k_tile0, RET_DK ** -0.5, 1.0).astype(F32)
    cos = cos_ref[...]
    sin = sin_ref[...]
    half = RET_DK // 2
    for hh in range(heads_per_tile):
        c0 = hh * RET_DK
        t1 = acc[:, c0:c0 + half]
        t2 = acc[:, c0 + half:c0 + RET_DK]
        o_ref[:, c0:c0 + half] = ((t1 * cos - t2 * sin) * scale).astype(o_ref.dtype)
        o_ref[:, c0 + half:c0 + RET_DK] = ((t1 * sin + t2 * cos) * scale).astype(o_ref.dtype)


def _proj_qk(xb, w, cos, sin, seq, tm, tn):
    T, K = xb.shape
    ncols = 2 * D_QK
    jb0 = OFF_Q // tn
    n_seq_tiles = seq // tm
    kern = functools.partial(_qk_kernel, heads_per_tile=tn // RET_DK, k_tile0=D_QK // tn)
    return pl.pallas_call(
        kern,
        grid=(ncols // tn, T // tm),
        in_specs=[pl.BlockSpec((tm, K), lambda j, i: (i, 0)),
                  pl.BlockSpec((K, tn), lambda j, i: (0, jb0 + j)),
                  pl.BlockSpec((tm, RET_DK // 2), lambda j, i: (i % n_seq_tiles, 0)),
                  pl.BlockSpec((tm, RET_DK // 2), lambda j, i: (i % n_seq_tiles, 0))],
        out_specs=pl.BlockSpec((tm, tn), lambda j, i: (i, j)),
        out_shape=jax.ShapeDtypeStruct((T, ncols), BF16),
        scratch_shapes=[pltpu.VMEM((K, tn), BF16)],
        compiler_params=_params(("parallel", "arbitrary")),
        name="qk_rope",
    )(xb, w, cos, sin)


def _lru_kernel(x_ref, wx_ref, wy_ref, wv_ref, cw_ref, cb_ref, wa_ref, ba_ref, wi_ref, bi_ref, lam_ref,
                o_ref, v_ref, wxb_ref, wyb_ref, wvb_ref, perm_ref, unperm_ref, tail_ref, hc_ref,
                a_ref, b_ref, gy_ref, *, tiles_per_seq, tm, width):
    i = pl.program_id(1)
    first = (i % tiles_per_seq) == 0
    seg = tm // SUBLANES
    halo = (CONV_WIDTH - 1) * SUBLANES

    @pl.when(i == 0)
    def _():
        wxb_ref[...] = wx_ref[...].astype(BF16)
        wyb_ref[...] = wy_ref[...].astype(BF16)
        wvb_ref[...] = wv_ref[...].astype(BF16)
        r = lax.broadcasted_iota(I32, (tm, tm), 0)
        c = lax.broadcasted_iota(I32, (tm, tm), 1)
        perm_ref[...] = jnp.where(c == (r & (SUBLANES - 1)) * seg + (r >> 3), 1.0, 0.0).astype(BF16)
        unperm_ref[...] = jnp.where(r == (c & (SUBLANES - 1)) * seg + (c >> 3), 1.0, 0.0).astype(BF16)

    @pl.when(first)
    def _():
        tail_ref[...] = jnp.zeros_like(tail_ref)
        hc_ref[...] = jnp.zeros_like(hc_ref)

    v_piece = v_ref.shape[1] // 4

    def emit_v(piece):
        cols = slice(piece * v_piece, (piece + 1) * v_piece)
        v_ref[:, cols] = _dot(x_ref[...], wvb_ref[:, cols]).astype(v_ref.dtype)

    emit_v(0)

    xp = _dot(perm_ref[...], x_ref[...]).astype(BF16)
    px = _dot(xp, wxb_ref[...])
    gy_ref[...] = jax.nn.gelu(_dot(xp, wyb_ref[...]), approximate=True)

    cw = cw_ref[...]
    cb = cb_ref[...]
    taps = [cw[CONV_WIDTH - 1 - d:CONV_WIDTH - d] for d in range(CONV_WIDTH)]
    xa = cb + taps[0] * px
    for d in range(1, CONV_WIDTH):
        back = jnp.concatenate([px[tm - d * SUBLANES:], px[:tm - d * SUBLANES]], axis=0)
        xa = xa + taps[d] * back
    sub = lax.broadcasted_iota(I32, (SUBLANES, width), 0)
    groups = {}
    for g in range(CONV_WIDTH - 1):
        rows = slice(g * SUBLANES, (g + 1) * SUBLANES)
        cur_end = px[tm - halo + g * SUBLANES:tm - halo + (g + 1) * SUBLANES]
        groups[g - (CONV_WIDTH - 1)] = jnp.where(sub == 0, pltpu.roll(tail_ref[rows, :], 1, 0),
                                                 pltpu.roll(cur_end, 1, 0))
        groups[g] = px[rows, :]
    head = []
    for q in range(CONV_WIDTH - 1):
        acc = cb + taps[0] * groups[q]
        for d in range(1, CONV_WIDTH):
            acc = acc + taps[d] * groups[q - d]
        head.append(acc)
    xa = jnp.concatenate(head + [xa[halo:]], axis=0)
    tail_ref[...] = px[tm - halo:]

    emit_v(1)

    xab = xa.astype(BF16)
    r_parts, i_parts = [], []
    for hh in range(width // LRU_BLOCK):
        xh = xab[:, hh * LRU_BLOCK:(hh + 1) * LRU_BLOCK]
        r_parts.append(_dot(xh, wa_ref[hh]))
        i_parts.append(_dot(xh, wi_ref[hh]))
    rg = jax.nn.sigmoid(jnp.concatenate(r_parts, axis=1) + ba_ref[...])
    ig = jax.nn.sigmoid(jnp.concatenate(i_parts, axis=1) + bi_ref[...])

    log_a = -LRU_C * rg * jax.nn.softplus(-lam_ref[...])
    a = jnp.exp(log_a)
    mult = jnp.sqrt(-jnp.tanh(log_a) * (a * a + 1.0))
    row = lax.broadcasted_iota(I32, (tm, width), 0)
    mult = jnp.where(jnp.logical_and(first, row == 0), 1.0, mult)
    a_ref[...] = a
    b_ref[...] = mult * ig * xa
    emit_v(2)

    run_a = jnp.ones((SUBLANES, width), F32)
    run_h = jnp.zeros((SUBLANES, width), F32)
    for q in range(seg):
        rows = slice(q * SUBLANES, (q + 1) * SUBLANES)
        aq = a_ref[rows, :]
        run_h = aq * run_h + b_ref[rows, :]
        run_a = aq * run_a
        b_ref[rows, :] = run_h
        a_ref[rows, :] = run_a

    emit_v(3)

    carry = hc_ref[...]
    carries = []
    for s in range(SUBLANES):
        carries.append(carry)
        carry = run_a[s:s + 1, :] * carry + run_h[s:s + 1, :]
    hc_ref[...] = carry
    carry_in = jnp.concatenate(carries, axis=0)
    for q in range(seg):
        rows = slice(q * SUBLANES, (q + 1) * SUBLANES)
        b_ref[rows, :] = (b_ref[rows, :] + a_ref[rows, :] * carry_in) * gy_ref[rows, :]

    o_ref[...] = _dot(unperm_ref[...], b_ref[...].astype(BF16)).astype(o_ref.dtype)


def _lru_branch(xb, w, conv_w, conv_b, wa, ba, wi, bi, lam, seq, tm, width):
    T, K = xb.shape
    nj = D_LRU // width
    jy0 = OFF_LRU_Y // width
    hp = width // LRU_BLOCK
    vw = D_RV // nj
    jv0 = OFF_V // vw
    kern = functools.partial(_lru_kernel, tiles_per_seq=seq // tm, tm=tm, width=width)
    vec = lambda j, i: (0, j)
    return pl.pallas_call(
        kern,
        grid=(nj, T // tm),
        in_specs=[pl.BlockSpec((tm, K), lambda j, i: (i, 0)),
                  pl.BlockSpec((K, width), lambda j, i: (0, j)),
                  pl.BlockSpec((K, width), lambda j, i: (0, jy0 + j)),
                  pl.BlockSpec((K, vw), lambda j, i: (0, jv0 + j)),
                  pl.BlockSpec((CONV_WIDTH, width), vec),
                  pl.BlockSpec((1, width), vec),
                  pl.BlockSpec((hp, LRU_BLOCK, LRU_BLOCK), lambda j, i: (j, 0, 0)),
                  pl.BlockSpec((1, width), vec),
                  pl.BlockSpec((hp, LRU_BLOCK, LRU_BLOCK), lambda j, i: (j, 0, 0)),
                  pl.BlockSpec((1, width), vec),
                  pl.BlockSpec((1, width), vec)],
        out_specs=[pl.BlockSpec((tm, width), lambda j, i: (i, j)),
                   pl.BlockSpec((tm, vw), lambda j, i: (i, j))],
        out_shape=[jax.ShapeDtypeStruct((T, D_LRU), BF16), jax.ShapeDtypeStruct((T, D_RV), BF16)],
        scratch_shapes=[pltpu.VMEM((K, width), BF16), pltpu.VMEM((K, width), BF16),
                        pltpu.VMEM((K, vw), BF16),
                        pltpu.VMEM((tm, tm), BF16), pltpu.VMEM((tm, tm), BF16),
                        pltpu.VMEM(((CONV_WIDTH - 1) * SUBLANES, width), F32),
                        pltpu.VMEM((1, width), F32),
                        pltpu.VMEM((tm, width), F32), pltpu.VMEM((tm, width), F32),
                        pltpu.VMEM((tm, width), F32)],
        compiler_params=_params(("parallel", "arbitrary")),
        name="lru_branch",
    )(xb, w, w, w, conv_w, conv_b.reshape(1, D_LRU), wa, ba.reshape(1, D_LRU), wi,
      bi.reshape(1, D_LRU), lam.reshape(1, D_LRU))


def _ret_kernel(gc_ref, q_ref, k_ref, v_ref, x_ref, wg_hbm, gain_ref, dm_ref, xi_ref, zeta_ref, o_ref,
                st_ref, wg_ref, wg_sem):
    @pl.when(jnp.logical_and(pl.program_id(0) == 0, pl.program_id(1) == 0))
    def _():
        fetch = pltpu.make_async_copy(wg_hbm, wg_ref, wg_sem)
        fetch.start()
        fetch.wait()

    @pl.when(pl.program_id(1) == 0)
    def _():
        st_ref[...] = jnp.zeros_like(st_ref)

    x = x_ref[...]
    for hd in range(RET_HEADS):
        qc = slice(hd * RET_DK, (hd + 1) * RET_DK)
        vc = slice(hd * RET_DV, (hd + 1) * RET_DV)
        q = q_ref[:, qc]
        k = k_ref[:, qc]
        v = v_ref[:, vc]
        st = st_ref[hd]
        s = lax.dot_general(q, k, (((1,), (1,)), ((), ())), preferred_element_type=F32) * dm_ref[hd]
        o = _dot(s.astype(BF16), v) + _dot(q, st.astype(BF16)) * xi_ref[hd]
        kz = (k.astype(F32) * zeta_ref[hd]).astype(BF16)
        st_ref[hd] = gc_ref[hd] * st + lax.dot_general(kz, v, (((0,), (0,)), ((), ())),
                                                       preferred_element_type=F32)
        mu = jnp.mean(o, axis=-1, keepdims=True)
        oc = o - mu
        var = jnp.mean(oc * oc, axis=-1, keepdims=True)
        oh = oc * lax.rsqrt(var + LN_EPS) * gain_ref[:, vc]
        g = _dot(x, wg_ref[:, vc])
        o_ref[:, vc] = (jax.nn.silu(g) * oh).astype(o_ref.dtype)


def _retention(qk, v, xb, w_g, gain, batch, seq):
    T = batch * seq
    C = min(RET_CHUNK, seq)
    nchunk = seq // C
    H = RET_HEADS
    log_g = jnp.log1p(-jnp.exp2(-5.0 - jnp.arange(H, dtype=F32)))
    idx = jnp.arange(C, dtype=F32)
    diff = idx[:, None] - idx[None, :]
    dmask = jnp.where(diff >= 0, jnp.exp(jnp.maximum(diff, 0.0)[None] * log_g[:, None, None]), 0.0)
    xi = jnp.exp((idx[None] + 1.0) * log_g[:, None])[:, :, None]
    zeta = jnp.exp((C - 1.0 - idx[None]) * log_g[:, None])[:, :, None]
    g_c = jnp.exp(C * log_g)
    rows = lambda b, n: b * nchunk + n
    whole3 = lambda b, n: (0, 0, 0)
    return pl.pallas_call(
        _ret_kernel,
        grid=(batch, nchunk),
        in_specs=[pl.BlockSpec(memory_space=pltpu.SMEM),
                  pl.BlockSpec((C, D_QK), lambda b, n: (rows(b, n), 0)),
                  pl.BlockSpec((C, D_QK), lambda b, n: (rows(b, n), 1)),
                  pl.BlockSpec((C, D_RV), lambda b, n: (rows(b, n), 0)),
                  pl.BlockSpec((C, D_MODEL), lambda b, n: (rows(b, n), 0)),
                  pl.BlockSpec(memory_space=pl.ANY),
                  pl.BlockSpec((1, D_RV), lambda b, n: (0, 0)),
                  pl.BlockSpec((H, C, C), whole3),
                  pl.BlockSpec((H, C, 1), whole3),
                  pl.BlockSpec((H, C, 1), whole3)],
        out_specs=pl.BlockSpec((C, D_RV), lambda b, n: (rows(b, n), 0)),
        out_shape=jax.ShapeDtypeStruct((T, D_RV), BF16),
        scratch_shapes=[pltpu.VMEM((H, RET_DK, RET_DV), F32), pltpu.VMEM((D_MODEL, D_RV), BF16),
                        pltpu.SemaphoreType.DMA],
        compiler_params=_params(("arbitrary", "arbitrary")),
        name="retention",
    )(g_c, qk, qk, v, xb, w_g, gain.reshape(1, D_RV), dmask, xi, zeta)


def _mix_kernel(ua_ref, ub_ref, x_ref, wlo_ref, wro_ref, wga_ref, wgb_ref, bga_ref, bgb_ref, o_ref):
    x = x_ref[...]
    ya = _dot(ua_ref[...], wlo_ref[...])
    yb = _dot(ub_ref[...], wro_ref[...])
    ga = jax.nn.sigmoid(_dot(x, wga_ref[...]) + bga_ref[...])
    gb = jax.nn.sigmoid(_dot(x, wgb_ref[...]) + bgb_ref[...])
    o_ref[...] = (ga * ya + gb * yb).astype(o_ref.dtype)


def _mix(ua, ub, xb, w_lru_out, w_ret_out, w_gates, b_gate, tm, tn):
    T = xb.shape[0]
    nj = D_MODEL // tn
    return pl.pallas_call(
        _mix_kernel,
        grid=(T // tm, nj),
        in_specs=[pl.BlockSpec((tm, D_LRU), lambda i, j: (i, 0)),
                  pl.BlockSpec((tm, D_RV), lambda i, j: (i, 0)),
                  pl.BlockSpec((tm, D_MODEL), lambda i, j: (i, 0)),
                  pl.BlockSpec((D_LRU, tn), lambda i, j: (0, j)),
                  pl.BlockSpec((D_RV, tn), lambda i, j: (0, j)),
                  pl.BlockSpec((D_MODEL, tn), lambda i, j: (0, j)),
                  pl.BlockSpec((D_MODEL, tn), lambda i, j: (0, nj + j)),
                  pl.BlockSpec((1, tn), lambda i, j: (0, j)),
                  pl.BlockSpec((1, tn), lambda i, j: (0, nj + j))],
        out_specs=pl.BlockSpec((tm, tn), lambda i, j: (i, j)),
        out_shape=jax.ShapeDtypeStruct((T, D_MODEL), BF16),
        compiler_params=_params(("parallel", "parallel")),
        name="gated_mix",
    )(ua, ub, xb, w_lru_out, w_ret_out, w_gates, w_gates, b_gate.reshape(1, 2 * D_MODEL),
      b_gate.reshape(1, 2 * D_MODEL))


def _layer_norm(y, g, b):
    mu = jnp.mean(y, axis=-1, keepdims=True)
    yc = y - mu
    var = jnp.mean(yc * yc, axis=-1, keepdims=True)
    return yc * lax.rsqrt(var + LN_EPS) * g + b


def _pack_bf16_pairs(hb):
    lo = lax.bitcast_convert_type(hb[:, :HALF].astype(F32), U32)
    hi = lax.bitcast_convert_type(hb[:, HALF:].astype(F32), U32)
    return (hi & jnp.uint32(0xFFFF0000)) | (lo >> 16)


def _unpack_bf16_pairs(p):
    lo = lax.bitcast_convert_type(p << 16, F32).astype(BF16)
    hi = lax.bitcast_convert_type(p & jnp.uint32(0xFFFF0000), F32).astype(BF16)
    return lo, hi


def _max2(a):
    return jnp.max(jnp.max(a, axis=1, keepdims=True), axis=0, keepdims=True)


def _min2(a):
    return jnp.min(jnp.min(a, axis=1, keepdims=True), axis=0, keepdims=True)


def _sum2(a):
    return jnp.sum(jnp.sum(a, axis=1, keepdims=True), axis=0, keepdims=True)


def _ln1_route_kernel(mix_ref, wo_ref, x_ref, g_ref, b_ref, wr_ref, rb_ref,
                      h_ref, dest_ref, wsel_ref, cnt_ref, xs_ref,
                      cnt_sc, base_sc, hp_sc, dest_v, dest_s, row_sem, idx_sem, *, tm, n_tiles, cap):
    i = pl.program_id(0)
    slot = i % 2
    prev = 1 - slot

    def wait_rows(s):
        for _ in range(TOP_K):
            pltpu.make_async_copy(hp_sc.at[s], hp_sc.at[s], row_sem).wait()

    @pl.when(i == 0)
    def _():
        cnt_sc[...] = jnp.zeros_like(cnt_sc)
        base_sc[...] = jnp.zeros_like(base_sc)
        hp_sc[1] = jnp.zeros((tm, HALF), U32)

        def spare(t, c):
            for k in range(TOP_K):
                dest_s[1, k, t] = N_EXPERTS * cap + k
            return c

        lax.fori_loop(0, tm, spare, 0)

    @pl.when(i > 0)
    def _():
        wait_rows(slot)

    for t in range(tm):
        for k in range(TOP_K):
            _row_copy(hp_sc.at[prev], t, xs_ref, dest_s[prev, k, t], row_sem).start()

    y = DN_ALPHA * x_ref[...] + _dot(mix_ref[...], wo_ref[...])
    h = _layer_norm(y, g_ref[...], b_ref[...])
    h_ref[...] = h
    hb = h.astype(BF16)
    hp_sc[slot] = _pack_bf16_pairs(hb)

    G, GS = N_GROUPS, GROUP_SIZE
    logits = lax.dot_general(wr_ref[...], hb, (((1,), (1,)), ((), ())), preferred_element_type=F32)
    scores = jax.nn.sigmoid(logits)
    s3 = scores.reshape(G, GS, tm)
    b3 = (scores + rb_ref[...]).reshape(G, GS, tm)
    neg = jnp.float32(-jnp.inf)

    in_grp = lax.broadcasted_iota(I32, (G, GS, tm), 1)
    m1 = jnp.max(b3, axis=1, keepdims=True)
    f1 = jnp.min(jnp.where(b3 == m1, in_grp, GS), axis=1, keepdims=True)
    m2 = jnp.max(jnp.where(in_grp == f1, neg, b3), axis=1, keepdims=True)
    grp = m1 + m2

    gi = lax.broadcasted_iota(I32, (G, 1, tm), 0)
    gkeep = jnp.zeros((G, 1, tm), F32)
    cur = grp
    for _ in range(TOPK_GROUPS):
        m = jnp.max(cur, axis=0, keepdims=True)
        f = jnp.min(jnp.where(cur == m, gi, G), axis=0, keepdims=True)
        hit = gi == f
        gkeep = jnp.where(hit, 1.0, gkeep)
        cur = jnp.where(hit, neg, cur)
    emask = jnp.broadcast_to(gkeep, (G, GS, tm)) > 0.0

    ei = lax.broadcasted_iota(I32, (G, GS, tm), 0) * GS + in_grp
    cur = jnp.where(emask, b3, neg)
    hits, idxs, ws = [], [], []
    for _ in range(TOP_K):
        m = _max2(cur)
        f = _min2(jnp.where(cur == m, ei, N_EXPERTS))
        hit = ei == f
        cur = jnp.where(hit, neg, cur)
        hits.append(hit)
        idxs.append(f)
        ws.append(_sum2(jnp.where(hit, s3, 0.0)))
    wsum = ws[0]
    for r in range(1, TOP_K):
        wsum = wsum + ws[r]

    sel = jnp.zeros((G, GS, tm), F32)
    for hit in hits:
        sel = jnp.where(hit, 1.0, sel)
    sel2 = sel.reshape(N_EXPERTS, tm)
    tri = (lax.broadcasted_iota(I32, (tm, tm), 0) < lax.broadcasted_iota(I32, (tm, tm), 1))
    rank_local = _dot(sel2.astype(BF16), jnp.where(tri, 1.0, 0.0).astype(BF16))
    base = jnp.where(i < n_tiles, cnt_sc[...], base_sc[...])
    base_sc[...] = base
    rank3 = (rank_local + base[:, 0:1]).reshape(G, GS, tm)
    cnt_new = base + jnp.sum(sel2, axis=1, keepdims=True)
    cnt_sc[...] = cnt_new
    cnt_ref[...] = cnt_new

    for r in range(TOP_K):
        rank_r = _sum2(jnp.where(hits[r], rank3, 0.0)).reshape(1, tm).astype(I32)
        dest_v[r:r + 1, :] = idxs[r].reshape(1, tm) * cap + rank_r
        wsel_ref[r:r + 1, :] = (ws[r] / wsum * ROUTED_SCALE).reshape(1, tm)
    dest_ref[0] = dest_v[...]

    to_smem = pltpu.make_async_copy(dest_v, dest_s.at[slot], idx_sem)
    to_smem.start()
    to_smem.wait()

    @pl.when(i == n_tiles)
    def _():
        wait_rows(prev)


def _ln1_route(mixed, w_o, x2, ln_g, ln_b, w_router_t, router_bias, tm, cap):
    T = x2.shape[0]
    n_tiles = T // tm
    kern = functools.partial(_ln1_route_kernel, tm=tm, n_tiles=n_tiles, cap=cap)
    tile = lambda i: jnp.minimum(i, n_tiles - 1)
    row = lambda i: (tile(i), 0)
    fixed = lambda i: (0, 0)
    return pl.pallas_call(
        kern,
        grid=(n_tiles + 1,),
        in_specs=[pl.BlockSpec((tm, D_MODEL), row),
                  pl.BlockSpec((D_MODEL, D_MODEL), fixed),
                  pl.BlockSpec((tm, D_MODEL), row),
                  pl.BlockSpec((1, D_MODEL), fixed),
                  pl.BlockSpec((1, D_MODEL), fixed),
                  pl.BlockSpec((N_EXPERTS, D_MODEL), fixed),
                  pl.BlockSpec((N_EXPERTS, 1), fixed)],
        out_specs=[pl.BlockSpec((tm, D_MODEL), row),
                   pl.BlockSpec((1, TOP_K, tm), lambda i: (tile(i), 0, 0)),
                   pl.BlockSpec((TOP_K, tm), lambda i: (0, tile(i))),
                   pl.BlockSpec((N_EXPERTS, LANES), fixed),
                   pl.BlockSpec(memory_space=pl.ANY)],
        out_shape=[jax.ShapeDtypeStruct((T, D_MODEL), F32),
                   jax.ShapeDtypeStruct((n_tiles, TOP_K, tm), I32),
                   jax.ShapeDtypeStruct((TOP_K, T), F32),
                   jax.ShapeDtypeStruct((N_EXPERTS, LANES), F32),
                   jax.ShapeDtypeStruct((N_EXPERTS * cap + SUBLANES, HALF), U32)],
        scratch_shapes=[pltpu.VMEM((N_EXPERTS, LANES), F32), pltpu.VMEM((N_EXPERTS, LANES), F32),
                        pltpu.VMEM((2, tm, HALF), U32),
                        pltpu.VMEM((TOP_K, tm), I32),
                        pltpu.SMEM((2, TOP_K, tm), I32),
                        pltpu.SemaphoreType.DMA, pltpu.SemaphoreType.DMA],
        compiler_params=_params(("arbitrary",)),
        name="ln1_route",
    )(mixed, w_o, x2, ln_g.reshape(1, D_MODEL), ln_b.reshape(1, D_MODEL), w_router_t,
      router_bias.reshape(N_EXPERTS, 1))


def _row_copy(src_ref, src_row, dst_ref, dst_row, sem):
    return pltpu.make_async_copy(src_ref.at[pl.ds(src_row, 1)], dst_ref.at[pl.ds(dst_row, 1)], sem)


def _pad_fill_kernel(lo_ref, hi_ref, xs_in_ref, xs_ref, zero_ref, sem):
    del xs_in_ref
    zero_ref[...] = jnp.zeros_like(zero_ref)

    def group_copy(r):
        return pltpu.make_async_copy(zero_ref, xs_ref.at[pl.ds(pl.multiple_of(r, SUBLANES), SUBLANES)], sem)

    def fill_expert(e, counts):
        n_rows, n_groups = counts
        lo = lo_ref[e]
        hi = hi_ref[e]
        mid = jnp.minimum((lo + SUBLANES - 1) // SUBLANES * SUBLANES, hi)
        groups = (hi - mid) // SUBLANES

        def fill_row(r, c):
            _row_copy(zero_ref, 0, xs_ref, r, sem).start()
            return c

        def fill_group(g, c):
            group_copy(mid + g * SUBLANES).start()
            return c

        lax.fori_loop(lo, mid, fill_row, 0)
        lax.fori_loop(0, groups, fill_group, 0)
        return n_rows + (mid - lo), n_groups + groups

    n_rows, n_groups = lax.fori_loop(0, N_EXPERTS, fill_expert, (0, 0))

    def drain_row(r, c):
        _row_copy(zero_ref, 0, xs_ref, 0, sem).wait()
        return c

    def drain_group(g, c):
        group_copy(0).wait()
        return c

    lax.fori_loop(0, n_rows, drain_row, 0)
    lax.fori_loop(0, n_groups, drain_group, 0)


def _pad_fill(fill_lo, fill_hi, xs):
    smem = pl.BlockSpec(memory_space=pltpu.SMEM)
    return pl.pallas_call(
        _pad_fill_kernel,
        in_specs=[smem, smem, pl.BlockSpec(memory_space=pl.ANY)],
        out_specs=pl.BlockSpec(memory_space=pl.ANY),
        out_shape=jax.ShapeDtypeStruct(xs.shape, xs.dtype),
        scratch_shapes=[pltpu.VMEM((SUBLANES, HALF), U32), pltpu.SemaphoreType.DMA],
        input_output_aliases={2: 0},
        name="moe_pad_fill",
    )(fill_lo, fill_hi, xs)


def _experts_kernel(be_ref, nb_ref, brow_ref, xs_ref, wg_ref, wu_ref, wd_ref, ys_ref,
                    wgb_ref, wub_ref, wdb_ref):
    del brow_ref
    b = pl.program_id(0)
    new_expert = jnp.logical_or(b == 0, be_ref[b] != be_ref[jnp.maximum(b - 1, 0)])

    @pl.when(jnp.logical_and(b < nb_ref[0], new_expert))
    def _():
        wgb_ref[...] = wg_ref[0].astype(BF16)
        wub_ref[...] = wu_ref[0].astype(BF16)
        wdb_ref[...] = wd_ref[0].astype(BF16)

    @pl.when(b < nb_ref[0])
    def _():
        lo, hi = _unpack_bf16_pairs(xs_ref[...])
        gate = _dot(lo, wgb_ref[:HALF, :]) + _dot(hi, wgb_ref[HALF:, :])
        up = _dot(lo, wub_ref[:HALF, :]) + _dot(hi, wub_ref[HALF:, :])
        hb = (jax.nn.silu(gate) * up).astype(BF16)
        ys_ref[...] = _pack_bf16_pairs(_dot(hb, wdb_ref[...]).astype(BF16))


def _experts(block_e, nb_used, block_row, xs, wg, wu, wd):
    n_rows = xs.shape[0] // MOE_ROWS * MOE_ROWS
    nb = block_e.shape[0]
    rows = lambda b, be, nbu, brow: (brow[b], 0)
    wsel = lambda b, be, nbu, brow: (be[b], 0, 0)
    grid_spec = pltpu.PrefetchScalarGridSpec(
        num_scalar_prefetch=3,
        grid=(nb,),
        in_specs=[pl.BlockSpec((MOE_ROWS, HALF), rows),
                  pl.BlockSpec((1, D_MODEL, D_EXPERT), wsel),
                  pl.BlockSpec((1, D_MODEL, D_EXPERT), wsel),
                  pl.BlockSpec((1, D_EXPERT, D_MODEL), wsel)],
        out_specs=pl.BlockSpec((MOE_ROWS, HALF), rows),
        scratch_shapes=[pltpu.VMEM((D_MODEL, D_EXPERT), BF16), pltpu.VMEM((D_MODEL, D_EXPERT), BF16),
                        pltpu.VMEM((D_EXPERT, D_MODEL), BF16)],
    )
    return pl.pallas_call(
        _experts_kernel,
        grid_spec=grid_spec,
        out_shape=jax.ShapeDtypeStruct((n_rows, HALF), U32),
        compiler_params=_params(("arbitrary",)),
        name="moe_experts",
    )(block_e, nb_used, block_row, xs, wg, wu, wd)


def _combine_kernel(dest_ref, next_ref, w_ref, h_ref, wgs_ref, wus_ref, wds_ref, g_ref, b_ref,
                    ys_ref, o_ref, buf_ref, sem, *, tm):
    i = pl.program_id(0)
    slot = i % 2

    def gather(idx_ref, dst_slot, t):
        for k in range(TOP_K):
            _row_copy(ys_ref, idx_ref[0, k, t], buf_ref.at[dst_slot, k], t, sem.at[dst_slot]).start()

    def wait_tile(s):
        pltpu.make_async_copy(buf_ref.at[s], buf_ref.at[s], sem.at[s]).wait()

    @pl.when(i == 0)
    def _():
        def first(t, c):
            gather(dest_ref, 0, t)
            return c

        lax.fori_loop(0, tm, first, 0)

    wait_tile(slot)
    for t in range(tm):
        gather(next_ref, 1 - slot, t)

    h = h_ref[...]
    hb = h.astype(BF16)
    hid = (jax.nn.silu(_dot(hb, wgs_ref[...])) * _dot(hb, wus_ref[...])).astype(BF16)
    shared = _dot(hid, wds_ref[...])

    w = w_ref[...]
    lo_sum = jnp.zeros((tm, HALF), F32)
    hi_sum = jnp.zeros((tm, HALF), F32)
    for k in range(TOP_K):
        p = buf_ref[slot, k]
        lo_sum = lo_sum + lax.bitcast_convert_type(p << 16, F32) * w[:, k:k + 1]
        hi_sum = hi_sum + lax.bitcast_convert_type(p & jnp.uint32(0xFFFF0000), F32) * w[:, k:k + 1]
    routed = jnp.concatenate([lo_sum, hi_sum], axis=1)
    o_ref[...] = _layer_norm(DN_ALPHA * h + (routed + shared), g_ref[...], b_ref[...])

    @pl.when(i == pl.num_programs(0) - 1)
    def _():
        wait_tile(1 - slot)


def _combine(dest3, w_tok, h, wgs, wus, wds, ln_g, ln_b, ys3):
    T = h.shape[0]
    n_tiles, _, tm = dest3.shape
    kern = functools.partial(_combine_kernel, tm=tm)
    row = lambda i: (i, 0)
    fixed = lambda i: (0, 0)
    return pl.pallas_call(
        kern,
        grid=(n_tiles,),
        in_specs=[pl.BlockSpec((1, TOP_K, tm), lambda i: (i, 0, 0), memory_space=pltpu.SMEM),
                  pl.BlockSpec((1, TOP_K, tm), lambda i: (jnp.minimum(i + 1, n_tiles - 1), 0, 0),
                               memory_space=pltpu.SMEM),
                  pl.BlockSpec((tm, TOP_K), row),
                  pl.BlockSpec((tm, D_MODEL), row),
                  pl.BlockSpec((D_MODEL, D_SHARED), fixed),
                  pl.BlockSpec((D_MODEL, D_SHARED), fixed),
                  pl.BlockSpec((D_SHARED, D_MODEL), fixed),
                  pl.BlockSpec((1, D_MODEL), fixed),
                  pl.BlockSpec((1, D_MODEL), fixed),
                  pl.BlockSpec(memory_space=pl.ANY)],
        out_specs=pl.BlockSpec((tm, D_MODEL), row),
        out_shape=jax.ShapeDtypeStruct((T, D_MODEL), F32),
        scratch_shapes=[pltpu.VMEM((2, TOP_K, tm, HALF), U32), pltpu.SemaphoreType.DMA((2,))],
        compiler_params=_params(("arbitrary",)),
        name="moe_combine",
    )(dest3, dest3, w_tok, h, wgs, wus, wds, ln_g.reshape(1, D_MODEL), ln_b.reshape(1, D_MODEL), ys3)


def _tile(n, pref):
    t = min(n, pref)
    assert n % t == 0, (n, pref)
    return t


def _layer(x, w_in, conv_w, conv_b, lru_wa, lru_ba, lru_wi, lru_bi, lru_lambda, ret_gn_gain,
           w_lru_out, w_ret_out, b_gate, w_o, ln1_g, ln1_b, w_router, router_bias,
           w_gate_e, w_up_e, w_down_e, w_gate_s, w_up_s, w_down_s, ln2_g, ln2_b):
    B, S, D = x.shape
    T = B * S
    x2 = x.reshape(T, D)
    xb = x2.astype(BF16)

    half = RET_DK // 2
    freq = ROPE_THETA ** (-jnp.arange(half, dtype=F32) / half)
    ang = jnp.arange(S, dtype=I32).astype(F32)[:, None] * freq
    cos, sin = jnp.cos(ang), jnp.sin(ang)

    tm_big = _tile(S, 1024)
    qk = _proj_qk(xb, w_in, cos, sin, S, tm_big, 1024)
    ua, v = _lru_branch(xb, w_in, conv_w, conv_b, lru_wa.astype(BF16), lru_ba, lru_wi.astype(BF16),
                     lru_bi, lru_lambda, S, _tile(S, 256), 512)
    ub = _retention(qk, v, xb, w_in[:, OFF_G:OFF_GA].astype(BF16), ret_gn_gain, B, S)

    mixed = _mix(ua, ub, xb, w_lru_out.astype(BF16), w_ret_out.astype(BF16),
                 w_in[:, OFF_GA:].astype(BF16), b_gate, _tile(T, 512), 512)

    tm_r = _tile(T, 256)
    cap = (T + MOE_ROWS - 1) // MOE_ROWS * MOE_ROWS
    h, dest, wsel, cnt, xs = _ln1_route(mixed, w_o.astype(BF16), x2, ln1_g, ln1_b,
                                        w_router.T.astype(BF16), router_bias, tm_r, cap)

    counts = cnt[:, 0].astype(I32)
    e_blocks = (counts + MOE_ROWS - 1) // MOE_ROWS
    blocks_end = jnp.cumsum(e_blocks)
    nb = (T * TOP_K + N_EXPERTS * (MOE_ROWS - 1) + MOE_ROWS - 1) // MOE_ROWS
    nb_used = blocks_end[-1:]
    blk = jnp.minimum(jnp.arange(nb, dtype=I32), nb_used - 1)
    block_e = jnp.minimum(jnp.sum((blocks_end[None, :] <= blk[:, None]).astype(I32), axis=1),
                          N_EXPERTS - 1)
    blocks_start = blocks_end - e_blocks
    e_onehot = block_e[:, None] == jnp.arange(N_EXPERTS, dtype=I32)[None, :]
    block_row = block_e * (cap // MOE_ROWS) + blk - jnp.sum(jnp.where(e_onehot, blocks_start[None, :], 0),
                                                            axis=1)
    e_row0 = jnp.arange(N_EXPERTS, dtype=I32) * cap

    xs = _pad_fill(e_row0 + counts, e_row0 + e_blocks * MOE_ROWS, xs)
    ys = _experts(block_e, nb_used, block_row, xs, w_gate_e, w_up_e, w_down_e)
    out = _combine(dest, wsel.T, h, w_gate_s.astype(BF16), w_up_s.astype(BF16),
                   w_down_s.astype(BF16), ln2_g, ln2_b, ys)
    return out.reshape(B, S, D)


def kernel(x, w_in, conv_w, conv_b, lru_wa, lru_ba, lru_wi, lru_bi, lru_lambda, ret_gn_gain, w_lru_out, w_ret_out, b_gate, w_o, ln1_g, ln1_b, w_router, router_bias, w_gate_e, w_up_e, w_down_e, w_gate_s, w_up_s, w_down_s, ln2_g, ln2_b):
    assert DEPTH == 1 and w_in.shape[0] == DEPTH
    args = (w_in, conv_w, conv_b, lru_wa, lru_ba, lru_wi, lru_bi, lru_lambda, ret_gn_gain,
            w_lru_out, w_ret_out, b_gate, w_o, ln1_g, ln1_b, w_router, router_bias,
            w_gate_e, w_up_e, w_down_e, w_gate_s, w_up_s, w_down_s, ln2_g, ln2_b)
    return _layer(x, *[a[0] for a in args])
```

```python
import functools

import jax
import jax.numpy as jnp
from jax import lax
from jax.experimental import pallas as pl
from jax.experimental.pallas import tpu as pltpu

F32 = jnp.float32
BF16 = jnp.bfloat16
I32 = jnp.int32
U32 = jnp.uint32

D_MODEL = 2048
D_LRU = 2048
LRU_HEADS = 16
LRU_BLOCK = D_LRU // LRU_HEADS
CONV_WIDTH = 4
LRU_C = 8.0
RET_HEADS = 8
RET_DK = 256
RET_DV = 512
D_QK = RET_HEADS * RET_DK
D_RV = RET_HEADS * RET_DV
ROPE_THETA = 10000.0
N_EXPERTS = 64
TOP_K = 8
N_GROUPS = 8
GROUP_SIZE = N_EXPERTS // N_GROUPS
TOPK_GROUPS = 4
D_EXPERT = 512
D_SHARED = 512
ROUTED_SCALE = 2.5
DEPTH = 1
DN_ALPHA = (2.0 * DEPTH) ** 0.25
LN_EPS = 1e-5

OFF_LRU_X = 0
OFF_LRU_Y = OFF_LRU_X + D_LRU
OFF_Q = OFF_LRU_Y + D_LRU
OFF_K = OFF_Q + D_QK
OFF_V = OFF_K + D_QK
OFF_G = OFF_V + D_RV
OFF_GA = OFF_G + D_RV
OFF_GB = OFF_GA + D_MODEL

V7X_VMEM_LIMIT = 56 * 1024 * 1024
LANES = 128
SUBLANES = 8
MOE_ROWS = 512
RET_CHUNK = 256
LRU_SUB = 2
HALF = D_MODEL // 2


def _params(sem):
    return pltpu.CompilerParams(dimension_semantics=sem, vmem_limit_bytes=V7X_VMEM_LIMIT)


def _dot(a, b):
    return jnp.dot(a, b, preferred_element_type=F32)


def _lru_kernel(x_ref, wx_ref, wy_ref, wv_ref, wq_ref, wk_ref, cos_ref, sin_ref, cw_ref, cb_ref,
                wa_ref, ba_ref, wi_ref, bi_ref, lam_ref,
                o_ref, v_ref, q_ref, k_ref,
                wxb_ref, wyb_ref, wvb_ref, wqb_ref, wkb_ref, perm_ref, unperm_ref, tail_ref, hc_ref,
                a_ref, b_ref, gy_ref, *, tiles_per_seq, tm, width):
    i = pl.program_id(1)
    seg = tm // SUBLANES
    halo = (CONV_WIDTH - 1) * SUBLANES

    @pl.when(i == 0)
    def _():
        wxb_ref[...] = wx_ref[...].astype(BF16)
        wyb_ref[...] = wy_ref[...].astype(BF16)
        wvb_ref[...] = wv_ref[...].astype(BF16)
        wqb_ref[...] = wq_ref[...].astype(BF16)
        wkb_ref[...] = wk_ref[...].astype(BF16)
        r = lax.broadcasted_iota(I32, (tm, tm), 0)
        c = lax.broadcasted_iota(I32, (tm, tm), 1)
        perm_ref[...] = jnp.where(c == (r & (SUBLANES - 1)) * seg + (r >> 3), 1.0, 0.0).astype(BF16)
        unperm_ref[...] = jnp.where(r == (c & (SUBLANES - 1)) * seg + (c >> 3), 1.0, 0.0).astype(BF16)
        tail_ref[...] = jnp.zeros_like(tail_ref)
        hc_ref[...] = jnp.zeros_like(hc_ref)

    cw = cw_ref[...]
    cb = cb_ref[...]
    taps = [cw[CONV_WIDTH - 1 - d:CONV_WIDTH - d] for d in range(CONV_WIDTH)]
    sub = lax.broadcasted_iota(I32, (SUBLANES, width), 0)
    row = lax.broadcasted_iota(I32, (tm, width), 0)
    v_piece = v_ref.shape[1] // 4

    def one_tile(u, tail, carry):
        rows_u = slice(u * tm, (u + 1) * tm)
        first = ((i * LRU_SUB + u) % tiles_per_seq) == 0
        tail = jnp.where(first, 0.0, tail)
        carry = jnp.where(first, 0.0, carry)
        x = x_ref[rows_u, :]

        def emit_v(piece):
            cols = slice(piece * v_piece, (piece + 1) * v_piece)
            v_ref[rows_u, cols] = _dot(x, wvb_ref[:, cols]).astype(v_ref.dtype)

        def emit_rotary(wb_ref, out_ref, scale):
            acc = _dot(x, wb_ref[...])
            cos = cos_ref[rows_u, :]
            sin = sin_ref[rows_u, :]
            half = RET_DK // 2
            for hh in range(width // RET_DK):
                c0 = hh * RET_DK
                t1 = acc[:, c0:c0 + half]
                t2 = acc[:, c0 + half:c0 + RET_DK]
                out_ref[rows_u, c0:c0 + half] = ((t1 * cos - t2 * sin) * scale).astype(out_ref.dtype)
                out_ref[rows_u, c0 + half:c0 + RET_DK] = (
                    (t1 * sin + t2 * cos) * scale).astype(out_ref.dtype)

        emit_v(0)
        xp = _dot(perm_ref[...], x).astype(BF16)
        px = _dot(xp, wxb_ref[...])
        gy_ref[u] = jax.nn.gelu(_dot(xp, wyb_ref[...]), approximate=True)

        xa = cb + taps[0] * px
        for d in range(1, CONV_WIDTH):
            back = jnp.concatenate([px[tm - d * SUBLANES:], px[:tm - d * SUBLANES]], axis=0)
            xa = xa + taps[d] * back
        groups = {}
        for g in range(CONV_WIDTH - 1):
            rows = slice(g * SUBLANES, (g + 1) * SUBLANES)
            cur_end = px[tm - halo + g * SUBLANES:tm - halo + (g + 1) * SUBLANES]
            groups[g - (CONV_WIDTH - 1)] = jnp.where(sub == 0, pltpu.roll(tail[rows, :], 1, 0),
                                                     pltpu.roll(cur_end, 1, 0))
            groups[g] = px[rows, :]
        head = []
        for q in range(CONV_WIDTH - 1):
            acc = cb + taps[0] * groups[q]
            for d in range(1, CONV_WIDTH):
                acc = acc + taps[d] * groups[q - d]
            head.append(acc)
        xa = jnp.concatenate(head + [xa[halo:]], axis=0)
        tail_out = px[tm - halo:]
        emit_v(1)

        xab = xa.astype(BF16)
        r_parts, i_parts = [], []
        for hh in range(width // LRU_BLOCK):
            xh = xab[:, hh * LRU_BLOCK:(hh + 1) * LRU_BLOCK]
            r_parts.append(_dot(xh, wa_ref[hh]))
            i_parts.append(_dot(xh, wi_ref[hh]))
        rg = jax.nn.sigmoid(jnp.concatenate(r_parts, axis=1) + ba_ref[...])
        ig = jax.nn.sigmoid(jnp.concatenate(i_parts, axis=1) + bi_ref[...])
        emit_rotary(wqb_ref, q_ref, 1.0)

        log_a = -LRU_C * rg * jax.nn.softplus(-lam_ref[...])
        a = jnp.exp(log_a)
        mult = jnp.sqrt(-jnp.tanh(log_a) * (a * a + 1.0))
        mult = jnp.where(jnp.logical_and(first, row == 0), 1.0, mult)
        a_ref[u] = a
        b_ref[u] = mult * ig * xa
        emit_v(2)

        run_a = jnp.ones((SUBLANES, width), F32)
        run_h = jnp.zeros((SUBLANES, width), F32)
        for q in range(seg):
            rows = slice(q * SUBLANES, (q + 1) * SUBLANES)
            aq = a_ref[u, rows, :]
            run_h = aq * run_h + b_ref[u, rows, :]
            run_a = aq * run_a
            b_ref[u, rows, :] = run_h
            a_ref[u, rows, :] = run_a
        emit_v(3)

        carries = []
        for s in range(SUBLANES):
            carries.append(carry)
            carry = run_a[s:s + 1, :] * carry + run_h[s:s + 1, :]
        carry_in = jnp.concatenate(carries, axis=0)
        for q in range(seg):
            rows = slice(q * SUBLANES, (q + 1) * SUBLANES)
            b_ref[u, rows, :] = (b_ref[u, rows, :] + a_ref[u, rows, :] * carry_in) * gy_ref[u, rows, :]

        emit_rotary(wkb_ref, k_ref, RET_DK ** -0.5)
        o_ref[rows_u, :] = _dot(unperm_ref[...], b_ref[u].astype(BF16)).astype(o_ref.dtype)
        return tail_out, carry

    tail = tail_ref[...]
    carry = hc_ref[...]
    for u in range(LRU_SUB):
        tail, carry = one_tile(u, tail, carry)
    tail_ref[...] = tail
    hc_ref[...] = carry


def _lru_branch(xb, w, cos, sin, conv_w, conv_b, wa, ba, wi, bi, lam, seq, tm, width):
    T, K = xb.shape
    nj = D_LRU // width
    assert D_QK == D_LRU, "q / k column tiles are walked together with the lru tiles"
    jy0 = OFF_LRU_Y // width
    jq0 = OFF_Q // width
    jk0 = OFF_K // width
    hp = width // LRU_BLOCK
    vw = D_RV // nj
    jv0 = OFF_V // vw
    tiles_per_seq = seq // tm
    rows = LRU_SUB * tm
    reps = max(1, rows // seq)
    cos, sin = jnp.tile(cos, (reps, 1)), jnp.tile(sin, (reps, 1))
    table_blocks = cos.shape[0] // rows
    kern = functools.partial(_lru_kernel, tiles_per_seq=tiles_per_seq, tm=tm, width=width)
    vec = lambda j, i: (0, j)
    once = pl.Buffered(1)
    table = pl.BlockSpec((rows, RET_DK // 2), lambda j, i: (i % table_blocks, 0))
    return pl.pallas_call(
        kern,
        grid=(nj, T // rows),
        in_specs=[pl.BlockSpec((rows, K), lambda j, i: (i, 0)),
                  pl.BlockSpec((K, width), lambda j, i: (0, j), pipeline_mode=once),
                  pl.BlockSpec((K, width), lambda j, i: (0, jy0 + j), pipeline_mode=once),
                  pl.BlockSpec((K, vw), lambda j, i: (0, jv0 + j), pipeline_mode=once),
                  pl.BlockSpec((K, width), lambda j, i: (0, jq0 + j), pipeline_mode=once),
                  pl.BlockSpec((K, width), lambda j, i: (0, jk0 + j), pipeline_mode=once),
                  table, table,
                  pl.BlockSpec((CONV_WIDTH, width), vec),
                  pl.BlockSpec((1, width), vec),
                  pl.BlockSpec((hp, LRU_BLOCK, LRU_BLOCK), lambda j, i: (j, 0, 0)),
                  pl.BlockSpec((1, width), vec),
                  pl.BlockSpec((hp, LRU_BLOCK, LRU_BLOCK), lambda j, i: (j, 0, 0)),
                  pl.BlockSpec((1, width), vec),
                  pl.BlockSpec((1, width), vec)],
        out_specs=[pl.BlockSpec((rows, width), lambda j, i: (i, j)),
                   pl.BlockSpec((rows, vw), lambda j, i: (i, j)),
                   pl.BlockSpec((rows, width), lambda j, i: (i, j)),
                   pl.BlockSpec((rows, width), lambda j, i: (i, j))],
        out_shape=[jax.ShapeDtypeStruct((T, D_LRU), BF16), jax.ShapeDtypeStruct((T, D_RV), BF16),
                   jax.ShapeDtypeStruct((T, D_QK), BF16), jax.ShapeDtypeStruct((T, D_QK), BF16)],
        scratch_shapes=[pltpu.VMEM((K, width), BF16), pltpu.VMEM((K, width), BF16),
                        pltpu.VMEM((K, vw), BF16),
                        pltpu.VMEM((K, width), BF16), pltpu.VMEM((K, width), BF16),
                        pltpu.VMEM((tm, tm), BF16), pltpu.VMEM((tm, tm), BF16),
                        pltpu.VMEM(((CONV_WIDTH - 1) * SUBLANES, width), F32),
                        pltpu.VMEM((1, width), F32),
                        pltpu.VMEM((LRU_SUB, tm, width), F32), pltpu.VMEM((LRU_SUB, tm, width), F32),
                        pltpu.VMEM((LRU_SUB, tm, width), F32)],
        compiler_params=_params(("parallel", "arbitrary")),
        name="lru_branch",
    )(xb, w, w, w, w, w, cos, sin, conv_w, conv_b.reshape(1, D_LRU), wa, ba.reshape(1, D_LRU), wi,
      bi.reshape(1, D_LRU), lam.reshape(1, D_LRU))


def _ret_kernel(gc_ref, q_ref, k_ref, v_ref, x_ref, wg_hbm, gain_ref, dm_ref, xi_ref, zeta_ref, o_ref,
                st_ref, wg_ref, wg_sem):
    @pl.when(jnp.logical_and(pl.program_id(0) == 0, pl.program_id(1) == 0))
    def _():
        fetch = pltpu.make_async_copy(wg_hbm, wg_ref, wg_sem)
        fetch.start()
        fetch.wait()

    @pl.when(pl.program_id(1) == 0)
    def _():
        st_ref[...] = jnp.zeros_like(st_ref)

    x = x_ref[...]
    for hd in range(RET_HEADS):
        qc = slice(hd * RET_DK, (hd + 1) * RET_DK)
        vc = slice(hd * RET_DV, (hd + 1) * RET_DV)
        q = q_ref[:, qc]
        k = k_ref[:, qc]
        v = v_ref[:, vc]
        st = st_ref[hd]
        s = lax.dot_general(q, k, (((1,), (1,)), ((), ())), preferred_element_type=F32) * dm_ref[hd]
        o = _dot(s.astype(BF16), v) + _dot(q, st.astype(BF16)) * xi_ref[hd]
        kz = (k.astype(F32) * zeta_ref[hd]).astype(BF16)
        st_ref[hd] = gc_ref[hd] * st + lax.dot_general(kz, v, (((0,), (0,)), ((), ())),
                                                       preferred_element_type=F32)
        mu = jnp.mean(o, axis=-1, keepdims=True)
        oc = o - mu
        var = jnp.mean(oc * oc, axis=-1, keepdims=True)
        oh = oc * lax.rsqrt(var + LN_EPS) * gain_ref[:, vc]
        g = _dot(x, wg_ref[:, vc])
        o_ref[:, vc] = (jax.nn.silu(g) * oh).astype(o_ref.dtype)


def _retention(q, k, v, xb, w_g, gain, batch, seq):
    T = batch * seq
    C = min(RET_CHUNK, seq)
    nchunk = seq // C
    H = RET_HEADS
    log_g = jnp.log1p(-jnp.exp2(-5.0 - jnp.arange(H, dtype=F32)))
    idx = jnp.arange(C, dtype=F32)
    diff = idx[:, None] - idx[None, :]
    dmask = jnp.where(diff >= 0, jnp.exp(jnp.maximum(diff, 0.0)[None] * log_g[:, None, None]), 0.0)
    xi = jnp.exp((idx[None] + 1.0) * log_g[:, None])[:, :, None]
    zeta = jnp.exp((C - 1.0 - idx[None]) * log_g[:, None])[:, :, None]
    g_c = jnp.exp(C * log_g)
    rows = lambda b, n: b * nchunk + n
    whole3 = lambda b, n: (0, 0, 0)
    return pl.pallas_call(
        _ret_kernel,
        grid=(batch, nchunk),
        in_specs=[pl.BlockSpec(memory_space=pltpu.SMEM),
                  pl.BlockSpec((C, D_QK), lambda b, n: (rows(b, n), 0)),
                  pl.BlockSpec((C, D_QK), lambda b, n: (rows(b, n), 0)),
                  pl.BlockSpec((C, D_RV), lambda b, n: (rows(b, n), 0)),
                  pl.BlockSpec((C, D_MODEL), lambda b, n: (rows(b, n), 0)),
                  pl.BlockSpec(memory_space=pl.ANY),
                  pl.BlockSpec((1, D_RV), lambda b, n: (0, 0)),
                  pl.BlockSpec((H, C, C), whole3),
                  pl.BlockSpec((H, C, 1), whole3),
                  pl.BlockSpec((H, C, 1), whole3)],
        out_specs=pl.BlockSpec((C, D_RV), lambda b, n: (rows(b, n), 0)),
        out_shape=jax.ShapeDtypeStruct((T, D_RV), BF16),
        scratch_shapes=[pltpu.VMEM((H, RET_DK, RET_DV), F32), pltpu.VMEM((D_MODEL, D_RV), BF16),
                        pltpu.SemaphoreType.DMA],
        compiler_params=_params(("arbitrary", "arbitrary")),
        name="retention",
    )(g_c, q, k, v, xb, w_g, gain.reshape(1, D_RV), dmask, xi, zeta)


def _mix_kernel(ua_ref, ub_ref, x_ref, wlo_ref, wro_ref, wga_ref, wgb_ref, bga_ref, bgb_ref, o_ref):
    x = x_ref[...]
    ya = _dot(ua_ref[...], wlo_ref[...])
    yb = _dot(ub_ref[...], wro_ref[...])
    ga = jax.nn.sigmoid(_dot(x, wga_ref[...]) + bga_ref[...])
    gb = jax.nn.sigmoid(_dot(x, wgb_ref[...]) + bgb_ref[...])
    o_ref[...] = (ga * ya + gb * yb).astype(o_ref.dtype)


def _mix(ua, ub, xb, w_lru_out, w_ret_out, w_gates, b_gate, tm, tn):
    T = xb.shape[0]
    nj = D_MODEL // tn
    return pl.pallas_call(
        _mix_kernel,
        grid=(T // tm, nj),
        in_specs=[pl.BlockSpec((tm, D_LRU), lambda i, j: (i, 0)),
                  pl.BlockSpec((tm, D_RV), lambda i, j: (i, 0)),
                  pl.BlockSpec((tm, D_MODEL), lambda i, j: (i, 0)),
                  pl.BlockSpec((D_LRU, tn), lambda i, j: (0, j)),
                  pl.BlockSpec((D_RV, tn), lambda i, j: (0, j)),
                  pl.BlockSpec((D_MODEL, tn), lambda i, j: (0, j)),
                  pl.BlockSpec((D_MODEL, tn), lambda i, j: (0, nj + j)),
                  pl.BlockSpec((1, tn), lambda i, j: (0, j)),
                  pl.BlockSpec((1, tn), lambda i, j: (0, nj + j))],
        out_specs=pl.BlockSpec((tm, tn), lambda i, j: (i, j)),
        out_shape=jax.ShapeDtypeStruct((T, D_MODEL), BF16),
        compiler_params=_params(("parallel", "parallel")),
        name="gated_mix",
    )(ua, ub, xb, w_lru_out, w_ret_out, w_gates, w_gates, b_gate.reshape(1, 2 * D_MODEL),
      b_gate.reshape(1, 2 * D_MODEL))


def _layer_norm(y, g, b):
    mu = jnp.mean(y, axis=-1, keepdims=True)
    yc = y - mu
    var = jnp.mean(yc * yc, axis=-1, keepdims=True)
    return yc * lax.rsqrt(var + LN_EPS) * g + b


def _pack_bf16_pairs(hb):
    lo = lax.bitcast_convert_type(hb[:, :HALF].astype(F32), U32)
    hi = lax.bitcast_convert_type(hb[:, HALF:].astype(F32), U32)
    return (hi & jnp.uint32(0xFFFF0000)) | (lo >> 16)


def _unpack_bf16_pairs(p):
    lo = lax.bitcast_convert_type(p << 16, F32).astype(BF16)
    hi = lax.bitcast_convert_type(p & jnp.uint32(0xFFFF0000), F32).astype(BF16)
    return lo, hi


def _max2(a):
    return jnp.max(jnp.max(a, axis=1, keepdims=True), axis=0, keepdims=True)


def _min2(a):
    return jnp.min(jnp.min(a, axis=1, keepdims=True), axis=0, keepdims=True)


def _sum2(a):
    return jnp.sum(jnp.sum(a, axis=1, keepdims=True), axis=0, keepdims=True)


def _ln1_route_kernel(mix_ref, wo_ref, x_ref, g_ref, b_ref, wr_ref, rb_ref,
                      h_ref, dest_ref, wsel_ref, cnt_ref, xs_ref,
                      cnt_sc, base_sc, hp_sc, dest_v, dest_s, row_sem, idx_sem, *, tm, n_tiles, cap):
    i = pl.program_id(0)
    slot = i % 2
    prev = 1 - slot

    def wait_rows(s):
        for _ in range(TOP_K):
            pltpu.make_async_copy(hp_sc.at[s], hp_sc.at[s], row_sem).wait()

    @pl.when(i == 0)
    def _():
        cnt_sc[...] = jnp.zeros_like(cnt_sc)
        base_sc[...] = jnp.zeros_like(base_sc)
        hp_sc[1] = jnp.zeros((tm, HALF), U32)

        def spare(t, c):
            for k in range(TOP_K):
                dest_s[1, k, t] = N_EXPERTS * cap + k
            return c

        lax.fori_loop(0, tm, spare, 0)

    @pl.when(i > 0)
    def _():
        wait_rows(slot)

    for t in range(tm):
        for k in range(TOP_K):
            _row_copy(hp_sc.at[prev], t, xs_ref, dest_s[prev, k, t], row_sem).start()

    y = DN_ALPHA * x_ref[...] + _dot(mix_ref[...], wo_ref[...])
    h = _layer_norm(y, g_ref[...], b_ref[...])
    h_ref[...] = h
    hb = h.astype(BF16)
    hp_sc[slot] = _pack_bf16_pairs(hb)

    G, GS = N_GROUPS, GROUP_SIZE
    logits = lax.dot_general(wr_ref[...], hb, (((1,), (1,)), ((), ())), preferred_element_type=F32)
    scores = jax.nn.sigmoid(logits)
    s3 = scores.reshape(G, GS, tm)
    b3 = (scores + rb_ref[...]).reshape(G, GS, tm)
    neg = jnp.float32(-jnp.inf)

    in_grp = lax.broadcasted_iota(I32, (G, GS, tm), 1)
    m1 = jnp.max(b3, axis=1, keepdims=True)
    f1 = jnp.min(jnp.where(b3 == m1, in_grp, GS), axis=1, keepdims=True)
    m2 = jnp.max(jnp.where(in_grp == f1, neg, b3), axis=1, keepdims=True)
    grp = m1 + m2

    gi = lax.broadcasted_iota(I32, (G, 1, tm), 0)
    gkeep = jnp.zeros((G, 1, tm), F32)
    cur = grp
    for _ in range(TOPK_GROUPS):
        m = jnp.max(cur, axis=0, keepdims=True)
        f = jnp.min(jnp.where(cur == m, gi, G), axis=0, keepdims=True)
        hit = gi == f
        gkeep = jnp.where(hit, 1.0, gkeep)
        cur = jnp.where(hit, neg, cur)
    emask = jnp.broadcast_to(gkeep, (G, GS, tm)) > 0.0

    ei = lax.broadcasted_iota(I32, (G, GS, tm), 0) * GS + in_grp
    cur = jnp.where(emask, b3, neg)
    hits, idxs, ws = [], [], []
    for _ in range(TOP_K):
        m = _max2(cur)
        f = _min2(jnp.where(cur == m, ei, N_EXPERTS))
        hit = ei == f
        cur = jnp.where(hit, neg, cur)
        hits.append(hit)
        idxs.append(f)
        ws.append(_sum2(jnp.where(hit, s3, 0.0)))
    wsum = ws[0]
    for r in range(1, TOP_K):
        wsum = wsum + ws[r]

    sel = jnp.zeros((G, GS, tm), F32)
    for hit in hits:
        sel = jnp.where(hit, 1.0, sel)
    sel2 = sel.reshape(N_EXPERTS, tm)
    tri = (lax.broadcasted_iota(I32, (tm, tm), 0) < lax.broadcasted_iota(I32, (tm, tm), 1))
    rank_local = _dot(sel2.astype(BF16), jnp.where(tri, 1.0, 0.0).astype(BF16))
    base = jnp.where(i < n_tiles, cnt_sc[...], base_sc[...])
    base_sc[...] = base
    rank3 = (rank_local + base[:, 0:1]).reshape(G, GS, tm)
    cnt_new = base + jnp.sum(sel2, axis=1, keepdims=True)
    cnt_sc[...] = cnt_new
    cnt_ref[...] = cnt_new

    for r in range(TOP_K):
        rank_r = _sum2(jnp.where(hits[r], rank3, 0.0)).reshape(1, tm).astype(I32)
        dest_v[r:r + 1, :] = idxs[r].reshape(1, tm) * cap + rank_r
        wsel_ref[r:r + 1, :] = (ws[r] / wsum * ROUTED_SCALE).reshape(1, tm)
    dest_ref[0] = dest_v[...]

    to_smem = pltpu.make_async_copy(dest_v, dest_s.at[slot], idx_sem)
    to_smem.start()
    to_smem.wait()

    @pl.when(i == n_tiles)
    def _():
        wait_rows(prev)


def _ln1_route(mixed, w_o, x2, ln_g, ln_b, w_router_t, router_bias, tm, cap):
    T = x2.shape[0]
    n_tiles = T // tm
    kern = functools.partial(_ln1_route_kernel, tm=tm, n_tiles=n_tiles, cap=cap)
    tile = lambda i: jnp.minimum(i, n_tiles - 1)
    row = lambda i: (tile(i), 0)
    fixed = lambda i: (0, 0)
    return pl.pallas_call(
        kern,
        grid=(n_tiles + 1,),
        in_specs=[pl.BlockSpec((tm, D_MODEL), row),
                  pl.BlockSpec((D_MODEL, D_MODEL), fixed),
                  pl.BlockSpec((tm, D_MODEL), row),
                  pl.BlockSpec((1, D_MODEL), fixed),
                  pl.BlockSpec((1, D_MODEL), fixed),
                  pl.BlockSpec((N_EXPERTS, D_MODEL), fixed),
                  pl.BlockSpec((N_EXPERTS, 1), fixed)],
        out_specs=[pl.BlockSpec((tm, D_MODEL), row),
                   pl.BlockSpec((1, TOP_K, tm), lambda i: (tile(i), 0, 0)),
                   pl.BlockSpec((TOP_K, tm), lambda i: (0, tile(i))),
                   pl.BlockSpec((N_EXPERTS, LANES), fixed),
                   pl.BlockSpec(memory_space=pl.ANY)],
        out_shape=[jax.ShapeDtypeStruct((T, D_MODEL), F32),
                   jax.ShapeDtypeStruct((n_tiles, TOP_K, tm), I32),
                   jax.ShapeDtypeStruct((TOP_K, T), F32),
                   jax.ShapeDtypeStruct((N_EXPERTS, LANES), F32),
                   jax.ShapeDtypeStruct((N_EXPERTS * cap + SUBLANES, HALF), U32)],
        scratch_shapes=[pltpu.VMEM((N_EXPERTS, LANES), F32), pltpu.VMEM((N_EXPERTS, LANES), F32),
                        pltpu.VMEM((2, tm, HALF), U32),
                        pltpu.VMEM((TOP_K, tm), I32),
                        pltpu.SMEM((2, TOP_K, tm), I32),
                        pltpu.SemaphoreType.DMA, pltpu.SemaphoreType.DMA],
        compiler_params=_params(("arbitrary",)),
        name="ln1_route",
    )(mixed, w_o, x2, ln_g.reshape(1, D_MODEL), ln_b.reshape(1, D_MODEL), w_router_t,
      router_bias.reshape(N_EXPERTS, 1))


def _row_copy(src_ref, src_row, dst_ref, dst_row, sem):
    return pltpu.make_async_copy(src_ref.at[pl.ds(src_row, 1)], dst_ref.at[pl.ds(dst_row, 1)], sem)


def _pad_fill_kernel(lo_ref, hi_ref, xs_in_ref, xs_ref, zero_ref, sem):
    del xs_in_ref
    zero_ref[...] = jnp.zeros_like(zero_ref)

    def group_copy(r):
        return pltpu.make_async_copy(zero_ref, xs_ref.at[pl.ds(pl.multiple_of(r, SUBLANES), SUBLANES)], sem)

    def fill_expert(e, counts):
        n_rows, n_groups = counts
        lo = lo_ref[e]
        hi = hi_ref[e]
        mid = jnp.minimum((lo + SUBLANES - 1) // SUBLANES * SUBLANES, hi)
        groups = (hi - mid) // SUBLANES

        def fill_row(r, c):
            _row_copy(zero_ref, 0, xs_ref, r, sem).start()
            return c

        def fill_group(g, c):
            group_copy(mid + g * SUBLANES).start()
            return c

        lax.fori_loop(lo, mid, fill_row, 0)
        lax.fori_loop(0, groups, fill_group, 0)
        return n_rows + (mid - lo), n_groups + groups

    n_rows, n_groups = lax.fori_loop(0, N_EXPERTS, fill_expert, (0, 0))

    def drain_row(r, c):
        _row_copy(zero_ref, 0, xs_ref, 0, sem).wait()
        return c

    def drain_group(g, c):
        group_copy(0).wait()
        return c

    lax.fori_loop(0, n_rows, drain_row, 0)
    lax.fori_loop(0, n_groups, drain_group, 0)


def _pad_fill(fill_lo, fill_hi, xs):
    smem = pl.BlockSpec(memory_space=pltpu.SMEM)
    return pl.pallas_call(
        _pad_fill_kernel,
        in_specs=[smem, smem, pl.BlockSpec(memory_space=pl.ANY)],
        out_specs=pl.BlockSpec(memory_space=pl.ANY),
        out_shape=jax.ShapeDtypeStruct(xs.shape, xs.dtype),
        scratch_shapes=[pltpu.VMEM((SUBLANES, HALF), U32), pltpu.SemaphoreType.DMA],
        input_output_aliases={2: 0},
        name="moe_pad_fill",
    )(fill_lo, fill_hi, xs)


def _experts_kernel(be_ref, nb_ref, brow_ref, xs_ref, wg_ref, wu_ref, wd_ref, ys_ref,
                    wgb_ref, wub_ref, wdb_ref):
    del brow_ref
    b = pl.program_id(0)
    new_expert = jnp.logical_or(b == 0, be_ref[b] != be_ref[jnp.maximum(b - 1, 0)])

    @pl.when(jnp.logical_and(b < nb_ref[0], new_expert))
    def _():
        wgb_ref[...] = wg_ref[0].astype(BF16)
        wub_ref[...] = wu_ref[0].astype(BF16)
        wdb_ref[...] = wd_ref[0].astype(BF16)

    @pl.when(b < nb_ref[0])
    def _():
        lo, hi = _unpack_bf16_pairs(xs_ref[...])
        gate = _dot(lo, wgb_ref[:HALF, :]) + _dot(hi, wgb_ref[HALF:, :])
        up = _dot(lo, wub_ref[:HALF, :]) + _dot(hi, wub_ref[HALF:, :])
        hb = (jax.nn.silu(gate) * up).astype(BF16)
        ys_ref[...] = _pack_bf16_pairs(_dot(hb, wdb_ref[...]).astype(BF16))


def _experts(block_e, nb_used, block_row, xs, wg, wu, wd):
    n_rows = xs.shape[0] // MOE_ROWS * MOE_ROWS
    nb = block_e.shape[0]
    rows = lambda b, be, nbu, brow: (brow[b], 0)
    wsel = lambda b, be, nbu, brow: (be[b], 0, 0)
    grid_spec = pltpu.PrefetchScalarGridSpec(
        num_scalar_prefetch=3,
        grid=(nb,),
        in_specs=[pl.BlockSpec((MOE_ROWS, HALF), rows),
                  pl.BlockSpec((1, D_MODEL, D_EXPERT), wsel),
                  pl.BlockSpec((1, D_MODEL, D_EXPERT), wsel),
                  pl.BlockSpec((1, D_EXPERT, D_MODEL), wsel)],
        out_specs=pl.BlockSpec((MOE_ROWS, HALF), rows),
        scratch_shapes=[pltpu.VMEM((D_MODEL, D_EXPERT), BF16), pltpu.VMEM((D_MODEL, D_EXPERT), BF16),
                        pltpu.VMEM((D_EXPERT, D_MODEL), BF16)],
    )
    return pl.pallas_call(
        _experts_kernel,
        grid_spec=grid_spec,
        out_shape=jax.ShapeDtypeStruct((n_rows, HALF), U32),
        compiler_params=_params(("arbitrary",)),
        name="moe_experts",
    )(block_e, nb_used, block_row, xs, wg, wu, wd)


def _combine_kernel(dest_ref, next_ref, w_ref, h_ref, wgs_ref, wus_ref, wds_ref, g_ref, b_ref,
                    ys_ref, o_ref, buf_ref, sem, *, tm):
    i = pl.program_id(0)
    slot = i % 2

    def gather(idx_ref, dst_slot, t):
        for k in range(TOP_K):
            _row_copy(ys_ref, idx_ref[0, k, t], buf_ref.at[dst_slot, k], t, sem.at[dst_slot]).start()

    def wait_tile(s):
        pltpu.make_async_copy(buf_ref.at[s], buf_ref.at[s], sem.at[s]).wait()

    @pl.when(i == 0)
    def _():
        def first(t, c):
            gather(dest_ref, 0, t)
            return c

        lax.fori_loop(0, tm, first, 0)

    wait_tile(slot)
    for t in range(tm):
        gather(next_ref, 1 - slot, t)

    h = h_ref[...]
    hb = h.astype(BF16)
    hid = (jax.nn.silu(_dot(hb, wgs_ref[...])) * _dot(hb, wus_ref[...])).astype(BF16)
    shared = _dot(hid, wds_ref[...])

    w = w_ref[...]
    lo_sum = jnp.zeros((tm, HALF), F32)
    hi_sum = jnp.zeros((tm, HALF), F32)
    for k in range(TOP_K):
        p = buf_ref[slot, k]
        lo_sum = lo_sum + lax.bitcast_convert_type(p << 16, F32) * w[:, k:k + 1]
        hi_sum = hi_sum + lax.bitcast_convert_type(p & jnp.uint32(0xFFFF0000), F32) * w[:, k:k + 1]
    routed = jnp.concatenate([lo_sum, hi_sum], axis=1)
    o_ref[...] = _layer_norm(DN_ALPHA * h + (routed + shared), g_ref[...], b_ref[...])

    @pl.when(i == pl.num_programs(0) - 1)
    def _():
        wait_tile(1 - slot)


def _combine(dest3, w_tok, h, wgs, wus, wds, ln_g, ln_b, ys3):
    T = h.shape[0]
    n_tiles, _, tm = dest3.shape
    kern = functools.partial(_combine_kernel, tm=tm)
    row = lambda i: (i, 0)
    fixed = lambda i: (0, 0)
    return pl.pallas_call(
        kern,
        grid=(n_tiles,),
        in_specs=[pl.BlockSpec((1, TOP_K, tm), lambda i: (i, 0, 0), memory_space=pltpu.SMEM),
                  pl.BlockSpec((1, TOP_K, tm), lambda i: (jnp.minimum(i + 1, n_tiles - 1), 0, 0),
                               memory_space=pltpu.SMEM),
                  pl.BlockSpec((tm, TOP_K), row),
                  pl.BlockSpec((tm, D_MODEL), row),
                  pl.BlockSpec((D_MODEL, D_SHARED), fixed),
                  pl.BlockSpec((D_MODEL, D_SHARED), fixed),
                  pl.BlockSpec((D_SHARED, D_MODEL), fixed),
                  pl.BlockSpec((1, D_MODEL), fixed),
                  pl.BlockSpec((1, D_MODEL), fixed),
                  pl.BlockSpec(memory_space=pl.ANY)],
        out_specs=pl.BlockSpec((tm, D_MODEL), row),
        out_shape=jax.ShapeDtypeStruct((T, D_MODEL), F32),
        scratch_shapes=[pltpu.VMEM((2, TOP_K, tm, HALF), U32), pltpu.SemaphoreType.DMA((2,))],
        compiler_params=_params(("arbitrary",)),
        name="moe_combine",
    )(dest3, dest3, w_tok, h, wgs, wus, wds, ln_g.reshape(1, D_MODEL), ln_b.reshape(1, D_MODEL), ys3)


def _tile(n, pref):
    t = min(n, pref)
    assert n % t == 0, (n, pref)
    return t


def _layer(x, w_in, conv_w, conv_b, lru_wa, lru_ba, lru_wi, lru_bi, lru_lambda, ret_gn_gain,
           w_lru_out, w_ret_out, b_gate, w_o, ln1_g, ln1_b, w_router, router_bias,
           w_gate_e, w_up_e, w_down_e, w_gate_s, w_up_s, w_down_s, ln2_g, ln2_b):
    B, S, D = x.shape
    T = B * S
    x2 = x.reshape(T, D)
    xb = x2.astype(BF16)

    half = RET_DK // 2
    freq = ROPE_THETA ** (-jnp.arange(half, dtype=F32) / half)
    ang = jnp.arange(S, dtype=I32).astype(F32)[:, None] * freq
    cos, sin = jnp.cos(ang), jnp.sin(ang)

    ua, v, q, k = _lru_branch(xb, w_in, cos, sin, conv_w, conv_b, lru_wa.astype(BF16), lru_ba,
                              lru_wi.astype(BF16), lru_bi, lru_lambda, S, _tile(S, 256), 512)
    ub = _retention(q, k, v, xb, w_in[:, OFF_G:OFF_GA].astype(BF16), ret_gn_gain, B, S)

    mixed = _mix(ua, ub, xb, w_lru_out.astype(BF16), w_ret_out.astype(BF16),
                 w_in[:, OFF_GA:].astype(BF16), b_gate, _tile(T, 512), 512)

    tm_r = _tile(T, 256)
    cap = (T + MOE_ROWS - 1) // MOE_ROWS * MOE_ROWS
    h, dest, wsel, cnt, xs = _ln1_route(mixed, w_o.astype(BF16), x2, ln1_g, ln1_b,
                                        w_router.T.astype(BF16), router_bias, tm_r, cap)

    counts = cnt[:, 0].astype(I32)
    e_blocks = (counts + MOE_ROWS - 1) // MOE_ROWS
    blocks_end = jnp.cumsum(e_blocks)
    nb = (T * TOP_K + N_EXPERTS * (MOE_ROWS - 1) + MOE_ROWS - 1) // MOE_ROWS
    nb_used = blocks_end[-1:]
    blk = jnp.minimum(jnp.arange(nb, dtype=I32), nb_used - 1)
    block_e = jnp.minimum(jnp.sum((blocks_end[None, :] <= blk[:, None]).astype(I32), axis=1),
                          N_EXPERTS - 1)
    blocks_start = blocks_end - e_blocks
    e_onehot = block_e[:, None] == jnp.arange(N_EXPERTS, dtype=I32)[None, :]
    block_row = block_e * (cap // MOE_ROWS) + blk - jnp.sum(jnp.where(e_onehot, blocks_start[None, :], 0),
                                                            axis=1)
    e_row0 = jnp.arange(N_EXPERTS, dtype=I32) * cap

    xs = _pad_fill(e_row0 + counts, e_row0 + e_blocks * MOE_ROWS, xs)
    ys = _experts(block_e, nb_used, block_row, xs, w_gate_e, w_up_e, w_down_e)
    out = _combine(dest, wsel.T, h, w_gate_s.astype(BF16), w_up_s.astype(BF16),
                   w_down_s.astype(BF16), ln2_g, ln2_b, ys)
    return out.reshape(B, S, D)


def kernel(x, w_in, conv_w, conv_b, lru_wa, lru_ba, lru_wi, lru_bi, lru_lambda, ret_gn_gain, w_lru_out, w_ret_out, b_gate, w_o, ln1_g, ln1_b, w_router, router_bias, w_gate_e, w_up_e, w_down_e, w_gate_s, w_up_s, w_down_s, ln2_g, ln2_b):
    assert DEPTH == 1 and w_in.shape[0] == DEPTH
    args = (w_in, conv_w, conv_b, lru_wa, lru_ba, lru_wi, lru_bi, lru_lambda, ret_gn_gain,
            w_lru_out, w_ret_out, b_gate, w_o, ln1_g, ln1_b, w_router, router_bias,
            w_gate_e, w_up_e, w_down_e, w_gate_s, w_up_s, w_down_s, ln2_g, ln2_b)
    return _layer(x, *[a[0] for a in args])
```

```python
import functools

import jax
import jax.numpy as jnp
from jax import lax
from jax.experimental import pallas as pl
from jax.experimental.pallas import tpu as pltpu

F32 = jnp.float32
BF16 = jnp.bfloat16
I32 = jnp.int32
U32 = jnp.uint32

D_MODEL = 2048
D_LRU = 2048
LRU_HEADS = 16
LRU_BLOCK = D_LRU // LRU_HEADS
CONV_WIDTH = 4
LRU_C = 8.0
RET_HEADS = 8
RET_DK = 256
RET_DV = 512
D_QK = RET_HEADS * RET_DK
D_RV = RET_HEADS * RET_DV
ROPE_THETA = 10000.0
N_EXPERTS = 64
TOP_K = 8
N_GROUPS = 8
GROUP_SIZE = N_EXPERTS // N_GROUPS
TOPK_GROUPS = 4
D_EXPERT = 512
D_SHARED = 512
ROUTED_SCALE = 2.5
DEPTH = 1
DN_ALPHA = (2.0 * DEPTH) ** 0.25
LN_EPS = 1e-5

OFF_LRU_X = 0
OFF_LRU_Y = OFF_LRU_X + D_LRU
OFF_Q = OFF_LRU_Y + D_LRU
OFF_K = OFF_Q + D_QK
OFF_V = OFF_K + D_QK
OFF_G = OFF_V + D_RV
OFF_GA = OFF_G + D_RV
OFF_GB = OFF_GA + D_MODEL

V7X_VMEM_LIMIT = 56 * 1024 * 1024
LANES = 128
SUBLANES = 8
MOE_ROWS = 512
RET_CHUNK = 256
LRU_SUB = 2
HALF = D_MODEL // 2


def _params(sem):
    return pltpu.CompilerParams(dimension_semantics=sem, vmem_limit_bytes=V7X_VMEM_LIMIT)


def _dot(a, b):
    return jnp.dot(a, b, preferred_element_type=F32)


def _lru_kernel(x_ref, wx_ref, wy_ref, wv_ref, wq_ref, wk_ref, cos_ref, sin_ref, cw_ref, cb_ref,
                wai_ref, ba_ref, bi_ref, lam_ref,
                o_ref, v_ref, q_ref, k_ref,
                wxb_ref, wyb_ref, wvb_ref, wqb_ref, wkb_ref, perm_ref, unperm_ref, tail_ref, hc_ref,
                a_ref, b_ref, gy_ref, *, tiles_per_seq, tm, width):
    i = pl.program_id(1)
    seg = tm // SUBLANES
    halo = (CONV_WIDTH - 1) * SUBLANES

    @pl.when(i == 0)
    def _():
        wxb_ref[...] = wx_ref[...].astype(BF16)
        wyb_ref[...] = wy_ref[...].astype(BF16)
        wvb_ref[...] = wv_ref[...].astype(BF16)
        wqb_ref[...] = wq_ref[...].astype(BF16)
        wkb_ref[...] = wk_ref[...].astype(BF16)
        r = lax.broadcasted_iota(I32, (tm, tm), 0)
        c = lax.broadcasted_iota(I32, (tm, tm), 1)
        perm_ref[...] = jnp.where(c == (r & (SUBLANES - 1)) * seg + (r >> 3), 1.0, 0.0).astype(BF16)
        unperm_ref[...] = jnp.where(r == (c & (SUBLANES - 1)) * seg + (c >> 3), 1.0, 0.0).astype(BF16)
        tail_ref[...] = jnp.zeros_like(tail_ref)
        hc_ref[...] = jnp.zeros_like(hc_ref)

    cw = cw_ref[...]
    cb = cb_ref[...]
    taps = [cw[CONV_WIDTH - 1 - d:CONV_WIDTH - d] for d in range(CONV_WIDTH)]
    sub = lax.broadcasted_iota(I32, (SUBLANES, width), 0)
    row = lax.broadcasted_iota(I32, (tm, width), 0)
    v_piece = v_ref.shape[1] // 4

    def one_tile(u, tail, carry):
        rows_u = slice(u * tm, (u + 1) * tm)
        first = ((i * LRU_SUB + u) % tiles_per_seq) == 0
        tail = jnp.where(first, 0.0, tail)
        carry = jnp.where(first, 0.0, carry)
        x = x_ref[rows_u, :]

        def emit_v(piece):
            cols = slice(piece * v_piece, (piece + 1) * v_piece)
            v_ref[rows_u, cols] = _dot(x, wvb_ref[:, cols]).astype(v_ref.dtype)

        def emit_rotary(wb_ref, out_ref, scale):
            acc = _dot(x, wb_ref[...])
            cos = cos_ref[rows_u, :]
            sin = sin_ref[rows_u, :]
            half = RET_DK // 2
            for hh in range(width // RET_DK):
                c0 = hh * RET_DK
                t1 = acc[:, c0:c0 + half]
                t2 = acc[:, c0 + half:c0 + RET_DK]
                out_ref[rows_u, c0:c0 + half] = ((t1 * cos - t2 * sin) * scale).astype(out_ref.dtype)
                out_ref[rows_u, c0 + half:c0 + RET_DK] = (
                    (t1 * sin + t2 * cos) * scale).astype(out_ref.dtype)

        emit_v(0)
        xp = _dot(perm_ref[...], x).astype(BF16)
        px = _dot(xp, wxb_ref[...])
        gy_ref[u] = jax.nn.gelu(_dot(xp, wyb_ref[...]), approximate=True)

        xa = cb + taps[0] * px
        for d in range(1, CONV_WIDTH):
            back = jnp.concatenate([px[tm - d * SUBLANES:], px[:tm - d * SUBLANES]], axis=0)
            xa = xa + taps[d] * back
        groups = {}
        for g in range(CONV_WIDTH - 1):
            rows = slice(g * SUBLANES, (g + 1) * SUBLANES)
            cur_end = px[tm - halo + g * SUBLANES:tm - halo + (g + 1) * SUBLANES]
            groups[g - (CONV_WIDTH - 1)] = jnp.where(sub == 0, pltpu.roll(tail[rows, :], 1, 0),
                                                     pltpu.roll(cur_end, 1, 0))
            groups[g] = px[rows, :]
        head = []
        for q in range(CONV_WIDTH - 1):
            acc = cb + taps[0] * groups[q]
            for d in range(1, CONV_WIDTH):
                acc = acc + taps[d] * groups[q - d]
            head.append(acc)
        xa = jnp.concatenate(head + [xa[halo:]], axis=0)
        tail_out = px[tm - halo:]
        emit_v(1)

        xab = xa.astype(BF16)
        r_parts, i_parts = [], []
        for hh in range(width // LRU_BLOCK):
            both = _dot(xab[:, hh * LRU_BLOCK:(hh + 1) * LRU_BLOCK], wai_ref[hh])
            r_parts.append(both[:, :LRU_BLOCK])
            i_parts.append(both[:, LRU_BLOCK:])
        rg = jax.nn.sigmoid(jnp.concatenate(r_parts, axis=1) + ba_ref[...])
        ig = jax.nn.sigmoid(jnp.concatenate(i_parts, axis=1) + bi_ref[...])
        emit_rotary(wqb_ref, q_ref, 1.0)

        log_a = -LRU_C * rg * jax.nn.softplus(-lam_ref[...])
        a = jnp.exp(log_a)
        mult = jnp.sqrt(-jnp.tanh(log_a) * (a * a + 1.0))
        mult = jnp.where(jnp.logical_and(first, row == 0), 1.0, mult)
        a_ref[u] = a
        b_ref[u] = mult * ig * xa
        emit_v(2)

        run_a = jnp.ones((SUBLANES, width), F32)
        run_h = jnp.zeros((SUBLANES, width), F32)
        for q in range(seg):
            rows = slice(q * SUBLANES, (q + 1) * SUBLANES)
            aq = a_ref[u, rows, :]
            run_h = aq * run_h + b_ref[u, rows, :]
            run_a = aq * run_a
            b_ref[u, rows, :] = run_h
            a_ref[u, rows, :] = run_a
        emit_v(3)

        carries = []
        for s in range(SUBLANES):
            carries.append(carry)
            carry = run_a[s:s + 1, :] * carry + run_h[s:s + 1, :]
        carry_in = jnp.concatenate(carries, axis=0)
        for q in range(seg):
            rows = slice(q * SUBLANES, (q + 1) * SUBLANES)
            b_ref[u, rows, :] = (b_ref[u, rows, :] + a_ref[u, rows, :] * carry_in) * gy_ref[u, rows, :]

        emit_rotary(wkb_ref, k_ref, RET_DK ** -0.5)
        o_ref[rows_u, :] = _dot(unperm_ref[...], b_ref[u].astype(BF16)).astype(o_ref.dtype)
        return tail_out, carry

    tail = tail_ref[...]
    carry = hc_ref[...]
    for u in range(LRU_SUB):
        tail, carry = one_tile(u, tail, carry)
    tail_ref[...] = tail
    hc_ref[...] = carry


def _lru_branch(xb, w, cos, sin, conv_w, conv_b, wa, ba, wi, bi, lam, seq, tm, width):
    T, K = xb.shape
    nj = D_LRU // width
    assert D_QK == D_LRU, "q / k column tiles are walked together with the lru tiles"
    jy0 = OFF_LRU_Y // width
    jq0 = OFF_Q // width
    jk0 = OFF_K // width
    hp = width // LRU_BLOCK
    vw = D_RV // nj
    jv0 = OFF_V // vw
    tiles_per_seq = seq // tm
    rows = LRU_SUB * tm
    reps = max(1, rows // seq)
    cos, sin = jnp.tile(cos, (reps, 1)), jnp.tile(sin, (reps, 1))
    table_blocks = cos.shape[0] // rows
    kern = functools.partial(_lru_kernel, tiles_per_seq=tiles_per_seq, tm=tm, width=width)
    vec = lambda j, i: (0, j)
    once = pl.Buffered(1)
    table = pl.BlockSpec((rows, RET_DK // 2), lambda j, i: (i % table_blocks, 0))
    return pl.pallas_call(
        kern,
        grid=(nj, T // rows),
        in_specs=[pl.BlockSpec((rows, K), lambda j, i: (i, 0)),
                  pl.BlockSpec((K, width), lambda j, i: (0, j), pipeline_mode=once),
                  pl.BlockSpec((K, width), lambda j, i: (0, jy0 + j), pipeline_mode=once),
                  pl.BlockSpec((K, vw), lambda j, i: (0, jv0 + j), pipeline_mode=once),
                  pl.BlockSpec((K, width), lambda j, i: (0, jq0 + j), pipeline_mode=once),
                  pl.BlockSpec((K, width), lambda j, i: (0, jk0 + j), pipeline_mode=once),
                  table, table,
                  pl.BlockSpec((CONV_WIDTH, width), vec),
                  pl.BlockSpec((1, width), vec),
                  pl.BlockSpec((hp, LRU_BLOCK, 2 * LRU_BLOCK), lambda j, i: (j, 0, 0)),
                  pl.BlockSpec((1, width), vec),
                  pl.BlockSpec((1, width), vec),
                  pl.BlockSpec((1, width), vec)],
        out_specs=[pl.BlockSpec((rows, width), lambda j, i: (i, j)),
                   pl.BlockSpec((rows, vw), lambda j, i: (i, j)),
                   pl.BlockSpec((rows, width), lambda j, i: (i, j)),
                   pl.BlockSpec((rows, width), lambda j, i: (i, j))],
        out_shape=[jax.ShapeDtypeStruct((T, D_LRU), BF16), jax.ShapeDtypeStruct((T, D_RV), BF16),
                   jax.ShapeDtypeStruct((T, D_QK), BF16), jax.ShapeDtypeStruct((T, D_QK), BF16)],
        scratch_shapes=[pltpu.VMEM((K, width), BF16), pltpu.VMEM((K, width), BF16),
                        pltpu.VMEM((K, vw), BF16),
                        pltpu.VMEM((K, width), BF16), pltpu.VMEM((K, width), BF16),
                        pltpu.VMEM((tm, tm), BF16), pltpu.VMEM((tm, tm), BF16),
                        pltpu.VMEM(((CONV_WIDTH - 1) * SUBLANES, width), F32),
                        pltpu.VMEM((1, width), F32),
                        pltpu.VMEM((LRU_SUB, tm, width), F32), pltpu.VMEM((LRU_SUB, tm, width), F32),
                        pltpu.VMEM((LRU_SUB, tm, width), F32)],
        compiler_params=_params(("parallel", "arbitrary")),
        name="lru_branch",
    )(xb, w, w, w, w, w, cos, sin, conv_w, conv_b.reshape(1, D_LRU),
      jnp.concatenate([wa, wi], axis=-1).astype(BF16), ba.reshape(1, D_LRU), bi.reshape(1, D_LRU),
      lam.reshape(1, D_LRU))


def _ret_kernel(gc_ref, q_ref, k_ref, v_ref, x_ref, w_hbm, gain_ref, dm_ref, xi_ref, zeta_ref, o_ref,
                st_ref, wg_ref, stage_ref, wg_sem):
    @pl.when(jnp.logical_and(pl.program_id(0) == 0, pl.program_id(1) == 0))
    def _():
        for hd in range(RET_HEADS):
            cols = pl.ds(OFF_G + hd * RET_DV, RET_DV)
            fetch = pltpu.make_async_copy(w_hbm.at[:, cols], stage_ref, wg_sem)
            fetch.start()
            fetch.wait()
            wg_ref[:, hd * RET_DV:(hd + 1) * RET_DV] = stage_ref[...].astype(BF16)

    @pl.when(pl.program_id(1) == 0)
    def _():
        st_ref[...] = jnp.zeros_like(st_ref)

    x = x_ref[...]
    for hd in range(RET_HEADS):
        qc = slice(hd * RET_DK, (hd + 1) * RET_DK)
        vc = slice(hd * RET_DV, (hd + 1) * RET_DV)
        q = q_ref[:, qc]
        k = k_ref[:, qc]
        v = v_ref[:, vc]
        st = st_ref[hd]
        s = lax.dot_general(q, k, (((1,), (1,)), ((), ())), preferred_element_type=F32) * dm_ref[hd]
        o = _dot(s.astype(BF16), v) + _dot(q, st.astype(BF16)) * xi_ref[hd]
        kz = (k.astype(F32) * zeta_ref[hd]).astype(BF16)
        st_ref[hd] = gc_ref[hd] * st + lax.dot_general(kz, v, (((0,), (0,)), ((), ())),
                                                       preferred_element_type=F32)
        mu = jnp.mean(o, axis=-1, keepdims=True)
        oc = o - mu
        var = jnp.mean(oc * oc, axis=-1, keepdims=True)
        oh = oc * lax.rsqrt(var + LN_EPS) * gain_ref[:, vc]
        g = _dot(x, wg_ref[:, vc])
        o_ref[:, vc] = (jax.nn.silu(g) * oh).astype(o_ref.dtype)


def _retention(q, k, v, xb, w_in, gain, batch, seq):
    T = batch * seq
    C = min(RET_CHUNK, seq)
    nchunk = seq // C
    H = RET_HEADS
    log_g = jnp.log1p(-jnp.exp2(-5.0 - jnp.arange(H, dtype=F32)))
    idx = jnp.arange(C, dtype=F32)
    diff = idx[:, None] - idx[None, :]
    dmask = jnp.where(diff >= 0, jnp.exp(jnp.maximum(diff, 0.0)[None] * log_g[:, None, None]), 0.0)
    xi = jnp.exp((idx[None] + 1.0) * log_g[:, None])[:, :, None]
    zeta = jnp.exp((C - 1.0 - idx[None]) * log_g[:, None])[:, :, None]
    g_c = jnp.exp(C * log_g)
    rows = lambda b, n: b * nchunk + n
    whole3 = lambda b, n: (0, 0, 0)
    return pl.pallas_call(
        _ret_kernel,
        grid=(batch, nchunk),
        in_specs=[pl.BlockSpec(memory_space=pltpu.SMEM),
                  pl.BlockSpec((C, D_QK), lambda b, n: (rows(b, n), 0)),
                  pl.BlockSpec((C, D_QK), lambda b, n: (rows(b, n), 0)),
                  pl.BlockSpec((C, D_RV), lambda b, n: (rows(b, n), 0)),
                  pl.BlockSpec((C, D_MODEL), lambda b, n: (rows(b, n), 0)),
                  pl.BlockSpec(memory_space=pl.ANY),
                  pl.BlockSpec((1, D_RV), lambda b, n: (0, 0)),
                  pl.BlockSpec((H, C, C), whole3),
                  pl.BlockSpec((H, C, 1), whole3),
                  pl.BlockSpec((H, C, 1), whole3)],
        out_specs=pl.BlockSpec((C, D_RV), lambda b, n: (rows(b, n), 0)),
        out_shape=jax.ShapeDtypeStruct((T, D_RV), BF16),
        scratch_shapes=[pltpu.VMEM((H, RET_DK, RET_DV), F32), pltpu.VMEM((D_MODEL, D_RV), BF16),
                        pltpu.VMEM((D_MODEL, RET_DV), F32), pltpu.SemaphoreType.DMA],
        compiler_params=_params(("arbitrary", "arbitrary")),
        name="retention",
    )(g_c, q, k, v, xb, w_in, gain.reshape(1, D_RV), dmask, xi, zeta)


def _mix_kernel(ua_ref, ub_ref, x_ref, wlo_ref, wro_ref, wga_ref, wgb_ref, bga_ref, bgb_ref, o_ref):
    x = x_ref[...]
    ya = _dot(ua_ref[...], wlo_ref[...])
    yb = _dot(ub_ref[...], wro_ref[...])
    ga = jax.nn.sigmoid(_dot(x, wga_ref[...]) + bga_ref[...])
    gb = jax.nn.sigmoid(_dot(x, wgb_ref[...]) + bgb_ref[...])
    o_ref[...] = (ga * ya + gb * yb).astype(o_ref.dtype)


def _mix(ua, ub, xb, w_lru_out, w_ret_out, w_gates, b_gate, tm, tn):
    T = xb.shape[0]
    nj = D_MODEL // tn
    return pl.pallas_call(
        _mix_kernel,
        grid=(T // tm, nj),
        in_specs=[pl.BlockSpec((tm, D_LRU), lambda i, j: (i, 0)),
                  pl.BlockSpec((tm, D_RV), lambda i, j: (i, 0)),
                  pl.BlockSpec((tm, D_MODEL), lambda i, j: (i, 0)),
                  pl.BlockSpec((D_LRU, tn), lambda i, j: (0, j)),
                  pl.BlockSpec((D_RV, tn), lambda i, j: (0, j)),
                  pl.BlockSpec((D_MODEL, tn), lambda i, j: (0, j)),
                  pl.BlockSpec((D_MODEL, tn), lambda i, j: (0, nj + j)),
                  pl.BlockSpec((1, tn), lambda i, j: (0, j)),
                  pl.BlockSpec((1, tn), lambda i, j: (0, nj + j))],
        out_specs=pl.BlockSpec((tm, tn), lambda i, j: (i, j)),
        out_shape=jax.ShapeDtypeStruct((T, D_MODEL), BF16),
        compiler_params=_params(("parallel", "parallel")),
        name="gated_mix",
    )(ua, ub, xb, w_lru_out, w_ret_out, w_gates, w_gates, b_gate.reshape(1, 2 * D_MODEL),
      b_gate.reshape(1, 2 * D_MODEL))


def _layer_norm(y, g, b):
    mu = jnp.mean(y, axis=-1, keepdims=True)
    yc = y - mu
    var = jnp.mean(yc * yc, axis=-1, keepdims=True)
    return yc * lax.rsqrt(var + LN_EPS) * g + b


def _pack_bf16_pairs(hb):
    lo = lax.bitcast_convert_type(hb[:, :HALF].astype(F32), U32)
    hi = lax.bitcast_convert_type(hb[:, HALF:].astype(F32), U32)
    return (hi & jnp.uint32(0xFFFF0000)) | (lo >> 16)


def _unpack_bf16_pairs(p):
    lo = lax.bitcast_convert_type(p << 16, F32).astype(BF16)
    hi = lax.bitcast_convert_type(p & jnp.uint32(0xFFFF0000), F32).astype(BF16)
    return lo, hi


def _max2(a):
    return jnp.max(jnp.max(a, axis=1, keepdims=True), axis=0, keepdims=True)


def _min2(a):
    return jnp.min(jnp.min(a, axis=1, keepdims=True), axis=0, keepdims=True)


def _sum2(a):
    return jnp.sum(jnp.sum(a, axis=1, keepdims=True), axis=0, keepdims=True)


def _ln1_route_kernel(mix_ref, wo_ref, x_ref, g_ref, b_ref, wr_ref, rb_ref,
                      h_ref, dest_ref, wsel_ref, cnt_ref, xs_ref,
                      cnt_sc, base_sc, hp_sc, dest_v, dest_s, row_sem, idx_sem, *, tm, n_tiles, cap):
    i = pl.program_id(0)
    slot = i % 2
    prev = 1 - slot

    def wait_rows(s):
        for _ in range(TOP_K):
            pltpu.make_async_copy(hp_sc.at[s], hp_sc.at[s], row_sem).wait()

    @pl.when(i == 0)
    def _():
        cnt_sc[...] = jnp.zeros_like(cnt_sc)
        base_sc[...] = jnp.zeros_like(base_sc)
        hp_sc[1] = jnp.zeros((tm, HALF), U32)

        def spare(t, c):
            for k in range(TOP_K):
                dest_s[1, k, t] = N_EXPERTS * cap + k
            return c

        lax.fori_loop(0, tm, spare, 0)

    @pl.when(i > 0)
    def _():
        wait_rows(slot)

    for t in range(tm):
        for k in range(TOP_K):
            _row_copy(hp_sc.at[prev], t, xs_ref, dest_s[prev, k, t], row_sem).start()

    y = DN_ALPHA * x_ref[...] + _dot(mix_ref[...], wo_ref[...])
    h = _layer_norm(y, g_ref[...], b_ref[...])
    h_ref[...] = h
    hb = h.astype(BF16)
    hp_sc[slot] = _pack_bf16_pairs(hb)

    G, GS = N_GROUPS, GROUP_SIZE
    logits = lax.dot_general(wr_ref[...], hb, (((1,), (1,)), ((), ())), preferred_element_type=F32)
    scores = jax.nn.sigmoid(logits)
    s3 = scores.reshape(G, GS, tm)
    b3 = (scores + rb_ref[...]).reshape(G, GS, tm)
    neg = jnp.float32(-jnp.inf)

    in_grp = lax.broadcasted_iota(I32, (G, GS, tm), 1)
    m1 = jnp.max(b3, axis=1, keepdims=True)
    f1 = jnp.min(jnp.where(b3 == m1, in_grp, GS), axis=1, keepdims=True)
    m2 = jnp.max(jnp.where(in_grp == f1, neg, b3), axis=1, keepdims=True)
    grp = m1 + m2

    gi = lax.broadcasted_iota(I32, (G, 1, tm), 0)
    gkeep = jnp.zeros((G, 1, tm), F32)
    cur = grp
    for _ in range(TOPK_GROUPS):
        m = jnp.max(cur, axis=0, keepdims=True)
        f = jnp.min(jnp.where(cur == m, gi, G), axis=0, keepdims=True)
        hit = gi == f
        gkeep = jnp.where(hit, 1.0, gkeep)
        cur = jnp.where(hit, neg, cur)
    emask = jnp.broadcast_to(gkeep, (G, GS, tm)) > 0.0

    ei = lax.broadcasted_iota(I32, (G, GS, tm), 0) * GS + in_grp
    cur = jnp.where(emask, b3, neg)
    hits, idxs, ws = [], [], []
    for _ in range(TOP_K):
        m = _max2(cur)
        f = _min2(jnp.where(cur == m, ei, N_EXPERTS))
        hit = ei == f
        cur = jnp.where(hit, neg, cur)
        hits.append(hit)
        idxs.append(f)
        ws.append(_sum2(jnp.where(hit, s3, 0.0)))
    wsum = ws[0]
    for r in range(1, TOP_K):
        wsum = wsum + ws[r]

    sel = jnp.zeros((G, GS, tm), F32)
    for hit in hits:
        sel = jnp.where(hit, 1.0, sel)
    sel2 = sel.reshape(N_EXPERTS, tm)
    tri = (lax.broadcasted_iota(I32, (tm, tm), 0) < lax.broadcasted_iota(I32, (tm, tm), 1))
    rank_local = _dot(sel2.astype(BF16), jnp.where(tri, 1.0, 0.0).astype(BF16))
    base = jnp.where(i < n_tiles, cnt_sc[...], base_sc[...])
    base_sc[...] = base
    rank3 = (rank_local + base[:, 0:1]).reshape(G, GS, tm)
    cnt_new = base + jnp.sum(sel2, axis=1, keepdims=True)
    cnt_sc[...] = cnt_new
    cnt_ref[...] = cnt_new

    for r in range(TOP_K):
        rank_r = _sum2(jnp.where(hits[r], rank3, 0.0)).reshape(1, tm).astype(I32)
        dest_v[r:r + 1, :] = idxs[r].reshape(1, tm) * cap + rank_r
        wsel_ref[r:r + 1, :] = (ws[r] / wsum * ROUTED_SCALE).reshape(1, tm)
    dest_ref[0] = dest_v[...]

    to_smem = pltpu.make_async_copy(dest_v, dest_s.at[slot], idx_sem)
    to_smem.start()
    to_smem.wait()

    @pl.when(i == n_tiles)
    def _():
        wait_rows(prev)


def _ln1_route(mixed, w_o, x2, ln_g, ln_b, w_router_t, router_bias, tm, cap):
    T = x2.shape[0]
    n_tiles = T // tm
    kern = functools.partial(_ln1_route_kernel, tm=tm, n_tiles=n_tiles, cap=cap)
    tile = lambda i: jnp.minimum(i, n_tiles - 1)
    row = lambda i: (tile(i), 0)
    fixed = lambda i: (0, 0)
    return pl.pallas_call(
        kern,
        grid=(n_tiles + 1,),
        in_specs=[pl.BlockSpec((tm, D_MODEL), row),
                  pl.BlockSpec((D_MODEL, D_MODEL), fixed),
                  pl.BlockSpec((tm, D_MODEL), row),
                  pl.BlockSpec((1, D_MODEL), fixed),
                  pl.BlockSpec((1, D_MODEL), fixed),
                  pl.BlockSpec((N_EXPERTS, D_MODEL), fixed),
                  pl.BlockSpec((N_EXPERTS, 1), fixed)],
        out_specs=[pl.BlockSpec((tm, D_MODEL), row),
                   pl.BlockSpec((1, TOP_K, tm), lambda i: (tile(i), 0, 0)),
                   pl.BlockSpec((TOP_K, tm), lambda i: (0, tile(i))),
                   pl.BlockSpec((N_EXPERTS, LANES), fixed),
                   pl.BlockSpec(memory_space=pl.ANY)],
        out_shape=[jax.ShapeDtypeStruct((T, D_MODEL), F32),
                   jax.ShapeDtypeStruct((n_tiles, TOP_K, tm), I32),
                   jax.ShapeDtypeStruct((TOP_K, T), F32),
                   jax.ShapeDtypeStruct((N_EXPERTS, LANES), F32),
                   jax.ShapeDtypeStruct((N_EXPERTS * cap + SUBLANES, HALF), U32)],
        scratch_shapes=[pltpu.VMEM((N_EXPERTS, LANES), F32), pltpu.VMEM((N_EXPERTS, LANES), F32),
                        pltpu.VMEM((2, tm, HALF), U32),
                        pltpu.VMEM((TOP_K, tm), I32),
                        pltpu.SMEM((2, TOP_K, tm), I32),
                        pltpu.SemaphoreType.DMA, pltpu.SemaphoreType.DMA],
        compiler_params=_params(("arbitrary",)),
        name="ln1_route",
    )(mixed, w_o, x2, ln_g.reshape(1, D_MODEL), ln_b.reshape(1, D_MODEL), w_router_t,
      router_bias.reshape(N_EXPERTS, 1))


def _row_copy(src_ref, src_row, dst_ref, dst_row, sem):
    return pltpu.make_async_copy(src_ref.at[pl.ds(src_row, 1)], dst_ref.at[pl.ds(dst_row, 1)], sem)


def _pad_fill_kernel(lo_ref, hi_ref, xs_in_ref, xs_ref, zero_ref, sem):
    del xs_in_ref
    zero_ref[...] = jnp.zeros_like(zero_ref)

    def group_copy(r):
        return pltpu.make_async_copy(zero_ref, xs_ref.at[pl.ds(pl.multiple_of(r, SUBLANES), SUBLANES)], sem)

    def fill_expert(e, counts):
        n_rows, n_groups = counts
        lo = lo_ref[e]
        hi = hi_ref[e]
        mid = jnp.minimum((lo + SUBLANES - 1) // SUBLANES * SUBLANES, hi)
        groups = (hi - mid) // SUBLANES

        def fill_row(r, c):
            _row_copy(zero_ref, 0, xs_ref, r, sem).start()
            return c

        def fill_group(g, c):
            group_copy(mid + g * SUBLANES).start()
            return c

        lax.fori_loop(lo, mid, fill_row, 0)
        lax.fori_loop(0, groups, fill_group, 0)
        return n_rows + (mid - lo), n_groups + groups

    n_rows, n_groups = lax.fori_loop(0, N_EXPERTS, fill_expert, (0, 0))

    def drain_row(r, c):
        _row_copy(zero_ref, 0, xs_ref, 0, sem).wait()
        return c

    def drain_group(g, c):
        group_copy(0).wait()
        return c

    lax.fori_loop(0, n_rows, drain_row, 0)
    lax.fori_loop(0, n_groups, drain_group, 0)


def _pad_fill(fill_lo, fill_hi, xs):
    smem = pl.BlockSpec(memory_space=pltpu.SMEM)
    return pl.pallas_call(
        _pad_fill_kernel,
        in_specs=[smem, smem, pl.BlockSpec(memory_space=pl.ANY)],
        out_specs=pl.BlockSpec(memory_space=pl.ANY),
        out_shape=jax.ShapeDtypeStruct(xs.shape, xs.dtype),
        scratch_shapes=[pltpu.VMEM((SUBLANES, HALF), U32), pltpu.SemaphoreType.DMA],
        input_output_aliases={2: 0},
        name="moe_pad_fill",
    )(fill_lo, fill_hi, xs)


def _experts_kernel(be_ref, nb_ref, brow_ref, next_ref, xs_ref, wg_hbm, wu_hbm, wd_hbm, ys_ref,
                    wgb_ref, wub_ref, wdb_ref, wg_f, wu_f, wd_f, sem, slot_ref):
    del brow_ref
    b = pl.program_id(0)
    new_expert = jnp.logical_or(b == 0, be_ref[b] != be_ref[jnp.maximum(b - 1, 0)])

    def weight_copies(e, s):
        return (pltpu.make_async_copy(wg_hbm.at[e], wg_f.at[s], sem.at[s]),
                pltpu.make_async_copy(wu_hbm.at[e], wu_f.at[s], sem.at[s]),
                pltpu.make_async_copy(wd_hbm.at[e], wd_f.at[s], sem.at[s]))

    @pl.when(b == 0)
    def _():
        slot_ref[0] = 0
        for c in weight_copies(be_ref[0], 0):
            c.start()

    @pl.when(jnp.logical_and(b < nb_ref[0], new_expert))
    def _():
        s = slot_ref[0]
        for c in weight_copies(be_ref[b], s):
            c.wait()

        @pl.when(next_ref[b] != be_ref[b])
        def _():
            for c in weight_copies(next_ref[b], 1 - s):
                c.start()

        wgb_ref[...] = wg_f[s].astype(BF16)
        wub_ref[...] = wu_f[s].astype(BF16)
        wdb_ref[...] = wd_f[s].astype(BF16)
        slot_ref[0] = 1 - s

    @pl.when(b < nb_ref[0])
    def _():
        lo, hi = _unpack_bf16_pairs(xs_ref[...])
        gate = _dot(lo, wgb_ref[:HALF, :]) + _dot(hi, wgb_ref[HALF:, :])
        up = _dot(lo, wub_ref[:HALF, :]) + _dot(hi, wub_ref[HALF:, :])
        hb = (jax.nn.silu(gate) * up).astype(BF16)
        ys_ref[...] = _pack_bf16_pairs(_dot(hb, wdb_ref[...]).astype(BF16))


def _experts(block_e, nb_used, block_row, next_e, xs, wg, wu, wd):
    n_rows = xs.shape[0] // MOE_ROWS * MOE_ROWS
    nb = block_e.shape[0]
    rows = lambda b, be, nbu, brow, nxt: (brow[b], 0)
    hbm = pl.BlockSpec(memory_space=pl.ANY)
    grid_spec = pltpu.PrefetchScalarGridSpec(
        num_scalar_prefetch=4,
        grid=(nb,),
        in_specs=[pl.BlockSpec((MOE_ROWS, HALF), rows), hbm, hbm, hbm],
        out_specs=pl.BlockSpec((MOE_ROWS, HALF), rows),
        scratch_shapes=[pltpu.VMEM((D_MODEL, D_EXPERT), BF16), pltpu.VMEM((D_MODEL, D_EXPERT), BF16),
                        pltpu.VMEM((D_EXPERT, D_MODEL), BF16),
                        pltpu.VMEM((2, D_MODEL, D_EXPERT), F32), pltpu.VMEM((2, D_MODEL, D_EXPERT), F32),
                        pltpu.VMEM((2, D_EXPERT, D_MODEL), F32),
                        pltpu.SemaphoreType.DMA((2,)), pltpu.SMEM((1,), I32)],
    )
    return pl.pallas_call(
        _experts_kernel,
        grid_spec=grid_spec,
        out_shape=jax.ShapeDtypeStruct((n_rows, HALF), U32),
        compiler_params=_params(("arbitrary",)),
        name="moe_experts",
    )(block_e, nb_used, block_row, next_e, xs, wg, wu, wd)


def _combine_kernel(dest_ref, next_ref, w_ref, h_ref, wgs_ref, wus_ref, wds_ref, g_ref, b_ref,
                    ys_ref, o_ref, buf_ref, sem, *, tm):
    i = pl.program_id(0)
    slot = i % 2

    def gather(idx_ref, dst_slot, t):
        for k in range(TOP_K):
            _row_copy(ys_ref, idx_ref[0, k, t], buf_ref.at[dst_slot, k], t, sem.at[dst_slot]).start()

    def wait_tile(s):
        pltpu.make_async_copy(buf_ref.at[s], buf_ref.at[s], sem.at[s]).wait()

    @pl.when(i == 0)
    def _():
        def first(t, c):
            gather(dest_ref, 0, t)
            return c

        lax.fori_loop(0, tm, first, 0)

    wait_tile(slot)
    for t in range(tm):
        gather(next_ref, 1 - slot, t)

    h = h_ref[...]
    hb = h.astype(BF16)
    hid = (jax.nn.silu(_dot(hb, wgs_ref[...])) * _dot(hb, wus_ref[...])).astype(BF16)
    shared = _dot(hid, wds_ref[...])

    w = w_ref[...]
    lo_sum = jnp.zeros((tm, HALF), F32)
    hi_sum = jnp.zeros((tm, HALF), F32)
    for k in range(TOP_K):
        p = buf_ref[slot, k]
        lo_sum = lo_sum + lax.bitcast_convert_type(p << 16, F32) * w[:, k:k + 1]
        hi_sum = hi_sum + lax.bitcast_convert_type(p & jnp.uint32(0xFFFF0000), F32) * w[:, k:k + 1]
    routed = jnp.concatenate([lo_sum, hi_sum], axis=1)
    o_ref[...] = _layer_norm(DN_ALPHA * h + (routed + shared), g_ref[...], b_ref[...])

    @pl.when(i == pl.num_programs(0) - 1)
    def _():
        wait_tile(1 - slot)


def _combine(dest3, w_tok, h, wgs, wus, wds, ln_g, ln_b, ys3):
    T = h.shape[0]
    n_tiles, _, tm = dest3.shape
    kern = functools.partial(_combine_kernel, tm=tm)
    row = lambda i: (i, 0)
    fixed = lambda i: (0, 0)
    return pl.pallas_call(
        kern,
        grid=(n_tiles,),
        in_specs=[pl.BlockSpec((1, TOP_K, tm), lambda i: (i, 0, 0), memory_space=pltpu.SMEM),
                  pl.BlockSpec((1, TOP_K, tm), lambda i: (jnp.minimum(i + 1, n_tiles - 1), 0, 0),
                               memory_space=pltpu.SMEM),
                  pl.BlockSpec((tm, TOP_K), row),
                  pl.BlockSpec((tm, D_MODEL), row),
                  pl.BlockSpec((D_MODEL, D_SHARED), fixed),
                  pl.BlockSpec((D_MODEL, D_SHARED), fixed),
                  pl.BlockSpec((D_SHARED, D_MODEL), fixed),
                  pl.BlockSpec((1, D_MODEL), fixed),
                  pl.BlockSpec((1, D_MODEL), fixed),
                  pl.BlockSpec(memory_space=pl.ANY)],
        out_specs=pl.BlockSpec((tm, D_MODEL), row),
        out_shape=jax.ShapeDtypeStruct((T, D_MODEL), F32),
        scratch_shapes=[pltpu.VMEM((2, TOP_K, tm, HALF), U32), pltpu.SemaphoreType.DMA((2,))],
        compiler_params=_params(("arbitrary",)),
        name="moe_combine",
    )(dest3, dest3, w_tok, h, wgs, wus, wds, ln_g.reshape(1, D_MODEL), ln_b.reshape(1, D_MODEL), ys3)


def _tile(n, pref):
    t = min(n, pref)
    assert n % t == 0, (n, pref)
    return t


def _layer(x, w_in, conv_w, conv_b, lru_wa, lru_ba, lru_wi, lru_bi, lru_lambda, ret_gn_gain,
           w_lru_out, w_ret_out, b_gate, w_o, ln1_g, ln1_b, w_router, router_bias,
           w_gate_e, w_up_e, w_down_e, w_gate_s, w_up_s, w_down_s, ln2_g, ln2_b):
    B, S, D = x.shape
    T = B * S
    x2 = x.reshape(T, D)
    xb = x2.astype(BF16)

    half = RET_DK // 2
    freq = ROPE_THETA ** (-jnp.arange(half, dtype=F32) / half)
    ang = jnp.arange(S, dtype=I32).astype(F32)[:, None] * freq
    cos, sin = jnp.cos(ang), jnp.sin(ang)

    ua, v, q, k = _lru_branch(xb, w_in, cos, sin, conv_w, conv_b, lru_wa, lru_ba, lru_wi, lru_bi,
                              lru_lambda, S, _tile(S, 256), 512)
    ub = _retention(q, k, v, xb, w_in, ret_gn_gain, B, S)

    mixed = _mix(ua, ub, xb, w_lru_out.astype(BF16), w_ret_out.astype(BF16),
                 w_in[:, OFF_GA:].astype(BF16), b_gate, _tile(T, 512), 512)

    tm_r = _tile(T, 256)
    cap = (T + MOE_ROWS - 1) // MOE_ROWS * MOE_ROWS
    h, dest, wsel, cnt, xs = _ln1_route(mixed, w_o.astype(BF16), x2, ln1_g, ln1_b,
                                        w_router.T.astype(BF16), router_bias, tm_r, cap)

    counts = cnt[:, 0].astype(I32)
    e_blocks = (counts + MOE_ROWS - 1) // MOE_ROWS
    blocks_end = jnp.cumsum(e_blocks)
    nb = (T * TOP_K + N_EXPERTS * (MOE_ROWS - 1) + MOE_ROWS - 1) // MOE_ROWS
    nb_used = blocks_end[-1:]
    blk = jnp.minimum(jnp.arange(nb, dtype=I32), nb_used - 1)
    block_e = jnp.minimum(jnp.sum((blocks_end[None, :] <= blk[:, None]).astype(I32), axis=1),
                          N_EXPERTS - 1)
    blocks_start = blocks_end - e_blocks
    e_onehot = block_e[:, None] == jnp.arange(N_EXPERTS, dtype=I32)[None, :]
    block_row = block_e * (cap // MOE_ROWS) + blk - jnp.sum(jnp.where(e_onehot, blocks_start[None, :], 0),
                                                            axis=1)
    e_row0 = jnp.arange(N_EXPERTS, dtype=I32) * cap
    seg_end = jnp.sum(jnp.where(e_onehot, blocks_end[None, :], 0), axis=1)
    after = jnp.minimum(seg_end, nb_used - 1)
    next_e = jnp.sum(jnp.where(after[:, None] == jnp.arange(nb, dtype=I32)[None, :], block_e[None, :], 0),
                     axis=1)

    xs = _pad_fill(e_row0 + counts, e_row0 + e_blocks * MOE_ROWS, xs)
    ys = _experts(block_e, nb_used, block_row, next_e, xs, w_gate_e, w_up_e, w_down_e)
    out = _combine(dest, wsel.T, h, w_gate_s.astype(BF16), w_up_s.astype(BF16),
                   w_down_s.astype(BF16), ln2_g, ln2_b, ys)
    return out.reshape(B, S, D)


def kernel(x, w_in, conv_w, conv_b, lru_wa, lru_ba, lru_wi, lru_bi, lru_lambda, ret_gn_gain, w_lru_out, w_ret_out, b_gate, w_o, ln1_g, ln1_b, w_router, router_bias, w_gate_e, w_up_e, w_down_e, w_gate_s, w_up_s, w_down_s, ln2_g, ln2_b):
    assert DEPTH == 1 and w_in.shape[0] == DEPTH
    args = (w_in, conv_w, conv_b, lru_wa, lru_ba, lru_wi, lru_bi, lru_lambda, ret_gn_gain,
            w_lru_out, w_ret_out, b_gate, w_o, ln1_g, ln1_b, w_router, router_bias,
            w_gate_e, w_up_e, w_down_e, w_gate_s, w_up_s, w_down_s, ln2_g, ln2_b)
    return _layer(x, *[a[0] for a in args])
```

```python
import functools

import jax
import jax.numpy as jnp
from jax import lax
from jax.experimental import pallas as pl
from jax.experimental.pallas import tpu as pltpu

F32 = jnp.float32
BF16 = jnp.bfloat16
I32 = jnp.int32
U32 = jnp.uint32

D_MODEL = 2048
D_LRU = 2048
LRU_HEADS = 16
LRU_BLOCK = D_LRU // LRU_HEADS
CONV_WIDTH = 4
LRU_C = 8.0
RET_HEADS = 8
RET_DK = 256
RET_DV = 512
D_QK = RET_HEADS * RET_DK
D_RV = RET_HEADS * RET_DV
ROPE_THETA = 10000.0
N_EXPERTS = 64
TOP_K = 8
N_GROUPS = 8
GROUP_SIZE = N_EXPERTS // N_GROUPS
TOPK_GROUPS = 4
D_EXPERT = 512
D_SHARED = 512
ROUTED_SCALE = 2.5
DEPTH = 1
DN_ALPHA = (2.0 * DEPTH) ** 0.25
LN_EPS = 1e-5

OFF_LRU_X = 0
OFF_LRU_Y = OFF_LRU_X + D_LRU
OFF_Q = OFF_LRU_Y + D_LRU
OFF_K = OFF_Q + D_QK
OFF_V = OFF_K + D_QK
OFF_G = OFF_V + D_RV
OFF_GA = OFF_G + D_RV
OFF_GB = OFF_GA + D_MODEL

V7X_VMEM_LIMIT = 56 * 1024 * 1024
LANES = 128
SUBLANES = 8
SUBLANE_BITS = SUBLANES.bit_length() - 1
MOE_ROWS = 512
RET_CHUNK = 256
LRU_SUB = 2
HALF = D_MODEL // 2


def _params(sem):
    return pltpu.CompilerParams(dimension_semantics=sem, vmem_limit_bytes=V7X_VMEM_LIMIT)


def _dot(a, b):
    return jnp.dot(a, b, preferred_element_type=F32)


def _lru_kernel(x_ref, wx_ref, wy_ref, wv_ref, wq_ref, wk_ref, cos_ref, sin_ref, cw_ref, cb_ref,
                wai_ref, ba_ref, bi_ref, lam_ref,
                o_ref, v_ref, q_ref, k_ref,
                wxb_ref, wyb_ref, wvb_ref, wqb_ref, wkb_ref, perm_ref, unperm_ref, tail_ref, hc_ref,
                a_ref, b_ref, gy_ref, *, tiles_per_seq, tm, width):
    i = pl.program_id(1)
    seg = tm // SUBLANES
    halo = (CONV_WIDTH - 1) * SUBLANES

    @pl.when(i == 0)
    def _():
        wxb_ref[...] = wx_ref[...].astype(BF16)
        wyb_ref[...] = wy_ref[...].astype(BF16)
        wvb_ref[...] = wv_ref[...].astype(BF16)
        wqb_ref[...] = wq_ref[...].astype(BF16)
        wkb_ref[...] = wk_ref[...].astype(BF16)
        r = lax.broadcasted_iota(I32, (tm, tm), 0)
        c = lax.broadcasted_iota(I32, (tm, tm), 1)
        r_time = (r & (SUBLANES - 1)) * seg + (r >> SUBLANE_BITS)
        c_time = (c & (SUBLANES - 1)) * seg + (c >> SUBLANE_BITS)
        perm_ref[...] = jnp.where(c == r_time, 1.0, 0.0).astype(BF16)
        unperm_ref[...] = jnp.where(r == c_time, 1.0, 0.0).astype(BF16)
        tail_ref[...] = jnp.zeros_like(tail_ref)
        hc_ref[...] = jnp.zeros_like(hc_ref)

    cw = cw_ref[...]
    cb = cb_ref[...]
    taps = [cw[CONV_WIDTH - 1 - d:CONV_WIDTH - d] for d in range(CONV_WIDTH)]
    sub = lax.broadcasted_iota(I32, (SUBLANES, width), 0)
    row = lax.broadcasted_iota(I32, (tm, width), 0)
    v_piece = v_ref.shape[1] // 4

    def one_tile(u, tail, carry):
        rows_u = slice(u * tm, (u + 1) * tm)
        first = ((i * LRU_SUB + u) % tiles_per_seq) == 0
        tail = jnp.where(first, 0.0, tail)
        carry = jnp.where(first, 0.0, carry)
        x = x_ref[rows_u, :]

        def emit_v(piece):
            cols = slice(piece * v_piece, (piece + 1) * v_piece)
            v_ref[rows_u, cols] = _dot(x, wvb_ref[:, cols]).astype(v_ref.dtype)

        def emit_rotary(wb_ref, out_ref, scale):
            acc = _dot(x, wb_ref[...])
            cos = cos_ref[rows_u, :]
            sin = sin_ref[rows_u, :]
            half = RET_DK // 2
            for hh in range(width // RET_DK):
                c0 = hh * RET_DK
                t1 = acc[:, c0:c0 + half]
                t2 = acc[:, c0 + half:c0 + RET_DK]
                out_ref[rows_u, c0:c0 + half] = ((t1 * cos - t2 * sin) * scale).astype(out_ref.dtype)
                out_ref[rows_u, c0 + half:c0 + RET_DK] = (
                    (t1 * sin + t2 * cos) * scale).astype(out_ref.dtype)

        emit_v(0)
        xp = _dot(perm_ref[...], x).astype(BF16)
        px = _dot(xp, wxb_ref[...])
        gy_ref[u] = jax.nn.gelu(_dot(xp, wyb_ref[...]), approximate=True)

        xa = cb + taps[0] * px
        for d in range(1, CONV_WIDTH):
            back = jnp.concatenate([px[tm - d * SUBLANES:], px[:tm - d * SUBLANES]], axis=0)
            xa = xa + taps[d] * back
        groups = {}
        for g in range(CONV_WIDTH - 1):
            rows = slice(g * SUBLANES, (g + 1) * SUBLANES)
            cur_end = px[tm - halo + g * SUBLANES:tm - halo + (g + 1) * SUBLANES]
            groups[g - (CONV_WIDTH - 1)] = jnp.where(sub == 0, pltpu.roll(tail[rows, :], 1, 0),
                                                     pltpu.roll(cur_end, 1, 0))
            groups[g] = px[rows, :]
        head = []
        for q in range(CONV_WIDTH - 1):
            acc = cb + taps[0] * groups[q]
            for d in range(1, CONV_WIDTH):
                acc = acc + taps[d] * groups[q - d]
            head.append(acc)
        xa = jnp.concatenate(head + [xa[halo:]], axis=0)
        tail_out = px[tm - halo:]
        emit_v(1)

        xab = xa.astype(BF16)
        r_parts, i_parts = [], []
        for hh in range(width // LRU_BLOCK):
            both = _dot(xab[:, hh * LRU_BLOCK:(hh + 1) * LRU_BLOCK], wai_ref[hh])
            r_parts.append(both[:, :LRU_BLOCK])
            i_parts.append(both[:, LRU_BLOCK:])
        rg = jax.nn.sigmoid(jnp.concatenate(r_parts, axis=1) + ba_ref[...])
        ig = jax.nn.sigmoid(jnp.concatenate(i_parts, axis=1) + bi_ref[...])
        emit_rotary(wqb_ref, q_ref, 1.0)

        log_a = -LRU_C * rg * jax.nn.softplus(-lam_ref[...])
        a = jnp.exp(log_a)
        mult = jnp.sqrt(-jnp.tanh(log_a) * (a * a + 1.0))
        mult = jnp.where(jnp.logical_and(first, row == 0), 1.0, mult)
        a_ref[u] = a
        b_ref[u] = mult * ig * xa
        emit_v(2)

        run_a = jnp.ones((SUBLANES, width), F32)
        run_h = jnp.zeros((SUBLANES, width), F32)
        for q in range(seg):
            rows = slice(q * SUBLANES, (q + 1) * SUBLANES)
            aq = a_ref[u, rows, :]
            run_h = aq * run_h + b_ref[u, rows, :]
            run_a = aq * run_a
            b_ref[u, rows, :] = run_h
            a_ref[u, rows, :] = run_a
        emit_v(3)

        carries = []
        for s in range(SUBLANES):
            carries.append(carry)
            carry = run_a[s:s + 1, :] * carry + run_h[s:s + 1, :]
        carry_in = jnp.concatenate(carries, axis=0)
        for q in range(seg):
            rows = slice(q * SUBLANES, (q + 1) * SUBLANES)
            b_ref[u, rows, :] = (b_ref[u, rows, :] + a_ref[u, rows, :] * carry_in) * gy_ref[u, rows, :]

        emit_rotary(wkb_ref, k_ref, RET_DK ** -0.5)
        o_ref[rows_u, :] = _dot(unperm_ref[...], b_ref[u].astype(BF16)).astype(o_ref.dtype)
        return tail_out, carry

    tail = tail_ref[...]
    carry = hc_ref[...]
    for u in range(LRU_SUB):
        tail, carry = one_tile(u, tail, carry)
    tail_ref[...] = tail
    hc_ref[...] = carry


def _lru_branch(xb, w, cos, sin, conv_w, conv_b, wa, ba, wi, bi, lam, seq, tm, width):
    T, K = xb.shape
    nj = D_LRU // width
    assert D_QK == D_LRU, "q / k column tiles are walked together with the lru tiles"
    jy0 = OFF_LRU_Y // width
    jq0 = OFF_Q // width
    jk0 = OFF_K // width
    hp = width // LRU_BLOCK
    vw = D_RV // nj
    jv0 = OFF_V // vw
    tiles_per_seq = seq // tm
    rows = LRU_SUB * tm
    reps = max(1, rows // seq)
    cos, sin = jnp.tile(cos, (reps, 1)), jnp.tile(sin, (reps, 1))
    table_blocks = cos.shape[0] // rows
    kern = functools.partial(_lru_kernel, tiles_per_seq=tiles_per_seq, tm=tm, width=width)
    vec = lambda j, i: (0, j)
    once = pl.Buffered(1)
    table = pl.BlockSpec((rows, RET_DK // 2), lambda j, i: (i % table_blocks, 0))
    return pl.pallas_call(
        kern,
        grid=(nj, T // rows),
        in_specs=[pl.BlockSpec((rows, K), lambda j, i: (i, 0)),
                  pl.BlockSpec((K, width), lambda j, i: (0, j), pipeline_mode=once),
                  pl.BlockSpec((K, width), lambda j, i: (0, jy0 + j), pipeline_mode=once),
                  pl.BlockSpec((K, vw), lambda j, i: (0, jv0 + j), pipeline_mode=once),
                  pl.BlockSpec((K, width), lambda j, i: (0, jq0 + j), pipeline_mode=once),
                  pl.BlockSpec((K, width), lambda j, i: (0, jk0 + j), pipeline_mode=once),
                  table, table,
                  pl.BlockSpec((CONV_WIDTH, width), vec),
                  pl.BlockSpec((1, width), vec),
                  pl.BlockSpec((hp, LRU_BLOCK, 2 * LRU_BLOCK), lambda j, i: (j, 0, 0)),
                  pl.BlockSpec((1, width), vec),
                  pl.BlockSpec((1, width), vec),
                  pl.BlockSpec((1, width), vec)],
        out_specs=[pl.BlockSpec((rows, width), lambda j, i: (i, j)),
                   pl.BlockSpec((rows, vw), lambda j, i: (i, j)),
                   pl.BlockSpec((rows, width), lambda j, i: (i, j)),
                   pl.BlockSpec((rows, width), lambda j, i: (i, j))],
        out_shape=[jax.ShapeDtypeStruct((T, D_LRU), BF16), jax.ShapeDtypeStruct((T, D_RV), BF16),
                   jax.ShapeDtypeStruct((T, D_QK), BF16), jax.ShapeDtypeStruct((T, D_QK), BF16)],
        scratch_shapes=[pltpu.VMEM((K, width), BF16), pltpu.VMEM((K, width), BF16),
                        pltpu.VMEM((K, vw), BF16),
                        pltpu.VMEM((K, width), BF16), pltpu.VMEM((K, width), BF16),
                        pltpu.VMEM((tm, tm), BF16), pltpu.VMEM((tm, tm), BF16),
                        pltpu.VMEM(((CONV_WIDTH - 1) * SUBLANES, width), F32),
                        pltpu.VMEM((1, width), F32),
                        pltpu.VMEM((LRU_SUB, tm, width), F32), pltpu.VMEM((LRU_SUB, tm, width), F32),
                        pltpu.VMEM((LRU_SUB, tm, width), F32)],
        compiler_params=_params(("parallel", "arbitrary")),
        name="lru_branch",
    )(xb, w, w, w, w, w, cos, sin, conv_w, conv_b.reshape(1, D_LRU),
      jnp.concatenate([wa, wi], axis=-1).astype(BF16), ba.reshape(1, D_LRU), bi.reshape(1, D_LRU),
      lam.reshape(1, D_LRU))


def _ret_kernel(gc_ref, q_ref, k_ref, v_ref, x_ref, w_hbm, gain_ref, dm_ref, xi_ref, zeta_ref, o_ref,
                st_ref, wg_ref, stage_ref, wg_sem):
    @pl.when(jnp.logical_and(pl.program_id(0) == 0, pl.program_id(1) == 0))
    def _():
        for hd in range(RET_HEADS):
            cols = pl.ds(OFF_G + hd * RET_DV, RET_DV)
            fetch = pltpu.make_async_copy(w_hbm.at[:, cols], stage_ref, wg_sem)
            fetch.start()
            fetch.wait()
            wg_ref[:, hd * RET_DV:(hd + 1) * RET_DV] = stage_ref[...].astype(BF16)

    @pl.when(pl.program_id(1) == 0)
    def _():
        st_ref[...] = jnp.zeros_like(st_ref)

    x = x_ref[...]
    for hd in range(RET_HEADS):
        qc = slice(hd * RET_DK, (hd + 1) * RET_DK)
        vc = slice(hd * RET_DV, (hd + 1) * RET_DV)
        q = q_ref[:, qc]
        k = k_ref[:, qc]
        v = v_ref[:, vc]
        st = st_ref[hd]
        s = lax.dot_general(q, k, (((1,), (1,)), ((), ())), preferred_element_type=F32) * dm_ref[hd]
        o = _dot(s.astype(BF16), v) + _dot(q, st.astype(BF16)) * xi_ref[hd]
        kz = (k.astype(F32) * zeta_ref[hd]).astype(BF16)
        st_ref[hd] = gc_ref[hd] * st + lax.dot_general(kz, v, (((0,), (0,)), ((), ())),
                                                       preferred_element_type=F32)
        mu = jnp.mean(o, axis=-1, keepdims=True)
        oc = o - mu
        var = jnp.mean(oc * oc, axis=-1, keepdims=True)
        oh = oc * lax.rsqrt(var + LN_EPS) * gain_ref[:, vc]
        g = _dot(x, wg_ref[:, vc])
        o_ref[:, vc] = (jax.nn.silu(g) * oh).astype(o_ref.dtype)


def _retention(q, k, v, xb, w_in, gain, batch, seq):
    T = batch * seq
    C = min(RET_CHUNK, seq)
    nchunk = seq // C
    H = RET_HEADS
    log_g = jnp.log1p(-jnp.exp2(-5.0 - jnp.arange(H, dtype=F32)))
    idx = jnp.arange(C, dtype=F32)
    diff = idx[:, None] - idx[None, :]
    dmask = jnp.where(diff >= 0, jnp.exp(jnp.maximum(diff, 0.0)[None] * log_g[:, None, None]), 0.0)
    xi = jnp.exp((idx[None] + 1.0) * log_g[:, None])[:, :, None]
    zeta = jnp.exp((C - 1.0 - idx[None]) * log_g[:, None])[:, :, None]
    g_c = jnp.exp(C * log_g)
    rows = lambda b, n: b * nchunk + n
    whole3 = lambda b, n: (0, 0, 0)
    return pl.pallas_call(
        _ret_kernel,
        grid=(batch, nchunk),
        in_specs=[pl.BlockSpec(memory_space=pltpu.SMEM),
                  pl.BlockSpec((C, D_QK), lambda b, n: (rows(b, n), 0)),
                  pl.BlockSpec((C, D_QK), lambda b, n: (rows(b, n), 0)),
                  pl.BlockSpec((C, D_RV), lambda b, n: (rows(b, n), 0)),
                  pl.BlockSpec((C, D_MODEL), lambda b, n: (rows(b, n), 0)),
                  pl.BlockSpec(memory_space=pl.ANY),
                  pl.BlockSpec((1, D_RV), lambda b, n: (0, 0)),
                  pl.BlockSpec((H, C, C), whole3),
                  pl.BlockSpec((H, C, 1), whole3),
                  pl.BlockSpec((H, C, 1), whole3)],
        out_specs=pl.BlockSpec((C, D_RV), lambda b, n: (rows(b, n), 0)),
        out_shape=jax.ShapeDtypeStruct((T, D_RV), BF16),
        scratch_shapes=[pltpu.VMEM((H, RET_DK, RET_DV), F32), pltpu.VMEM((D_MODEL, D_RV), BF16),
                        pltpu.VMEM((D_MODEL, RET_DV), F32), pltpu.SemaphoreType.DMA],
        compiler_params=_params(("arbitrary", "arbitrary")),
        name="retention",
    )(g_c, q, k, v, xb, w_in, gain.reshape(1, D_RV), dmask, xi, zeta)


def _mix_kernel(ua_ref, ub_ref, x_ref, wlo_ref, wro_ref, wga_ref, wgb_ref, bga_ref, bgb_ref, o_ref):
    x = x_ref[...]
    ya = _dot(ua_ref[...], wlo_ref[...])
    yb = _dot(ub_ref[...], wro_ref[...])
    ga = jax.nn.sigmoid(_dot(x, wga_ref[...]) + bga_ref[...])
    gb = jax.nn.sigmoid(_dot(x, wgb_ref[...]) + bgb_ref[...])
    o_ref[...] = (ga * ya + gb * yb).astype(o_ref.dtype)


def _mix(ua, ub, xb, w_lru_out, w_ret_out, w_gates, b_gate, tm, tn):
    T = xb.shape[0]
    nj = D_MODEL // tn
    return pl.pallas_call(
        _mix_kernel,
        grid=(T // tm, nj),
        in_specs=[pl.BlockSpec((tm, D_LRU), lambda i, j: (i, 0)),
                  pl.BlockSpec((tm, D_RV), lambda i, j: (i, 0)),
                  pl.BlockSpec((tm, D_MODEL), lambda i, j: (i, 0)),
                  pl.BlockSpec((D_LRU, tn), lambda i, j: (0, j)),
                  pl.BlockSpec((D_RV, tn), lambda i, j: (0, j)),
                  pl.BlockSpec((D_MODEL, tn), lambda i, j: (0, j)),
                  pl.BlockSpec((D_MODEL, tn), lambda i, j: (0, nj + j)),
                  pl.BlockSpec((1, tn), lambda i, j: (0, j)),
                  pl.BlockSpec((1, tn), lambda i, j: (0, nj + j))],
        out_specs=pl.BlockSpec((tm, tn), lambda i, j: (i, j)),
        out_shape=jax.ShapeDtypeStruct((T, D_MODEL), BF16),
        compiler_params=_params(("parallel", "parallel")),
        name="gated_mix",
    )(ua, ub, xb, w_lru_out, w_ret_out, w_gates, w_gates, b_gate.reshape(1, 2 * D_MODEL),
      b_gate.reshape(1, 2 * D_MODEL))


def _layer_norm(y, g, b):
    mu = jnp.mean(y, axis=-1, keepdims=True)
    yc = y - mu
    var = jnp.mean(yc * yc, axis=-1, keepdims=True)
    return yc * lax.rsqrt(var + LN_EPS) * g + b


def _pack_bf16_pairs(hb):
    lo = lax.bitcast_convert_type(hb[:, :HALF].astype(F32), U32)
    hi = lax.bitcast_convert_type(hb[:, HALF:].astype(F32), U32)
    return (hi & jnp.uint32(0xFFFF0000)) | (lo >> 16)


def _unpack_bf16_pairs(p):
    lo = lax.bitcast_convert_type(p << 16, F32).astype(BF16)
    hi = lax.bitcast_convert_type(p & jnp.uint32(0xFFFF0000), F32).astype(BF16)
    return lo, hi


def _max2(a):
    return jnp.max(jnp.max(a, axis=1, keepdims=True), axis=0, keepdims=True)


def _min2(a):
    return jnp.min(jnp.min(a, axis=1, keepdims=True), axis=0, keepdims=True)


def _sum2(a):
    return jnp.sum(jnp.sum(a, axis=1, keepdims=True), axis=0, keepdims=True)


def _ln1_route_kernel(mix_ref, wo_ref, x_ref, g_ref, b_ref, wr_ref, rb_ref,
                      h_ref, dest_ref, wsel_ref, cnt_ref, xs_ref,
                      cnt_sc, base_sc, hp_sc, dest_v, dest_s, row_sem, idx_sem, *, tm, n_tiles, cap):
    i = pl.program_id(0)
    slot = i % 2
    prev = 1 - slot

    def wait_rows(s):
        for _ in range(TOP_K):
            pltpu.make_async_copy(hp_sc.at[s], hp_sc.at[s], row_sem).wait()

    @pl.when(i == 0)
    def _():
        cnt_sc[...] = jnp.zeros_like(cnt_sc)
        base_sc[...] = jnp.zeros_like(base_sc)
        hp_sc[1] = jnp.zeros((tm, HALF), U32)

        def spare(t, c):
            for k in range(TOP_K):
                dest_s[1, k, t] = N_EXPERTS * cap + k
            return c

        lax.fori_loop(0, tm, spare, 0)

    @pl.when(i > 0)
    def _():
        wait_rows(slot)

    for t in range(tm):
        for k in range(TOP_K):
            _row_copy(hp_sc.at[prev], t, xs_ref, dest_s[prev, k, t], row_sem).start()

    y = DN_ALPHA * x_ref[...] + _dot(mix_ref[...], wo_ref[...])
    h = _layer_norm(y, g_ref[...], b_ref[...])
    h_ref[...] = h
    hb = h.astype(BF16)
    hp_sc[slot] = _pack_bf16_pairs(hb)

    G, GS = N_GROUPS, GROUP_SIZE
    logits = lax.dot_general(wr_ref[...], hb, (((1,), (1,)), ((), ())), preferred_element_type=F32)
    scores = jax.nn.sigmoid(logits)
    s3 = scores.reshape(G, GS, tm)
    b3 = (scores + rb_ref[...]).reshape(G, GS, tm)
    neg = jnp.float32(-jnp.inf)

    in_grp = lax.broadcasted_iota(I32, (G, GS, tm), 1)
    m1 = jnp.max(b3, axis=1, keepdims=True)
    f1 = jnp.min(jnp.where(b3 == m1, in_grp, GS), axis=1, keepdims=True)
    m2 = jnp.max(jnp.where(in_grp == f1, neg, b3), axis=1, keepdims=True)
    grp = m1 + m2

    gi = lax.broadcasted_iota(I32, (G, 1, tm), 0)
    gkeep = jnp.zeros((G, 1, tm), F32)
    cur = grp
    for _ in range(TOPK_GROUPS):
        m = jnp.max(cur, axis=0, keepdims=True)
        f = jnp.min(jnp.where(cur == m, gi, G), axis=0, keepdims=True)
        hit = gi == f
        gkeep = jnp.where(hit, 1.0, gkeep)
        cur = jnp.where(hit, neg, cur)
    emask = jnp.broadcast_to(gkeep, (G, GS, tm)) > 0.0

    ei = lax.broadcasted_iota(I32, (G, GS, tm), 0) * GS + in_grp
    cur = jnp.where(emask, b3, neg)
    hits, idxs, ws = [], [], []
    for _ in range(TOP_K):
        m = _max2(cur)
        f = _min2(jnp.where(cur == m, ei, N_EXPERTS))
        hit = ei == f
        cur = jnp.where(hit, neg, cur)
        hits.append(hit)
        idxs.append(f)
        ws.append(_sum2(jnp.where(hit, s3, 0.0)))
    wsum = ws[0]
    for r in range(1, TOP_K):
        wsum = wsum + ws[r]

    sel = jnp.zeros((G, GS, tm), F32)
    for hit in hits:
        sel = jnp.where(hit, 1.0, sel)
    sel2 = sel.reshape(N_EXPERTS, tm)
    tri = (lax.broadcasted_iota(I32, (tm, tm), 0) < lax.broadcasted_iota(I32, (tm, tm), 1))
    rank_local = _dot(sel2.astype(BF16), jnp.where(tri, 1.0, 0.0).astype(BF16))
    base = jnp.where(i < n_tiles, cnt_sc[...], base_sc[...])
    base_sc[...] = base
    rank3 = (rank_local + base[:, 0:1]).reshape(G, GS, tm)
    cnt_new = base + jnp.sum(sel2, axis=1, keepdims=True)
    cnt_sc[...] = cnt_new
    cnt_ref[...] = cnt_new

    for r in range(TOP_K):
        rank_r = _sum2(jnp.where(hits[r], rank3, 0.0)).reshape(1, tm).astype(I32)
        dest_v[r:r + 1, :] = idxs[r].reshape(1, tm) * cap + rank_r
        wsel_ref[r:r + 1, :] = (ws[r] / wsum * ROUTED_SCALE).reshape(1, tm)
    dest_ref[0] = dest_v[...]

    to_smem = pltpu.make_async_copy(dest_v, dest_s.at[slot], idx_sem)
    to_smem.start()
    to_smem.wait()

    @pl.when(i == n_tiles)
    def _():
        wait_rows(prev)


def _ln1_route(mixed, w_o, x2, ln_g, ln_b, w_router_t, router_bias, tm, cap):
    T = x2.shape[0]
    n_tiles = T // tm
    kern = functools.partial(_ln1_route_kernel, tm=tm, n_tiles=n_tiles, cap=cap)
    tile = lambda i: jnp.minimum(i, n_tiles - 1)
    row = lambda i: (tile(i), 0)
    fixed = lambda i: (0, 0)
    return pl.pallas_call(
        kern,
        grid=(n_tiles + 1,),
        in_specs=[pl.BlockSpec((tm, D_MODEL), row),
                  pl.BlockSpec((D_MODEL, D_MODEL), fixed),
                  pl.BlockSpec((tm, D_MODEL), row),
                  pl.BlockSpec((1, D_MODEL), fixed),
                  pl.BlockSpec((1, D_MODEL), fixed),
                  pl.BlockSpec((N_EXPERTS, D_MODEL), fixed),
                  pl.BlockSpec((N_EXPERTS, 1), fixed)],
        out_specs=[pl.BlockSpec((tm, D_MODEL), row),
                   pl.BlockSpec((1, TOP_K, tm), lambda i: (tile(i), 0, 0)),
                   pl.BlockSpec((TOP_K, tm), lambda i: (0, tile(i))),
                   pl.BlockSpec((N_EXPERTS, LANES), fixed),
                   pl.BlockSpec(memory_space=pl.ANY)],
        out_shape=[jax.ShapeDtypeStruct((T, D_MODEL), F32),
                   jax.ShapeDtypeStruct((n_tiles, TOP_K, tm), I32),
                   jax.ShapeDtypeStruct((TOP_K, T), F32),
                   jax.ShapeDtypeStruct((N_EXPERTS, LANES), F32),
                   jax.ShapeDtypeStruct((N_EXPERTS * cap + SUBLANES, HALF), U32)],
        scratch_shapes=[pltpu.VMEM((N_EXPERTS, LANES), F32), pltpu.VMEM((N_EXPERTS, LANES), F32),
                        pltpu.VMEM((2, tm, HALF), U32),
                        pltpu.VMEM((TOP_K, tm), I32),
                        pltpu.SMEM((2, TOP_K, tm), I32),
                        pltpu.SemaphoreType.DMA, pltpu.SemaphoreType.DMA],
        compiler_params=_params(("arbitrary",)),
        name="ln1_route",
    )(mixed, w_o, x2, ln_g.reshape(1, D_MODEL), ln_b.reshape(1, D_MODEL), w_router_t,
      router_bias.reshape(N_EXPERTS, 1))


def _row_copy(src_ref, src_row, dst_ref, dst_row, sem):
    return pltpu.make_async_copy(src_ref.at[pl.ds(src_row, 1)], dst_ref.at[pl.ds(dst_row, 1)], sem)


def _pad_fill_kernel(lo_ref, hi_ref, xs_in_ref, xs_ref, zero_ref, sem):
    del xs_in_ref
    zero_ref[...] = jnp.zeros_like(zero_ref)

    def group_copy(r):
        return pltpu.make_async_copy(zero_ref, xs_ref.at[pl.ds(pl.multiple_of(r, SUBLANES), SUBLANES)], sem)

    def fill_expert(e, counts):
        n_rows, n_groups = counts
        lo = lo_ref[e]
        hi = hi_ref[e]
        mid = jnp.minimum((lo + SUBLANES - 1) // SUBLANES * SUBLANES, hi)
        groups = (hi - mid) // SUBLANES

        def fill_row(r, c):
            _row_copy(zero_ref, 0, xs_ref, r, sem).start()
            return c

        def fill_group(g, c):
            group_copy(mid + g * SUBLANES).start()
            return c

        lax.fori_loop(lo, mid, fill_row, 0)
        lax.fori_loop(0, groups, fill_group, 0)
        return n_rows + (mid - lo), n_groups + groups

    n_rows, n_groups = lax.fori_loop(0, N_EXPERTS, fill_expert, (0, 0))

    def drain_row(r, c):
        _row_copy(zero_ref, 0, xs_ref, 0, sem).wait()
        return c

    def drain_group(g, c):
        group_copy(0).wait()
        return c

    lax.fori_loop(0, n_rows, drain_row, 0)
    lax.fori_loop(0, n_groups, drain_group, 0)


def _pad_fill(fill_lo, fill_hi, xs):
    smem = pl.BlockSpec(memory_space=pltpu.SMEM)
    return pl.pallas_call(
        _pad_fill_kernel,
        in_specs=[smem, smem, pl.BlockSpec(memory_space=pl.ANY)],
        out_specs=pl.BlockSpec(memory_space=pl.ANY),
        out_shape=jax.ShapeDtypeStruct(xs.shape, xs.dtype),
        scratch_shapes=[pltpu.VMEM((SUBLANES, HALF), U32), pltpu.SemaphoreType.DMA],
        input_output_aliases={2: 0},
        name="moe_pad_fill",
    )(fill_lo, fill_hi, xs)


def _experts_kernel(be_ref, nb_ref, brow_ref, next_ref, xs_ref, wg_hbm, wu_hbm, wd_hbm, ys_ref,
                    wgb_ref, wub_ref, wdb_ref, wg_f, wu_f, wd_f, sem, slot_ref):
    del brow_ref
    b = pl.program_id(0)
    new_expert = jnp.logical_or(b == 0, be_ref[b] != be_ref[jnp.maximum(b - 1, 0)])

    def weight_copies(e, s):
        return (pltpu.make_async_copy(wg_hbm.at[e], wg_f.at[s], sem.at[s]),
                pltpu.make_async_copy(wu_hbm.at[e], wu_f.at[s], sem.at[s]),
                pltpu.make_async_copy(wd_hbm.at[e], wd_f.at[s], sem.at[s]))

    @pl.when(b == 0)
    def _():
        slot_ref[0] = 0
        for c in weight_copies(be_ref[0], 0):
            c.start()

    @pl.when(jnp.logical_and(b < nb_ref[0], new_expert))
    def _():
        s = slot_ref[0]
        for c in weight_copies(be_ref[b], s):
            c.wait()

        @pl.when(next_ref[b] != be_ref[b])
        def _():
            for c in weight_copies(next_ref[b], 1 - s):
                c.start()

        wgb_ref[...] = wg_f[s].astype(BF16)
        wub_ref[...] = wu_f[s].astype(BF16)
        wdb_ref[...] = wd_f[s].astype(BF16)
        slot_ref[0] = 1 - s

    @pl.when(b < nb_ref[0])
    def _():
        lo, hi = _unpack_bf16_pairs(xs_ref[...])
        gate = _dot(lo, wgb_ref[:HALF, :]) + _dot(hi, wgb_ref[HALF:, :])
        up = _dot(lo, wub_ref[:HALF, :]) + _dot(hi, wub_ref[HALF:, :])
        hb = (jax.nn.silu(gate) * up).astype(BF16)
        ys_ref[...] = _pack_bf16_pairs(_dot(hb, wdb_ref[...]).astype(BF16))


def _experts(block_e, nb_used, block_row, next_e, xs, wg, wu, wd):
    n_rows = xs.shape[0] // MOE_ROWS * MOE_ROWS
    nb = block_e.shape[0]
    rows = lambda b, be, nbu, brow, nxt: (brow[b], 0)
    hbm = pl.BlockSpec(memory_space=pl.ANY)
    grid_spec = pltpu.PrefetchScalarGridSpec(
        num_scalar_prefetch=4,
        grid=(nb,),
        in_specs=[pl.BlockSpec((MOE_ROWS, HALF), rows), hbm, hbm, hbm],
        out_specs=pl.BlockSpec((MOE_ROWS, HALF), rows),
        scratch_shapes=[pltpu.VMEM((D_MODEL, D_EXPERT), BF16), pltpu.VMEM((D_MODEL, D_EXPERT), BF16),
                        pltpu.VMEM((D_EXPERT, D_MODEL), BF16),
                        pltpu.VMEM((2, D_MODEL, D_EXPERT), F32), pltpu.VMEM((2, D_MODEL, D_EXPERT), F32),
                        pltpu.VMEM((2, D_EXPERT, D_MODEL), F32),
                        pltpu.SemaphoreType.DMA((2,)), pltpu.SMEM((1,), I32)],
    )
    return pl.pallas_call(
        _experts_kernel,
        grid_spec=grid_spec,
        out_shape=jax.ShapeDtypeStruct((n_rows, HALF), U32),
        compiler_params=_params(("arbitrary",)),
        name="moe_experts",
    )(block_e, nb_used, block_row, next_e, xs, wg, wu, wd)


def _combine_kernel(dest_ref, next_ref, w_ref, h_ref, wgs_ref, wus_ref, wds_ref, g_ref, b_ref,
                    ys_ref, o_ref, buf_ref, sem, *, tm):
    i = pl.program_id(0)
    slot = i % 2

    def gather(idx_ref, dst_slot, t):
        for k in range(TOP_K):
            _row_copy(ys_ref, idx_ref[0, k, t], buf_ref.at[dst_slot, k], t, sem.at[dst_slot]).start()

    def wait_tile(s):
        pltpu.make_async_copy(buf_ref.at[s], buf_ref.at[s], sem.at[s]).wait()

    @pl.when(i == 0)
    def _():
        def first(t, c):
            gather(dest_ref, 0, t)
            return c

        lax.fori_loop(0, tm, first, 0)

    wait_tile(slot)
    for t in range(tm):
        gather(next_ref, 1 - slot, t)

    h = h_ref[...]
    hb = h.astype(BF16)
    hid = (jax.nn.silu(_dot(hb, wgs_ref[...])) * _dot(hb, wus_ref[...])).astype(BF16)
    shared = _dot(hid, wds_ref[...])

    w = w_ref[...]
    lo_sum = jnp.zeros((tm, HALF), F32)
    hi_sum = jnp.zeros((tm, HALF), F32)
    for k in range(TOP_K):
        p = buf_ref[slot, k]
        lo_sum = lo_sum + lax.bitcast_convert_type(p << 16, F32) * w[:, k:k + 1]
        hi_sum = hi_sum + lax.bitcast_convert_type(p & jnp.uint32(0xFFFF0000), F32) * w[:, k:k + 1]
    routed = jnp.concatenate([lo_sum, hi_sum], axis=1)
    o_ref[...] = _layer_norm(DN_ALPHA * h + (routed + shared), g_ref[...], b_ref[...])

    @pl.when(i == pl.num_programs(0) - 1)
    def _():
        wait_tile(1 - slot)


def _combine(dest3, w_tok, h, wgs, wus, wds, ln_g, ln_b, ys3):
    T = h.shape[0]
    n_tiles, _, tm = dest3.shape
    kern = functools.partial(_combine_kernel, tm=tm)
    row = lambda i: (i, 0)
    fixed = lambda i: (0, 0)
    return pl.pallas_call(
        kern,
        grid=(n_tiles,),
        in_specs=[pl.BlockSpec((1, TOP_K, tm), lambda i: (i, 0, 0), memory_space=pltpu.SMEM),
                  pl.BlockSpec((1, TOP_K, tm), lambda i: (jnp.minimum(i + 1, n_tiles - 1), 0, 0),
                               memory_space=pltpu.SMEM),
                  pl.BlockSpec((tm, TOP_K), row),
                  pl.BlockSpec((tm, D_MODEL), row),
                  pl.BlockSpec((D_MODEL, D_SHARED), fixed),
                  pl.BlockSpec((D_MODEL, D_SHARED), fixed),
                  pl.BlockSpec((D_SHARED, D_MODEL), fixed),
                  pl.BlockSpec((1, D_MODEL), fixed),
                  pl.BlockSpec((1, D_MODEL), fixed),
                  pl.BlockSpec(memory_space=pl.ANY)],
        out_specs=pl.BlockSpec((tm, D_MODEL), row),
        out_shape=jax.ShapeDtypeStruct((T, D_MODEL), F32),
        scratch_shapes=[pltpu.VMEM((2, TOP_K, tm, HALF), U32), pltpu.SemaphoreType.DMA((2,))],
        compiler_params=_params(("arbitrary",)),
        name="moe_combine",
    )(dest3, dest3, w_tok, h, wgs, wus, wds, ln_g.reshape(1, D_MODEL), ln_b.reshape(1, D_MODEL), ys3)


def _tile(n, pref):
    t = min(n, pref)
    assert n % t == 0, (n, pref)
    return t


def _layer(x, w_in, conv_w, conv_b, lru_wa, lru_ba, lru_wi, lru_bi, lru_lambda, ret_gn_gain,
           w_lru_out, w_ret_out, b_gate, w_o, ln1_g, ln1_b, w_router, router_bias,
           w_gate_e, w_up_e, w_down_e, w_gate_s, w_up_s, w_down_s, ln2_g, ln2_b):
    B, S, D = x.shape
    T = B * S
    x2 = x.reshape(T, D)
    xb = x2.astype(BF16)

    half = RET_DK // 2
    freq = ROPE_THETA ** (-jnp.arange(half, dtype=F32) / half)
    ang = jnp.arange(S, dtype=I32).astype(F32)[:, None] * freq
    cos, sin = jnp.cos(ang), jnp.sin(ang)

    ua, v, q, k = _lru_branch(xb, w_in, cos, sin, conv_w, conv_b, lru_wa, lru_ba, lru_wi, lru_bi,
                              lru_lambda, S, _tile(S, 256), 512)
    ub = _retention(q, k, v, xb, w_in, ret_gn_gain, B, S)

    mixed = _mix(ua, ub, xb, w_lru_out.astype(BF16), w_ret_out.astype(BF16),
                 w_in[:, OFF_GA:].astype(BF16), b_gate, _tile(T, 512), 512)

    tm_r = _tile(T, 256)
    cap = (T + MOE_ROWS - 1) // MOE_ROWS * MOE_ROWS
    h, dest, wsel, cnt, xs = _ln1_route(mixed, w_o.astype(BF16), x2, ln1_g, ln1_b,
                                        w_router.T.astype(BF16), router_bias, tm_r, cap)

    counts = cnt[:, 0].astype(I32)
    e_blocks = (counts + MOE_ROWS - 1) // MOE_ROWS
    blocks_end = jnp.cumsum(e_blocks)
    nb = (T * TOP_K + N_EXPERTS * (MOE_ROWS - 1) + MOE_ROWS - 1) // MOE_ROWS
    nb_used = blocks_end[-1:]
    blk = jnp.minimum(jnp.arange(nb, dtype=I32), nb_used - 1)
    block_e = jnp.minimum(jnp.sum((blocks_end[None, :] <= blk[:, None]).astype(I32), axis=1),
                          N_EXPERTS - 1)
    blocks_start = blocks_end - e_blocks
    e_onehot = block_e[:, None] == jnp.arange(N_EXPERTS, dtype=I32)[None, :]
    block_row = block_e * (cap // MOE_ROWS) + blk - jnp.sum(jnp.where(e_onehot, blocks_start[None, :], 0),
                                                            axis=1)
    e_row0 = jnp.arange(N_EXPERTS, dtype=I32) * cap
    seg_end = jnp.sum(jnp.where(e_onehot, blocks_end[None, :], 0), axis=1)
    after = jnp.minimum(seg_end, nb_used - 1)
    next_e = jnp.sum(jnp.where(after[:, None] == jnp.arange(nb, dtype=I32)[None, :], block_e[None, :], 0),
                     axis=1)

    xs = _pad_fill(e_row0 + counts, e_row0 + e_blocks * MOE_ROWS, xs)
    ys = _experts(block_e, nb_used, block_row, next_e, xs, w_gate_e, w_up_e, w_down_e)
    out = _combine(dest, wsel.T, h, w_gate_s.astype(BF16), w_up_s.astype(BF16),
                   w_down_s.astype(BF16), ln2_g, ln2_b, ys)
    return out.reshape(B, S, D)


def kernel(x, w_in, conv_w, conv_b, lru_wa, lru_ba, lru_wi, lru_bi, lru_lambda, ret_gn_gain, w_lru_out, w_ret_out, b_gate, w_o, ln1_g, ln1_b, w_router, router_bias, w_gate_e, w_up_e, w_down_e, w_gate_s, w_up_s, w_down_s, ln2_g, ln2_b):
    assert DEPTH == 1 and w_in.shape[0] == DEPTH
    args = (w_in, conv_w, conv_b, lru_wa, lru_ba, lru_wi, lru_bi, lru_lambda, ret_gn_gain,
            w_lru_out, w_ret_out, b_gate, w_o, ln1_g, ln1_b, w_router, router_bias,
            w_gate_e, w_up_e, w_down_e, w_gate_s, w_up_s, w_down_s, ln2_g, ln2_b)
    return _layer(x, *[a[0] for a in args])
```

```python
import functools

import jax
import jax.numpy as jnp
from jax import lax
from jax.experimental import pallas as pl
from jax.experimental.pallas import tpu as pltpu

F32 = jnp.float32
BF16 = jnp.bfloat16
I32 = jnp.int32
U32 = jnp.uint32

D_MODEL = 2048
D_LRU = 2048
LRU_HEADS = 16
LRU_BLOCK = D_LRU // LRU_HEADS
CONV_WIDTH = 4
LRU_C = 8.0
RET_HEADS = 8
RET_DK = 256
RET_DV = 512
D_QK = RET_HEADS * RET_DK
D_RV = RET_HEADS * RET_DV
ROPE_THETA = 10000.0
N_EXPERTS = 64
TOP_K = 8
N_GROUPS = 8
GROUP_SIZE = N_EXPERTS // N_GROUPS
TOPK_GROUPS = 4
D_EXPERT = 512
D_SHARED = 512
ROUTED_SCALE = 2.5
DEPTH = 1
DN_ALPHA = (2.0 * DEPTH) ** 0.25
LN_EPS = 1e-5

OFF_LRU_X = 0
OFF_LRU_Y = OFF_LRU_X + D_LRU
OFF_Q = OFF_LRU_Y + D_LRU
OFF_K = OFF_Q + D_QK
OFF_V = OFF_K + D_QK
OFF_G = OFF_V + D_RV
OFF_GA = OFF_G + D_RV
OFF_GB = OFF_GA + D_MODEL

V7X_VMEM_LIMIT = 56 * 1024 * 1024
LANES = 128
SUBLANES = 8
SUBLANE_BITS = SUBLANES.bit_length() - 1
MOE_ROWS = 512
RET_CHUNK = 256
LRU_SUB = 2
HALF = D_MODEL // 2


def _params(sem):
    return pltpu.CompilerParams(dimension_semantics=sem, vmem_limit_bytes=V7X_VMEM_LIMIT)


def _dot(a, b):
    return jnp.dot(a, b, preferred_element_type=F32)


def _lru_kernel(x_ref, wx_ref, wy_ref, wv_ref, wq_ref, wk_ref, cos_ref, sin_ref, cw_ref, cb_ref,
                wai_ref, ba_ref, bi_ref, lam_ref,
                o_ref, v_ref, q_ref, k_ref,
                wxb_ref, wyb_ref, wvb_ref, wqb_ref, wkb_ref, perm_ref, unperm_ref, tail_ref, hc_ref,
                a_ref, b_ref, gy_ref, *, tiles_per_seq, tm, width):
    i = pl.program_id(1)
    seg = tm // SUBLANES
    halo = (CONV_WIDTH - 1) * SUBLANES

    @pl.when(i == 0)
    def _():
        wxb_ref[...] = wx_ref[...].astype(BF16)
        wyb_ref[...] = wy_ref[...].astype(BF16)
        wvb_ref[...] = wv_ref[...].astype(BF16)
        wqb_ref[...] = wq_ref[...].astype(BF16)
        wkb_ref[...] = wk_ref[...].astype(BF16)
        r = lax.broadcasted_iota(I32, (tm, tm), 0)
        c = lax.broadcasted_iota(I32, (tm, tm), 1)
        r_time = (r & (SUBLANES - 1)) * seg + (r >> SUBLANE_BITS)
        c_time = (c & (SUBLANES - 1)) * seg + (c >> SUBLANE_BITS)
        perm_ref[...] = jnp.where(c == r_time, 1.0, 0.0).astype(BF16)
        unperm_ref[...] = jnp.where(r == c_time, 1.0, 0.0).astype(BF16)
        tail_ref[...] = jnp.zeros_like(tail_ref)
        hc_ref[...] = jnp.zeros_like(hc_ref)

    cw = cw_ref[...]
    cb = cb_ref[...]
    taps = [cw[CONV_WIDTH - 1 - d:CONV_WIDTH - d] for d in range(CONV_WIDTH)]
    sub = lax.broadcasted_iota(I32, (SUBLANES, width), 0)
    row = lax.broadcasted_iota(I32, (tm, width), 0)
    v_piece = v_ref.shape[1] // 4

    def one_tile(u, tail, carry):
        rows_u = slice(u * tm, (u + 1) * tm)
        first = ((i * LRU_SUB + u) % tiles_per_seq) == 0
        tail = jnp.where(first, 0.0, tail)
        carry = jnp.where(first, 0.0, carry)
        x = x_ref[rows_u, :]

        def emit_v(piece):
            cols = slice(piece * v_piece, (piece + 1) * v_piece)
            v_ref[rows_u, cols] = _dot(x, wvb_ref[:, cols]).astype(v_ref.dtype)

        def emit_rotary(wb_ref, out_ref, scale):
            acc = _dot(x, wb_ref[...])
            cos = cos_ref[rows_u, :]
            sin = sin_ref[rows_u, :]
            half = RET_DK // 2
            for hh in range(width // RET_DK):
                c0 = hh * RET_DK
                t1 = acc[:, c0:c0 + half]
                t2 = acc[:, c0 + half:c0 + RET_DK]
                out_ref[rows_u, c0:c0 + half] = ((t1 * cos - t2 * sin) * scale).astype(out_ref.dtype)
                out_ref[rows_u, c0 + half:c0 + RET_DK] = (
                    (t1 * sin + t2 * cos) * scale).astype(out_ref.dtype)

        emit_v(0)
        xp = _dot(perm_ref[...], x).astype(BF16)
        px = _dot(xp, wxb_ref[...])
        gy_ref[u] = jax.nn.gelu(_dot(xp, wyb_ref[...]), approximate=True)

        xa = cb + taps[0] * px
        for d in range(1, CONV_WIDTH):
            back = jnp.concatenate([px[tm - d * SUBLANES:], px[:tm - d * SUBLANES]], axis=0)
            xa = xa + taps[d] * back
        groups = {}
        for g in range(CONV_WIDTH - 1):
            rows = slice(g * SUBLANES, (g + 1) * SUBLANES)
            cur_end = px[tm - halo + g * SUBLANES:tm - halo + (g + 1) * SUBLANES]
            groups[g - (CONV_WIDTH - 1)] = jnp.where(sub == 0, pltpu.roll(tail[rows, :], 1, 0),
                                                     pltpu.roll(cur_end, 1, 0))
            groups[g] = px[rows, :]
        head = []
        for q in range(CONV_WIDTH - 1):
            acc = cb + taps[0] * groups[q]
            for d in range(1, CONV_WIDTH):
                acc = acc + taps[d] * groups[q - d]
            head.append(acc)
        xa = jnp.concatenate(head + [xa[halo:]], axis=0)
        tail_out = px[tm - halo:]
        emit_v(1)

        xab = xa.astype(BF16)
        r_parts, i_parts = [], []
        for hh in range(width // LRU_BLOCK):
            both = _dot(xab[:, hh * LRU_BLOCK:(hh + 1) * LRU_BLOCK], wai_ref[hh])
            r_parts.append(both[:, :LRU_BLOCK])
            i_parts.append(both[:, LRU_BLOCK:])
        rg = jax.nn.sigmoid(jnp.concatenate(r_parts, axis=1) + ba_ref[...])
        ig = jax.nn.sigmoid(jnp.concatenate(i_parts, axis=1) + bi_ref[...])
        emit_rotary(wqb_ref, q_ref, 1.0)

        log_a = -LRU_C * rg * jax.nn.softplus(-lam_ref[...])
        a = jnp.exp(log_a)
        mult = jnp.sqrt(-jnp.tanh(log_a) * (a * a + 1.0))
        mult = jnp.where(jnp.logical_and(first, row == 0), 1.0, mult)
        a_ref[u] = a
        b_ref[u] = mult * ig * xa
        emit_v(2)

        run_a = jnp.ones((SUBLANES, width), F32)
        run_h = jnp.zeros((SUBLANES, width), F32)
        for q in range(seg):
            rows = slice(q * SUBLANES, (q + 1) * SUBLANES)
            aq = a_ref[u, rows, :]
            run_h = aq * run_h + b_ref[u, rows, :]
            run_a = aq * run_a
            b_ref[u, rows, :] = run_h
            a_ref[u, rows, :] = run_a
        emit_v(3)

        carries = []
        for s in range(SUBLANES):
            carries.append(carry)
            carry = run_a[s:s + 1, :] * carry + run_h[s:s + 1, :]
        carry_in = jnp.concatenate(carries, axis=0)
        for q in range(seg):
            rows = slice(q * SUBLANES, (q + 1) * SUBLANES)
            b_ref[u, rows, :] = (b_ref[u, rows, :] + a_ref[u, rows, :] * carry_in) * gy_ref[u, rows, :]

        emit_rotary(wkb_ref, k_ref, RET_DK ** -0.5)
        o_ref[rows_u, :] = _dot(unperm_ref[...], b_ref[u].astype(BF16)).astype(o_ref.dtype)
        return tail_out, carry

    tail = tail_ref[...]
    carry = hc_ref[...]
    for u in range(LRU_SUB):
        tail, carry = one_tile(u, tail, carry)
    tail_ref[...] = tail
    hc_ref[...] = carry


def _lru_branch(xb, w, cos, sin, conv_w, conv_b, wa, ba, wi, bi, lam, seq, tm, width):
    T, K = xb.shape
    nj = D_LRU // width
    assert D_QK == D_LRU, "q / k column tiles are walked together with the lru tiles"
    jy0 = OFF_LRU_Y // width
    jq0 = OFF_Q // width
    jk0 = OFF_K // width
    hp = width // LRU_BLOCK
    vw = D_RV // nj
    jv0 = OFF_V // vw
    tiles_per_seq = seq // tm
    rows = LRU_SUB * tm
    reps = max(1, rows // seq)
    cos, sin = jnp.tile(cos, (reps, 1)), jnp.tile(sin, (reps, 1))
    table_blocks = cos.shape[0] // rows
    kern = functools.partial(_lru_kernel, tiles_per_seq=tiles_per_seq, tm=tm, width=width)
    vec = lambda j, i: (0, j)
    once = pl.Buffered(1)
    table = pl.BlockSpec((rows, RET_DK // 2), lambda j, i: (i % table_blocks, 0))
    return pl.pallas_call(
        kern,
        grid=(nj, T // rows),
        in_specs=[pl.BlockSpec((rows, K), lambda j, i: (i, 0)),
                  pl.BlockSpec((K, width), lambda j, i: (0, j), pipeline_mode=once),
                  pl.BlockSpec((K, width), lambda j, i: (0, jy0 + j), pipeline_mode=once),
                  pl.BlockSpec((K, vw), lambda j, i: (0, jv0 + j), pipeline_mode=once),
                  pl.BlockSpec((K, width), lambda j, i: (0, jq0 + j), pipeline_mode=once),
                  pl.BlockSpec((K, width), lambda j, i: (0, jk0 + j), pipeline_mode=once),
                  table, table,
                  pl.BlockSpec((CONV_WIDTH, width), vec),
                  pl.BlockSpec((1, width), vec),
                  pl.BlockSpec((hp, LRU_BLOCK, 2 * LRU_BLOCK), lambda j, i: (j, 0, 0)),
                  pl.BlockSpec((1, width), vec),
                  pl.BlockSpec((1, width), vec),
                  pl.BlockSpec((1, width), vec)],
        out_specs=[pl.BlockSpec((rows, width), lambda j, i: (i, j)),
                   pl.BlockSpec((rows, vw), lambda j, i: (i, j)),
                   pl.BlockSpec((rows, width), lambda j, i: (i, j)),
                   pl.BlockSpec((rows, width), lambda j, i: (i, j))],
        out_shape=[jax.ShapeDtypeStruct((T, D_LRU), BF16), jax.ShapeDtypeStruct((T, D_RV), BF16),
                   jax.ShapeDtypeStruct((T, D_QK), BF16), jax.ShapeDtypeStruct((T, D_QK), BF16)],
        scratch_shapes=[pltpu.VMEM((K, width), BF16), pltpu.VMEM((K, width), BF16),
                        pltpu.VMEM((K, vw), BF16),
                        pltpu.VMEM((K, width), BF16), pltpu.VMEM((K, width), BF16),
                        pltpu.VMEM((tm, tm), BF16), pltpu.VMEM((tm, tm), BF16),
                        pltpu.VMEM(((CONV_WIDTH - 1) * SUBLANES, width), F32),
                        pltpu.VMEM((1, width), F32),
                        pltpu.VMEM((LRU_SUB, tm, width), F32), pltpu.VMEM((LRU_SUB, tm, width), F32),
                        pltpu.VMEM((LRU_SUB, tm, width), F32)],
        compiler_params=_params(("parallel", "arbitrary")),
        name="lru_branch",
    )(xb, w, w, w, w, w, cos, sin, conv_w, conv_b.reshape(1, D_LRU),
      jnp.concatenate([wa, wi], axis=-1).astype(BF16), ba.reshape(1, D_LRU), bi.reshape(1, D_LRU),
      lam.reshape(1, D_LRU))


def _ret_kernel(gc_ref, q_ref, k_ref, v_ref, x_ref, w_hbm, gain_ref, dm_ref, xi_ref, zeta_ref, o_ref,
                st_ref, wg_ref, stage_ref, wg_sem):
    @pl.when(jnp.logical_and(pl.program_id(0) == 0, pl.program_id(1) == 0))
    def _():
        for hd in range(RET_HEADS):
            cols = pl.ds(OFF_G + hd * RET_DV, RET_DV)
            fetch = pltpu.make_async_copy(w_hbm.at[:, cols], stage_ref, wg_sem)
            fetch.start()
            fetch.wait()
            wg_ref[:, hd * RET_DV:(hd + 1) * RET_DV] = stage_ref[...].astype(BF16)

    @pl.when(pl.program_id(1) == 0)
    def _():
        st_ref[...] = jnp.zeros_like(st_ref)

    x = x_ref[...]
    for hd in range(RET_HEADS):
        qc = slice(hd * RET_DK, (hd + 1) * RET_DK)
        vc = slice(hd * RET_DV, (hd + 1) * RET_DV)
        q = q_ref[:, qc]
        k = k_ref[:, qc]
        v = v_ref[:, vc]
        st = st_ref[hd]
        s = lax.dot_general(q, k, (((1,), (1,)), ((), ())), preferred_element_type=F32) * dm_ref[hd]
        o = _dot(s.astype(BF16), v) + _dot(q, st.astype(BF16)) * xi_ref[hd]
        kz = (k.astype(F32) * zeta_ref[hd]).astype(BF16)
        st_ref[hd] = gc_ref[hd] * st + lax.dot_general(kz, v, (((0,), (0,)), ((), ())),
                                                       preferred_element_type=F32)
        mu = jnp.mean(o, axis=-1, keepdims=True)
        oc = o - mu
        var = jnp.mean(oc * oc, axis=-1, keepdims=True)
        oh = oc * lax.rsqrt(var + LN_EPS) * gain_ref[:, vc]
        g = _dot(x, wg_ref[:, vc])
        o_ref[:, vc] = (jax.nn.silu(g) * oh).astype(o_ref.dtype)


def _retention(q, k, v, xb, w_in, gain, batch, seq):
    T = batch * seq
    C = min(RET_CHUNK, seq)
    nchunk = seq // C
    H = RET_HEADS
    log_g = jnp.log1p(-jnp.exp2(-5.0 - jnp.arange(H, dtype=F32)))
    idx = jnp.arange(C, dtype=F32)
    diff = idx[:, None] - idx[None, :]
    dmask = jnp.where(diff >= 0, jnp.exp(jnp.maximum(diff, 0.0)[None] * log_g[:, None, None]), 0.0)
    xi = jnp.exp((idx[None] + 1.0) * log_g[:, None])[:, :, None]
    zeta = jnp.exp((C - 1.0 - idx[None]) * log_g[:, None])[:, :, None]
    g_c = jnp.exp(C * log_g)
    rows = lambda b, n: b * nchunk + n
    whole3 = lambda b, n: (0, 0, 0)
    return pl.pallas_call(
        _ret_kernel,
        grid=(batch, nchunk),
        in_specs=[pl.BlockSpec(memory_space=pltpu.SMEM),
                  pl.BlockSpec((C, D_QK), lambda b, n: (rows(b, n), 0)),
                  pl.BlockSpec((C, D_QK), lambda b, n: (rows(b, n), 0)),
                  pl.BlockSpec((C, D_RV), lambda b, n: (rows(b, n), 0)),
                  pl.BlockSpec((C, D_MODEL), lambda b, n: (rows(b, n), 0)),
                  pl.BlockSpec(memory_space=pl.ANY),
                  pl.BlockSpec((1, D_RV), lambda b, n: (0, 0)),
                  pl.BlockSpec((H, C, C), whole3),
                  pl.BlockSpec((H, C, 1), whole3),
                  pl.BlockSpec((H, C, 1), whole3)],
        out_specs=pl.BlockSpec((C, D_RV), lambda b, n: (rows(b, n), 0)),
        out_shape=jax.ShapeDtypeStruct((T, D_RV), BF16),
        scratch_shapes=[pltpu.VMEM((H, RET_DK, RET_DV), F32), pltpu.VMEM((D_MODEL, D_RV), BF16),
                        pltpu.VMEM((D_MODEL, RET_DV), F32), pltpu.SemaphoreType.DMA],
        compiler_params=_params(("arbitrary", "arbitrary")),
        name="retention",
    )(g_c, q, k, v, xb, w_in, gain.reshape(1, D_RV), dmask, xi, zeta)


def _mix_kernel(ua_ref, ub_ref, x_ref, wlo_ref, wro_ref, wga_ref, wgb_ref, bga_ref, bgb_ref, o_ref):
    x = x_ref[...]
    ya = _dot(ua_ref[...], wlo_ref[...])
    yb = _dot(ub_ref[...], wro_ref[...])
    ga = jax.nn.sigmoid(_dot(x, wga_ref[...]) + bga_ref[...])
    gb = jax.nn.sigmoid(_dot(x, wgb_ref[...]) + bgb_ref[...])
    o_ref[...] = (ga * ya + gb * yb).astype(o_ref.dtype)


def _mix(ua, ub, xb, w_lru_out, w_ret_out, w_gates, b_gate, tm, tn):
    T = xb.shape[0]
    nj = D_MODEL // tn
    return pl.pallas_call(
        _mix_kernel,
        grid=(T // tm, nj),
        in_specs=[pl.BlockSpec((tm, D_LRU), lambda i, j: (i, 0)),
                  pl.BlockSpec((tm, D_RV), lambda i, j: (i, 0)),
                  pl.BlockSpec((tm, D_MODEL), lambda i, j: (i, 0)),
                  pl.BlockSpec((D_LRU, tn), lambda i, j: (0, j)),
                  pl.BlockSpec((D_RV, tn), lambda i, j: (0, j)),
                  pl.BlockSpec((D_MODEL, tn), lambda i, j: (0, j)),
                  pl.BlockSpec((D_MODEL, tn), lambda i, j: (0, nj + j)),
                  pl.BlockSpec((1, tn), lambda i, j: (0, j)),
                  pl.BlockSpec((1, tn), lambda i, j: (0, nj + j))],
        out_specs=pl.BlockSpec((tm, tn), lambda i, j: (i, j)),
        out_shape=jax.ShapeDtypeStruct((T, D_MODEL), BF16),
        compiler_params=_params(("parallel", "parallel")),
        name="gated_mix",
    )(ua, ub, xb, w_lru_out, w_ret_out, w_gates, w_gates, b_gate.reshape(1, 2 * D_MODEL),
      b_gate.reshape(1, 2 * D_MODEL))


def _layer_norm(y, g, b):
    mu = jnp.mean(y, axis=-1, keepdims=True)
    yc = y - mu
    var = jnp.mean(yc * yc, axis=-1, keepdims=True)
    return yc * lax.rsqrt(var + LN_EPS) * g + b


def _pack_bf16_pairs(hb):
    lo = lax.bitcast_convert_type(hb[:, :HALF].astype(F32), U32)
    hi = lax.bitcast_convert_type(hb[:, HALF:].astype(F32), U32)
    return (hi & jnp.uint32(0xFFFF0000)) | (lo >> 16)


def _unpack_bf16_pairs(p):
    lo = lax.bitcast_convert_type(p << 16, F32).astype(BF16)
    hi = lax.bitcast_convert_type(p & jnp.uint32(0xFFFF0000), F32).astype(BF16)
    return lo, hi


def _max2(a):
    return jnp.max(jnp.max(a, axis=1, keepdims=True), axis=0, keepdims=True)


def _min2(a):
    return jnp.min(jnp.min(a, axis=1, keepdims=True), axis=0, keepdims=True)


def _sum2(a):
    return jnp.sum(jnp.sum(a, axis=1, keepdims=True), axis=0, keepdims=True)


def _ln1_route_kernel(mix_ref, wo_ref, x_ref, g_ref, b_ref, wr_ref, rb_ref,
                      h_ref, dest_ref, wsel_ref, cnt_ref, xs_ref,
                      cnt_sc, base_sc, hp_sc, dest_v, dest_s, row_sem, idx_sem, *, tm, n_tiles, cap):
    i = pl.program_id(0)
    slot = i % 2
    prev = 1 - slot

    def wait_rows(s):
        for _ in range(TOP_K):
            pltpu.make_async_copy(hp_sc.at[s], hp_sc.at[s], row_sem).wait()

    @pl.when(i == 0)
    def _():
        cnt_sc[...] = jnp.zeros_like(cnt_sc)
        base_sc[...] = jnp.zeros_like(base_sc)
        hp_sc[1] = jnp.zeros((tm, HALF), U32)

        def spare(t, c):
            for k in range(TOP_K):
                dest_s[1, k, t] = N_EXPERTS * cap + k
            return c

        lax.fori_loop(0, tm, spare, 0)

    @pl.when(i > 0)
    def _():
        wait_rows(slot)
        pltpu.make_async_copy(dest_v, dest_s.at[prev], idx_sem).wait()

    for t in range(tm):
        for k in range(TOP_K):
            _row_copy(hp_sc.at[prev], t, xs_ref, dest_s[prev, k, t], row_sem).start()

    y = DN_ALPHA * x_ref[...] + _dot(mix_ref[...], wo_ref[...])
    h = _layer_norm(y, g_ref[...], b_ref[...])
    h_ref[...] = h
    hb = h.astype(BF16)
    hp_sc[slot] = _pack_bf16_pairs(hb)

    G, GS = N_GROUPS, GROUP_SIZE
    logits = lax.dot_general(wr_ref[...], hb, (((1,), (1,)), ((), ())), preferred_element_type=F32)
    scores = jax.nn.sigmoid(logits)
    s3 = scores.reshape(G, GS, tm)
    b3 = (scores + rb_ref[...]).reshape(G, GS, tm)
    neg = jnp.float32(-jnp.inf)

    in_grp = lax.broadcasted_iota(I32, (G, GS, tm), 1)
    m1 = jnp.max(b3, axis=1, keepdims=True)
    f1 = jnp.min(jnp.where(b3 == m1, in_grp, GS), axis=1, keepdims=True)
    m2 = jnp.max(jnp.where(in_grp == f1, neg, b3), axis=1, keepdims=True)
    grp = m1 + m2

    gi = lax.broadcasted_iota(I32, (G, 1, tm), 0)
    gkeep = jnp.zeros((G, 1, tm), F32)
    cur = grp
    for _ in range(TOPK_GROUPS):
        m = jnp.max(cur, axis=0, keepdims=True)
        f = jnp.min(jnp.where(cur == m, gi, G), axis=0, keepdims=True)
        hit = gi == f
        gkeep = jnp.where(hit, 1.0, gkeep)
        cur = jnp.where(hit, neg, cur)
    emask = jnp.broadcast_to(gkeep, (G, GS, tm)) > 0.0

    ei = lax.broadcasted_iota(I32, (G, GS, tm), 0) * GS + in_grp
    cur = jnp.where(emask, b3, neg)
    hits, idxs, ws = [], [], []
    for _ in range(TOP_K):
        m = _max2(cur)
        f = _min2(jnp.where(cur == m, ei, N_EXPERTS))
        hit = ei == f
        cur = jnp.where(hit, neg, cur)
        hits.append(hit)
        idxs.append(f)
        ws.append(_sum2(jnp.where(hit, s3, 0.0)))
    wsum = ws[0]
    for r in range(1, TOP_K):
        wsum = wsum + ws[r]

    sel = jnp.zeros((G, GS, tm), F32)
    for hit in hits:
        sel = jnp.where(hit, 1.0, sel)
    sel2 = sel.reshape(N_EXPERTS, tm)
    tri = (lax.broadcasted_iota(I32, (tm, tm), 0) < lax.broadcasted_iota(I32, (tm, tm), 1))
    rank_local = _dot(sel2.astype(BF16), jnp.where(tri, 1.0, 0.0).astype(BF16))
    base = jnp.where(i < n_tiles, cnt_sc[...], base_sc[...])
    base_sc[...] = base
    rank3 = (rank_local + base[:, 0:1]).reshape(G, GS, tm)
    cnt_new = base + jnp.sum(sel2, axis=1, keepdims=True)
    cnt_sc[...] = cnt_new
    cnt_ref[...] = cnt_new

    for r in range(TOP_K):
        rank_r = _sum2(jnp.where(hits[r], rank3, 0.0)).reshape(1, tm).astype(I32)
        dest_v[r:r + 1, :] = idxs[r].reshape(1, tm) * cap + rank_r
        wsel_ref[r:r + 1, :] = (ws[r] / wsum * ROUTED_SCALE).reshape(1, tm)
    dest_ref[0] = dest_v[...]

    to_smem = pltpu.make_async_copy(dest_v, dest_s.at[slot], idx_sem)
    to_smem.start()

    @pl.when(i == n_tiles)
    def _():
        wait_rows(prev)
        to_smem.wait()


def _ln1_route(mixed, w_o, x2, ln_g, ln_b, w_router_t, router_bias, tm, cap):
    T = x2.shape[0]
    n_tiles = T // tm
    kern = functools.partial(_ln1_route_kernel, tm=tm, n_tiles=n_tiles, cap=cap)
    tile = lambda i: jnp.minimum(i, n_tiles - 1)
    row = lambda i: (tile(i), 0)
    fixed = lambda i: (0, 0)
    return pl.pallas_call(
        kern,
        grid=(n_tiles + 1,),
        in_specs=[pl.BlockSpec((tm, D_MODEL), row),
                  pl.BlockSpec((D_MODEL, D_MODEL), fixed),
                  pl.BlockSpec((tm, D_MODEL), row),
                  pl.BlockSpec((1, D_MODEL), fixed),
                  pl.BlockSpec((1, D_MODEL), fixed),
                  pl.BlockSpec((N_EXPERTS, D_MODEL), fixed),
                  pl.BlockSpec((N_EXPERTS, 1), fixed)],
        out_specs=[pl.BlockSpec((tm, D_MODEL), row),
                   pl.BlockSpec((1, TOP_K, tm), lambda i: (tile(i), 0, 0)),
                   pl.BlockSpec((TOP_K, tm), lambda i: (0, tile(i))),
                   pl.BlockSpec((N_EXPERTS, LANES), fixed),
                   pl.BlockSpec(memory_space=pl.ANY)],
        out_shape=[jax.ShapeDtypeStruct((T, D_MODEL), F32),
                   jax.ShapeDtypeStruct((n_tiles, TOP_K, tm), I32),
                   jax.ShapeDtypeStruct((TOP_K, T), F32),
                   jax.ShapeDtypeStruct((N_EXPERTS, LANES), F32),
                   jax.ShapeDtypeStruct((N_EXPERTS * cap + SUBLANES, HALF), U32)],
        scratch_shapes=[pltpu.VMEM((N_EXPERTS, LANES), F32), pltpu.VMEM((N_EXPERTS, LANES), F32),
                        pltpu.VMEM((2, tm, HALF), U32),
                        pltpu.VMEM((TOP_K, tm), I32),
                        pltpu.SMEM((2, TOP_K, tm), I32),
                        pltpu.SemaphoreType.DMA, pltpu.SemaphoreType.DMA],
        compiler_params=_params(("arbitrary",)),
        name="ln1_route",
    )(mixed, w_o, x2, ln_g.reshape(1, D_MODEL), ln_b.reshape(1, D_MODEL), w_router_t,
      router_bias.reshape(N_EXPERTS, 1))


def _row_copy(src_ref, src_row, dst_ref, dst_row, sem):
    return pltpu.make_async_copy(src_ref.at[pl.ds(src_row, 1)], dst_ref.at[pl.ds(dst_row, 1)], sem)


def _pad_fill_kernel(lo_ref, hi_ref, xs_in_ref, xs_ref, zero_ref, sem):
    del xs_in_ref
    zero_ref[...] = jnp.zeros_like(zero_ref)

    def group_copy(r):
        return pltpu.make_async_copy(zero_ref, xs_ref.at[pl.ds(pl.multiple_of(r, SUBLANES), SUBLANES)], sem)

    def fill_expert(e, counts):
        n_rows, n_groups = counts
        lo = lo_ref[e]
        hi = hi_ref[e]
        mid = jnp.minimum((lo + SUBLANES - 1) // SUBLANES * SUBLANES, hi)
        groups = (hi - mid) // SUBLANES

        def fill_row(r, c):
            _row_copy(zero_ref, 0, xs_ref, r, sem).start()
            return c

        def fill_group(g, c):
            group_copy(mid + g * SUBLANES).start()
            return c

        lax.fori_loop(lo, mid, fill_row, 0)
        lax.fori_loop(0, groups, fill_group, 0)
        return n_rows + (mid - lo), n_groups + groups

    n_rows, n_groups = lax.fori_loop(0, N_EXPERTS, fill_expert, (0, 0))

    def drain_row(r, c):
        _row_copy(zero_ref, 0, xs_ref, 0, sem).wait()
        return c

    def drain_group(g, c):
        group_copy(0).wait()
        return c

    lax.fori_loop(0, n_rows, drain_row, 0)
    lax.fori_loop(0, n_groups, drain_group, 0)


def _pad_fill(fill_lo, fill_hi, xs):
    smem = pl.BlockSpec(memory_space=pltpu.SMEM)
    return pl.pallas_call(
        _pad_fill_kernel,
        in_specs=[smem, smem, pl.BlockSpec(memory_space=pl.ANY)],
        out_specs=pl.BlockSpec(memory_space=pl.ANY),
        out_shape=jax.ShapeDtypeStruct(xs.shape, xs.dtype),
        scratch_shapes=[pltpu.VMEM((SUBLANES, HALF), U32), pltpu.SemaphoreType.DMA],
        input_output_aliases={2: 0},
        name="moe_pad_fill",
    )(fill_lo, fill_hi, xs)


def _experts_kernel(be_ref, nb_ref, brow_ref, next_ref, xs_ref, wg_hbm, wu_hbm, wd_hbm, ys_ref,
                    wgb_ref, wub_ref, wdb_ref, wg_f, wu_f, wd_f, sem, slot_ref):
    del brow_ref
    b = pl.program_id(0)
    new_expert = jnp.logical_or(b == 0, be_ref[b] != be_ref[jnp.maximum(b - 1, 0)])

    def weight_copies(e, s):
        return (pltpu.make_async_copy(wg_hbm.at[e], wg_f.at[s], sem.at[s]),
                pltpu.make_async_copy(wu_hbm.at[e], wu_f.at[s], sem.at[s]),
                pltpu.make_async_copy(wd_hbm.at[e], wd_f.at[s], sem.at[s]))

    @pl.when(b == 0)
    def _():
        slot_ref[0] = 0
        for c in weight_copies(be_ref[0], 0):
            c.start()

    @pl.when(jnp.logical_and(b < nb_ref[0], new_expert))
    def _():
        s = slot_ref[0]
        for c in weight_copies(be_ref[b], s):
            c.wait()

        @pl.when(next_ref[b] != be_ref[b])
        def _():
            for c in weight_copies(next_ref[b], 1 - s):
                c.start()

        wgb_ref[...] = wg_f[s].astype(BF16)
        wub_ref[...] = wu_f[s].astype(BF16)
        wdb_ref[...] = wd_f[s].astype(BF16)
        slot_ref[0] = 1 - s

    @pl.when(b < nb_ref[0])
    def _():
        lo, hi = _unpack_bf16_pairs(xs_ref[...])
        gate = _dot(lo, wgb_ref[:HALF, :]) + _dot(hi, wgb_ref[HALF:, :])
        up = _dot(lo, wub_ref[:HALF, :]) + _dot(hi, wub_ref[HALF:, :])
        hb = (jax.nn.silu(gate) * up).astype(BF16)
        ys_ref[...] = _pack_bf16_pairs(_dot(hb, wdb_ref[...]).astype(BF16))


def _experts(block_e, nb_used, block_row, next_e, xs, wg, wu, wd):
    n_rows = xs.shape[0] // MOE_ROWS * MOE_ROWS
    nb = block_e.shape[0]
    rows = lambda b, be, nbu, brow, nxt: (brow[b], 0)
    hbm = pl.BlockSpec(memory_space=pl.ANY)
    grid_spec = pltpu.PrefetchScalarGridSpec(
        num_scalar_prefetch=4,
        grid=(nb,),
        in_specs=[pl.BlockSpec((MOE_ROWS, HALF), rows), hbm, hbm, hbm],
        out_specs=pl.BlockSpec((MOE_ROWS, HALF), rows),
        scratch_shapes=[pltpu.VMEM((D_MODEL, D_EXPERT), BF16), pltpu.VMEM((D_MODEL, D_EXPERT), BF16),
                        pltpu.VMEM((D_EXPERT, D_MODEL), BF16),
                        pltpu.VMEM((2, D_MODEL, D_EXPERT), F32), pltpu.VMEM((2, D_MODEL, D_EXPERT), F32),
                        pltpu.VMEM((2, D_EXPERT, D_MODEL), F32),
                        pltpu.SemaphoreType.DMA((2,)), pltpu.SMEM((1,), I32)],
    )
    return pl.pallas_call(
        _experts_kernel,
        grid_spec=grid_spec,
        out_shape=jax.ShapeDtypeStruct((n_rows, HALF), U32),
        compiler_params=_params(("arbitrary",)),
        name="moe_experts",
    )(block_e, nb_used, block_row, next_e, xs, wg, wu, wd)


def _combine_kernel(dest_ref, next_ref, w_ref, h_ref, wgs_ref, wus_ref, wds_ref, g_ref, b_ref,
                    ys_ref, o_ref, buf_ref, sem, *, tm):
    i = pl.program_id(0)
    slot = i % 2

    def gather(idx_ref, dst_slot, t):
        for k in range(TOP_K):
            _row_copy(ys_ref, idx_ref[0, k, t], buf_ref.at[dst_slot, k], t, sem.at[dst_slot]).start()

    def wait_tile(s):
        pltpu.make_async_copy(buf_ref.at[s], buf_ref.at[s], sem.at[s]).wait()

    @pl.when(i == 0)
    def _():
        def first(t, c):
            gather(dest_ref, 0, t)
            return c

        lax.fori_loop(0, tm, first, 0)

    wait_tile(slot)
    for t in range(tm):
        gather(next_ref, 1 - slot, t)

    h = h_ref[...]
    hb = h.astype(BF16)
    hid = (jax.nn.silu(_dot(hb, wgs_ref[...])) * _dot(hb, wus_ref[...])).astype(BF16)
    shared = _dot(hid, wds_ref[...])

    w = w_ref[...]
    lo_sum = jnp.zeros((tm, HALF), F32)
    hi_sum = jnp.zeros((tm, HALF), F32)
    for k in range(TOP_K):
        p = buf_ref[slot, k]
        lo_sum = lo_sum + lax.bitcast_convert_type(p << 16, F32) * w[:, k:k + 1]
        hi_sum = hi_sum + lax.bitcast_convert_type(p & jnp.uint32(0xFFFF0000), F32) * w[:, k:k + 1]
    routed = jnp.concatenate([lo_sum, hi_sum], axis=1)
    o_ref[...] = _layer_norm(DN_ALPHA * h + (routed + shared), g_ref[...], b_ref[...])

    @pl.when(i == pl.num_programs(0) - 1)
    def _():
        wait_tile(1 - slot)


def _combine(dest3, w_tok, h, wgs, wus, wds, ln_g, ln_b, ys3):
    T = h.shape[0]
    n_tiles, _, tm = dest3.shape
    kern = functools.partial(_combine_kernel, tm=tm)
    row = lambda i: (i, 0)
    fixed = lambda i: (0, 0)
    return pl.pallas_call(
        kern,
        grid=(n_tiles,),
        in_specs=[pl.BlockSpec((1, TOP_K, tm), lambda i: (i, 0, 0), memory_space=pltpu.SMEM),
                  pl.BlockSpec((1, TOP_K, tm), lambda i: (jnp.minimum(i + 1, n_tiles - 1), 0, 0),
                               memory_space=pltpu.SMEM),
                  pl.BlockSpec((tm, TOP_K), row),
                  pl.BlockSpec((tm, D_MODEL), row),
                  pl.BlockSpec((D_MODEL, D_SHARED), fixed),
                  pl.BlockSpec((D_MODEL, D_SHARED), fixed),
                  pl.BlockSpec((D_SHARED, D_MODEL), fixed),
                  pl.BlockSpec((1, D_MODEL), fixed),
                  pl.BlockSpec((1, D_MODEL), fixed),
                  pl.BlockSpec(memory_space=pl.ANY)],
        out_specs=pl.BlockSpec((tm, D_MODEL), row),
        out_shape=jax.ShapeDtypeStruct((T, D_MODEL), F32),
        scratch_shapes=[pltpu.VMEM((2, TOP_K, tm, HALF), U32), pltpu.SemaphoreType.DMA((2,))],
        compiler_params=_params(("arbitrary",)),
        name="moe_combine",
    )(dest3, dest3, w_tok, h, wgs, wus, wds, ln_g.reshape(1, D_MODEL), ln_b.reshape(1, D_MODEL), ys3)


def _tile(n, pref):
    t = min(n, pref)
    assert n % t == 0, (n, pref)
    return t


def _layer(x, w_in, conv_w, conv_b, lru_wa, lru_ba, lru_wi, lru_bi, lru_lambda, ret_gn_gain,
           w_lru_out, w_ret_out, b_gate, w_o, ln1_g, ln1_b, w_router, router_bias,
           w_gate_e, w_up_e, w_down_e, w_gate_s, w_up_s, w_down_s, ln2_g, ln2_b):
    B, S, D = x.shape
    T = B * S
    x2 = x.reshape(T, D)
    xb = x2.astype(BF16)

    half = RET_DK // 2
    freq = ROPE_THETA ** (-jnp.arange(half, dtype=F32) / half)
    ang = jnp.arange(S, dtype=I32).astype(F32)[:, None] * freq
    cos, sin = jnp.cos(ang), jnp.sin(ang)

    ua, v, q, k = _lru_branch(xb, w_in, cos, sin, conv_w, conv_b, lru_wa, lru_ba, lru_wi, lru_bi,
                              lru_lambda, S, _tile(S, 256), 512)
    ub = _retention(q, k, v, xb, w_in, ret_gn_gain, B, S)

    mixed = _mix(ua, ub, xb, w_lru_out.astype(BF16), w_ret_out.astype(BF16),
                 w_in[:, OFF_GA:].astype(BF16), b_gate, _tile(T, 512), 512)

    tm_r = _tile(T, 256)
    cap = (T + MOE_ROWS - 1) // MOE_ROWS * MOE_ROWS
    h, dest, wsel, cnt, xs = _ln1_route(mixed, w_o.astype(BF16), x2, ln1_g, ln1_b,
                                        w_router.T.astype(BF16), router_bias, tm_r, cap)

    counts = cnt[:, 0].astype(I32)
    e_blocks = (counts + MOE_ROWS - 1) // MOE_ROWS
    blocks_end = jnp.cumsum(e_blocks)
    nb = (T * TOP_K + N_EXPERTS * (MOE_ROWS - 1) + MOE_ROWS - 1) // MOE_ROWS
    nb_used = blocks_end[-1:]
    blk = jnp.minimum(jnp.arange(nb, dtype=I32), nb_used - 1)
    block_e = jnp.minimum(jnp.sum((blocks_end[None, :] <= blk[:, None]).astype(I32), axis=1),
                          N_EXPERTS - 1)
    blocks_start = blocks_end - e_blocks
    e_onehot = block_e[:, None] == jnp.arange(N_EXPERTS, dtype=I32)[None, :]
    block_row = block_e * (cap // MOE_ROWS) + blk - jnp.sum(jnp.where(e_onehot, blocks_start[None, :], 0),
                                                            axis=1)
    e_row0 = jnp.arange(N_EXPERTS, dtype=I32) * cap
    seg_end = jnp.sum(jnp.where(e_onehot, blocks_end[None, :], 0), axis=1)
    after = jnp.minimum(seg_end, nb_used - 1)
    next_e = jnp.sum(jnp.where(after[:, None] == jnp.arange(nb, dtype=I32)[None, :], block_e[None, :], 0),
                     axis=1)

    xs = _pad_fill(e_row0 + counts, e_row0 + e_blocks * MOE_ROWS, xs)
    ys = _experts(block_e, nb_used, block_row, next_e, xs, w_gate_e, w_up_e, w_down_e)
    out = _combine(dest, wsel.T, h, w_gate_s.astype(BF16), w_up_s.astype(BF16),
                   w_down_s.astype(BF16), ln2_g, ln2_b, ys)
    return out.reshape(B, S, D)


def kernel(x, w_in, conv_w, conv_b, lru_wa, lru_ba, lru_wi, lru_bi, lru_lambda, ret_gn_gain, w_lru_out, w_ret_out, b_gate, w_o, ln1_g, ln1_b, w_router, router_bias, w_gate_e, w_up_e, w_down_e, w_gate_s, w_up_s, w_down_s, ln2_g, ln2_b):
    assert DEPTH == 1 and w_in.shape[0] == DEPTH
    args = (w_in, conv_w, conv_b, lru_wa, lru_ba, lru_wi, lru_bi, lru_lambda, ret_gn_gain,
            w_lru_out, w_ret_out, b_gate, w_o, ln1_g, ln1_b, w_router, router_bias,
            w_gate_e, w_up_e, w_down_e, w_gate_s, w_up_s, w_down_s, ln2_g, ln2_b)
    return _layer(x, *[a[0] for a in args])
```

```python
import functools

import jax
import jax.numpy as jnp
from jax import lax
from jax.experimental import pallas as pl
from jax.experimental.pallas import tpu as pltpu

F32 = jnp.float32
BF16 = jnp.bfloat16
I32 = jnp.int32
U32 = jnp.uint32

D_MODEL = 2048
D_LRU = 2048
LRU_HEADS = 16
LRU_BLOCK = D_LRU // LRU_HEADS
CONV_WIDTH = 4
LRU_C = 8.0
RET_HEADS = 8
RET_DK = 256
RET_DV = 512
D_QK = RET_HEADS * RET_DK
D_RV = RET_HEADS * RET_DV
ROPE_THETA = 10000.0
N_EXPERTS = 64
TOP_K = 8
N_GROUPS = 8
GROUP_SIZE = N_EXPERTS // N_GROUPS
TOPK_GROUPS = 4
D_EXPERT = 512
D_SHARED = 512
ROUTED_SCALE = 2.5
DEPTH = 1
DN_ALPHA = (2.0 * DEPTH) ** 0.25
LN_EPS = 1e-5

OFF_LRU_X = 0
OFF_LRU_Y = OFF_LRU_X + D_LRU
OFF_Q = OFF_LRU_Y + D_LRU
OFF_K = OFF_Q + D_QK
OFF_V = OFF_K + D_QK
OFF_G = OFF_V + D_RV
OFF_GA = OFF_G + D_RV
OFF_GB = OFF_GA + D_MODEL

V7X_VMEM_LIMIT = 56 * 1024 * 1024
LANES = 128
SUBLANES = 8
SUBLANE_BITS = SUBLANES.bit_length() - 1
MOE_ROWS = 512
RET_CHUNK = 256
LRU_SUB = 2
HALF = D_MODEL // 2


def _params(sem):
    return pltpu.CompilerParams(dimension_semantics=sem, vmem_limit_bytes=V7X_VMEM_LIMIT)


def _dot(a, b):
    return jnp.dot(a, b, preferred_element_type=F32)


def _lru_kernel(x_ref, wx_ref, wy_ref, wv_ref, wq_ref, wk_ref, cos_ref, sin_ref, cw_ref, cb_ref,
                wai_ref, ba_ref, bi_ref, lam_ref,
                o_ref, v_ref, q_ref, k_ref,
                wxb_ref, wyb_ref, wvb_ref, wqb_ref, wkb_ref, perm_ref, unperm_ref, tail_ref, hc_ref,
                a_ref, b_ref, gy_ref, *, tiles_per_seq, tm, width):
    i = pl.program_id(1)
    seg = tm // SUBLANES
    halo = (CONV_WIDTH - 1) * SUBLANES

    @pl.when(i == 0)
    def _():
        wxb_ref[...] = wx_ref[...].astype(BF16)
        wyb_ref[...] = wy_ref[...].astype(BF16)
        wvb_ref[...] = wv_ref[...].astype(BF16)
        wqb_ref[...] = wq_ref[...].astype(BF16)
        wkb_ref[...] = wk_ref[...].astype(BF16)
        r = lax.broadcasted_iota(I32, (tm, tm), 0)
        c = lax.broadcasted_iota(I32, (tm, tm), 1)
        r_time = (r & (SUBLANES - 1)) * seg + (r >> SUBLANE_BITS)
        c_time = (c & (SUBLANES - 1)) * seg + (c >> SUBLANE_BITS)
        perm_ref[...] = jnp.where(c == r_time, 1.0, 0.0).astype(BF16)
        unperm_ref[...] = jnp.where(r == c_time, 1.0, 0.0).astype(BF16)
        tail_ref[...] = jnp.zeros_like(tail_ref)
        hc_ref[...] = jnp.zeros_like(hc_ref)

    cw = cw_ref[...]
    cb = cb_ref[...]
    taps = [cw[CONV_WIDTH - 1 - d:CONV_WIDTH - d] for d in range(CONV_WIDTH)]
    sub = lax.broadcasted_iota(I32, (SUBLANES, width), 0)
    row = lax.broadcasted_iota(I32, (tm, width), 0)
    v_piece = v_ref.shape[1] // 4

    def one_tile(u, tail, carry):
        rows_u = slice(u * tm, (u + 1) * tm)
        first = ((i * LRU_SUB + u) % tiles_per_seq) == 0
        tail = jnp.where(first, 0.0, tail)
        carry = jnp.where(first, 0.0, carry)
        x = x_ref[rows_u, :]

        def emit_v(piece):
            cols = slice(piece * v_piece, (piece + 1) * v_piece)
            v_ref[rows_u, cols] = _dot(x, wvb_ref[:, cols]).astype(v_ref.dtype)

        def emit_rotary(wb_ref, out_ref, scale):
            acc = _dot(x, wb_ref[...])
            cos = cos_ref[rows_u, :]
            sin = sin_ref[rows_u, :]
            half = RET_DK // 2
            for hh in range(width // RET_DK):
                c0 = hh * RET_DK
                t1 = acc[:, c0:c0 + half]
                t2 = acc[:, c0 + half:c0 + RET_DK]
                out_ref[rows_u, c0:c0 + half] = ((t1 * cos - t2 * sin) * scale).astype(out_ref.dtype)
                out_ref[rows_u, c0 + half:c0 + RET_DK] = (
                    (t1 * sin + t2 * cos) * scale).astype(out_ref.dtype)

        emit_v(0)
        xp = _dot(perm_ref[...], x).astype(BF16)
        px = _dot(xp, wxb_ref[...])
        gy_ref[u] = jax.nn.gelu(_dot(xp, wyb_ref[...]), approximate=True)

        xa = cb + taps[0] * px
        for d in range(1, CONV_WIDTH):
            back = jnp.concatenate([px[tm - d * SUBLANES:], px[:tm - d * SUBLANES]], axis=0)
            xa = xa + taps[d] * back
        groups = {}
        for g in range(CONV_WIDTH - 1):
            rows = slice(g * SUBLANES, (g + 1) * SUBLANES)
            cur_end = px[tm - halo + g * SUBLANES:tm - halo + (g + 1) * SUBLANES]
            groups[g - (CONV_WIDTH - 1)] = jnp.where(sub == 0, pltpu.roll(tail[rows, :], 1, 0),
                                                     pltpu.roll(cur_end, 1, 0))
            groups[g] = px[rows, :]
        head = []
        for q in range(CONV_WIDTH - 1):
            acc = cb + taps[0] * groups[q]
            for d in range(1, CONV_WIDTH):
                acc = acc + taps[d] * groups[q - d]
            head.append(acc)
        xa = jnp.concatenate(head + [xa[halo:]], axis=0)
        tail_out = px[tm - halo:]
        emit_v(1)

        xab = xa.astype(BF16)
        r_parts, i_parts = [], []
        for hh in range(width // LRU_BLOCK):
            both = _dot(xab[:, hh * LRU_BLOCK:(hh + 1) * LRU_BLOCK], wai_ref[hh])
            r_parts.append(both[:, :LRU_BLOCK])
            i_parts.append(both[:, LRU_BLOCK:])
        rg = jax.nn.sigmoid(jnp.concatenate(r_parts, axis=1) + ba_ref[...])
        ig = jax.nn.sigmoid(jnp.concatenate(i_parts, axis=1) + bi_ref[...])
        emit_rotary(wqb_ref, q_ref, 1.0)

        log_a = -LRU_C * rg * jax.nn.softplus(-lam_ref[...])
        a = jnp.exp(log_a)
        mult = jnp.sqrt(-jnp.tanh(log_a) * (a * a + 1.0))
        mult = jnp.where(jnp.logical_and(first, row == 0), 1.0, mult)
        a_ref[u] = a
        b_ref[u] = mult * ig * xa
        emit_v(2)

        run_a = jnp.ones((SUBLANES, width), F32)
        run_h = jnp.zeros((SUBLANES, width), F32)
        for q in range(seg):
            rows = slice(q * SUBLANES, (q + 1) * SUBLANES)
            aq = a_ref[u, rows, :]
            run_h = aq * run_h + b_ref[u, rows, :]
            run_a = aq * run_a
            b_ref[u, rows, :] = run_h
            a_ref[u, rows, :] = run_a
        emit_v(3)

        carries = []
        for s in range(SUBLANES):
            carries.append(carry)
            carry = run_a[s:s + 1, :] * carry + run_h[s:s + 1, :]
        carry_in = jnp.concatenate(carries, axis=0)
        for q in range(seg):
            rows = slice(q * SUBLANES, (q + 1) * SUBLANES)
            b_ref[u, rows, :] = (b_ref[u, rows, :] + a_ref[u, rows, :] * carry_in) * gy_ref[u, rows, :]

        emit_rotary(wkb_ref, k_ref, RET_DK ** -0.5)
        o_ref[rows_u, :] = _dot(unperm_ref[...], b_ref[u].astype(BF16)).astype(o_ref.dtype)
        return tail_out, carry

    tail = tail_ref[...]
    carry = hc_ref[...]
    for u in range(LRU_SUB):
        tail, carry = one_tile(u, tail, carry)
    tail_ref[...] = tail
    hc_ref[...] = carry


def _lru_branch(xb, w, cos, sin, conv_w, conv_b, wa, ba, wi, bi, lam, seq, tm, width):
    T, K = xb.shape
    nj = D_LRU // width
    assert D_QK == D_LRU, "q / k column tiles are walked together with the lru tiles"
    jy0 = OFF_LRU_Y // width
    jq0 = OFF_Q // width
    jk0 = OFF_K // width
    hp = width // LRU_BLOCK
    vw = D_RV // nj
    jv0 = OFF_V // vw
    tiles_per_seq = seq // tm
    rows = LRU_SUB * tm
    reps = max(1, rows // seq)
    cos, sin = jnp.tile(cos, (reps, 1)), jnp.tile(sin, (reps, 1))
    table_blocks = cos.shape[0] // rows
    kern = functools.partial(_lru_kernel, tiles_per_seq=tiles_per_seq, tm=tm, width=width)
    vec = lambda j, i: (0, j)
    once = pl.Buffered(1)
    table = pl.BlockSpec((rows, RET_DK // 2), lambda j, i: (i % table_blocks, 0))
    return pl.pallas_call(
        kern,
        grid=(nj, T // rows),
        in_specs=[pl.BlockSpec((rows, K), lambda j, i: (i, 0)),
                  pl.BlockSpec((K, width), lambda j, i: (0, j), pipeline_mode=once),
                  pl.BlockSpec((K, width), lambda j, i: (0, jy0 + j), pipeline_mode=once),
                  pl.BlockSpec((K, vw), lambda j, i: (0, jv0 + j), pipeline_mode=once),
                  pl.BlockSpec((K, width), lambda j, i: (0, jq0 + j), pipeline_mode=once),
                  pl.BlockSpec((K, width), lambda j, i: (0, jk0 + j), pipeline_mode=once),
                  table, table,
                  pl.BlockSpec((CONV_WIDTH, width), vec),
                  pl.BlockSpec((1, width), vec),
                  pl.BlockSpec((hp, LRU_BLOCK, 2 * LRU_BLOCK), lambda j, i: (j, 0, 0)),
                  pl.BlockSpec((1, width), vec),
                  pl.BlockSpec((1, width), vec),
                  pl.BlockSpec((1, width), vec)],
        out_specs=[pl.BlockSpec((rows, width), lambda j, i: (i, j)),
                   pl.BlockSpec((rows, vw), lambda j, i: (i, j)),
                   pl.BlockSpec((rows, width), lambda j, i: (i, j)),
                   pl.BlockSpec((rows, width), lambda j, i: (i, j))],
        out_shape=[jax.ShapeDtypeStruct((T, D_LRU), BF16), jax.ShapeDtypeStruct((T, D_RV), BF16),
                   jax.ShapeDtypeStruct((T, D_QK), BF16), jax.ShapeDtypeStruct((T, D_QK), BF16)],
        scratch_shapes=[pltpu.VMEM((K, width), BF16), pltpu.VMEM((K, width), BF16),
                        pltpu.VMEM((K, vw), BF16),
                        pltpu.VMEM((K, width), BF16), pltpu.VMEM((K, width), BF16),
                        pltpu.VMEM((tm, tm), BF16), pltpu.VMEM((tm, tm), BF16),
                        pltpu.VMEM(((CONV_WIDTH - 1) * SUBLANES, width), F32),
                        pltpu.VMEM((1, width), F32),
                        pltpu.VMEM((LRU_SUB, tm, width), F32), pltpu.VMEM((LRU_SUB, tm, width), F32),
                        pltpu.VMEM((LRU_SUB, tm, width), F32)],
        compiler_params=_params(("parallel", "arbitrary")),
        name="lru_branch",
    )(xb, w, w, w, w, w, cos, sin, conv_w, conv_b.reshape(1, D_LRU),
      jnp.concatenate([wa, wi], axis=-1).astype(BF16), ba.reshape(1, D_LRU), bi.reshape(1, D_LRU),
      lam.reshape(1, D_LRU))


def _ret_kernel(gc_ref, q_ref, k_ref, v_ref, x_ref, w_hbm, gain_ref, dm_ref, xi_ref, zeta_ref, o_ref,
                st_ref, wg_ref, stage_ref, wg_sem):
    @pl.when(jnp.logical_and(pl.program_id(0) == 0, pl.program_id(1) == 0))
    def _():
        for hd in range(RET_HEADS):
            cols = pl.ds(OFF_G + hd * RET_DV, RET_DV)
            fetch = pltpu.make_async_copy(w_hbm.at[:, cols], stage_ref, wg_sem)
            fetch.start()
            fetch.wait()
            wg_ref[:, hd * RET_DV:(hd + 1) * RET_DV] = stage_ref[...].astype(BF16)

    @pl.when(pl.program_id(1) == 0)
    def _():
        st_ref[...] = jnp.zeros_like(st_ref)

    x = x_ref[...]
    for hd in range(RET_HEADS):
        qc = slice(hd * RET_DK, (hd + 1) * RET_DK)
        vc = slice(hd * RET_DV, (hd + 1) * RET_DV)
        q = q_ref[:, qc]
        k = k_ref[:, qc]
        v = v_ref[:, vc]
        st = st_ref[hd]
        s = lax.dot_general(q, k, (((1,), (1,)), ((), ())), preferred_element_type=F32) * dm_ref[hd]
        o = _dot(s.astype(BF16), v) + _dot(q, st.astype(BF16)) * xi_ref[hd]
        kz = (k.astype(F32) * zeta_ref[hd]).astype(BF16)
        st_ref[hd] = gc_ref[hd] * st + lax.dot_general(kz, v, (((0,), (0,)), ((), ())),
                                                       preferred_element_type=F32)
        mu = jnp.mean(o, axis=-1, keepdims=True)
        oc = o - mu
        var = jnp.mean(oc * oc, axis=-1, keepdims=True)
        oh = oc * lax.rsqrt(var + LN_EPS) * gain_ref[:, vc]
        g = _dot(x, wg_ref[:, vc])
        o_ref[:, vc] = (jax.nn.silu(g) * oh).astype(o_ref.dtype)


def _retention(q, k, v, xb, w_in, gain, batch, seq):
    T = batch * seq
    C = min(RET_CHUNK, seq)
    nchunk = seq // C
    H = RET_HEADS
    log_g = jnp.log1p(-jnp.exp2(-5.0 - jnp.arange(H, dtype=F32)))
    idx = jnp.arange(C, dtype=F32)
    diff = idx[:, None] - idx[None, :]
    dmask = jnp.where(diff >= 0, jnp.exp(jnp.maximum(diff, 0.0)[None] * log_g[:, None, None]), 0.0)
    xi = jnp.exp((idx[None] + 1.0) * log_g[:, None])[:, :, None]
    zeta = jnp.exp((C - 1.0 - idx[None]) * log_g[:, None])[:, :, None]
    g_c = jnp.exp(C * log_g)
    rows = lambda b, n: b * nchunk + n
    whole3 = lambda b, n: (0, 0, 0)
    return pl.pallas_call(
        _ret_kernel,
        grid=(batch, nchunk),
        in_specs=[pl.BlockSpec(memory_space=pltpu.SMEM),
                  pl.BlockSpec((C, D_QK), lambda b, n: (rows(b, n), 0)),
                  pl.BlockSpec((C, D_QK), lambda b, n: (rows(b, n), 0)),
                  pl.BlockSpec((C, D_RV), lambda b, n: (rows(b, n), 0)),
                  pl.BlockSpec((C, D_MODEL), lambda b, n: (rows(b, n), 0)),
                  pl.BlockSpec(memory_space=pl.ANY),
                  pl.BlockSpec((1, D_RV), lambda b, n: (0, 0)),
                  pl.BlockSpec((H, C, C), whole3),
                  pl.BlockSpec((H, C, 1), whole3),
                  pl.BlockSpec((H, C, 1), whole3)],
        out_specs=pl.BlockSpec((C, D_RV), lambda b, n: (rows(b, n), 0)),
        out_shape=jax.ShapeDtypeStruct((T, D_RV), BF16),
        scratch_shapes=[pltpu.VMEM((H, RET_DK, RET_DV), F32), pltpu.VMEM((D_MODEL, D_RV), BF16),
                        pltpu.VMEM((D_MODEL, RET_DV), F32), pltpu.SemaphoreType.DMA],
        compiler_params=_params(("arbitrary", "arbitrary")),
        name="retention",
    )(g_c, q, k, v, xb, w_in, gain.reshape(1, D_RV), dmask, xi, zeta)


def _mix_kernel(ua_ref, ub_ref, x_ref, wlo_ref, wro_ref, wga_ref, wgb_ref, bga_ref, bgb_ref, o_ref):
    x = x_ref[...]
    ya = _dot(ua_ref[...], wlo_ref[...])
    yb = _dot(ub_ref[...], wro_ref[...])
    ga = jax.nn.sigmoid(_dot(x, wga_ref[...]) + bga_ref[...])
    gb = jax.nn.sigmoid(_dot(x, wgb_ref[...]) + bgb_ref[...])
    o_ref[...] = (ga * ya + gb * yb).astype(o_ref.dtype)


def _mix(ua, ub, xb, w_lru_out, w_ret_out, w_gates, b_gate, tm, tn):
    T = xb.shape[0]
    nj = D_MODEL // tn
    return pl.pallas_call(
        _mix_kernel,
        grid=(T // tm, nj),
        in_specs=[pl.BlockSpec((tm, D_LRU), lambda i, j: (i, 0)),
                  pl.BlockSpec((tm, D_RV), lambda i, j: (i, 0)),
                  pl.BlockSpec((tm, D_MODEL), lambda i, j: (i, 0)),
                  pl.BlockSpec((D_LRU, tn), lambda i, j: (0, j)),
                  pl.BlockSpec((D_RV, tn), lambda i, j: (0, j)),
                  pl.BlockSpec((D_MODEL, tn), lambda i, j: (0, j)),
                  pl.BlockSpec((D_MODEL, tn), lambda i, j: (0, nj + j)),
                  pl.BlockSpec((1, tn), lambda i, j: (0, j)),
                  pl.BlockSpec((1, tn), lambda i, j: (0, nj + j))],
        out_specs=pl.BlockSpec((tm, tn), lambda i, j: (i, j)),
        out_shape=jax.ShapeDtypeStruct((T, D_MODEL), BF16),
        compiler_params=_params(("parallel", "parallel")),
        name="gated_mix",
    )(ua, ub, xb, w_lru_out, w_ret_out, w_gates, w_gates, b_gate.reshape(1, 2 * D_MODEL),
      b_gate.reshape(1, 2 * D_MODEL))


def _layer_norm(y, g, b):
    mu = jnp.mean(y, axis=-1, keepdims=True)
    yc = y - mu
    var = jnp.mean(yc * yc, axis=-1, keepdims=True)
    return yc * lax.rsqrt(var + LN_EPS) * g + b


def _pack_bf16_pairs(hb):
    lo = lax.bitcast_convert_type(hb[:, :HALF].astype(F32), U32)
    hi = lax.bitcast_convert_type(hb[:, HALF:].astype(F32), U32)
    return (hi & jnp.uint32(0xFFFF0000)) | (lo >> 16)


def _unpack_bf16_pairs(p):
    lo = lax.bitcast_convert_type(p << 16, F32).astype(BF16)
    hi = lax.bitcast_convert_type(p & jnp.uint32(0xFFFF0000), F32).astype(BF16)
    return lo, hi


def _max2(a):
    return jnp.max(jnp.max(a, axis=1, keepdims=True), axis=0, keepdims=True)


def _min2(a):
    return jnp.min(jnp.min(a, axis=1, keepdims=True), axis=0, keepdims=True)


def _sum2(a):
    return jnp.sum(jnp.sum(a, axis=1, keepdims=True), axis=0, keepdims=True)


def _ln1_route_kernel(mix_ref, wo_ref, x_ref, g_ref, b_ref, wr_ref, rb_ref,
                      h_ref, dest_ref, wsel_ref, cnt_ref, xs_ref,
                      cnt_sc, base_sc, hp_sc, dest_v, dest_s, row_sem, idx_sem, *, tm, n_tiles, cap):
    i = pl.program_id(0)
    slot = i % 2
    prev = 1 - slot

    def wait_rows(s):
        for _ in range(TOP_K):
            pltpu.make_async_copy(hp_sc.at[s], hp_sc.at[s], row_sem).wait()

    @pl.when(i == 0)
    def _():
        cnt_sc[...] = jnp.zeros_like(cnt_sc)
        base_sc[...] = jnp.zeros_like(base_sc)
        hp_sc[1] = jnp.zeros((tm, HALF), U32)

        def spare(t, c):
            for k in range(TOP_K):
                dest_s[1, k, t] = N_EXPERTS * cap + k
            return c

        lax.fori_loop(0, tm, spare, 0)

    @pl.when(i > 0)
    def _():
        wait_rows(slot)

    for t in range(tm):
        for k in range(TOP_K):
            _row_copy(hp_sc.at[prev], t, xs_ref, dest_s[prev, k, t], row_sem).start()

    y = DN_ALPHA * x_ref[...] + _dot(mix_ref[...], wo_ref[...])
    h = _layer_norm(y, g_ref[...], b_ref[...])
    h_ref[...] = h
    hb = h.astype(BF16)
    hp_sc[slot] = _pack_bf16_pairs(hb)

    G, GS = N_GROUPS, GROUP_SIZE
    logits = lax.dot_general(wr_ref[...], hb, (((1,), (1,)), ((), ())), preferred_element_type=F32)
    scores = jax.nn.sigmoid(logits)
    s3 = scores.reshape(G, GS, tm)
    b3 = (scores + rb_ref[...]).reshape(G, GS, tm)
    neg = jnp.float32(-jnp.inf)

    in_grp = lax.broadcasted_iota(I32, (G, GS, tm), 1)
    m1 = jnp.max(b3, axis=1, keepdims=True)
    f1 = jnp.min(jnp.where(b3 == m1, in_grp, GS), axis=1, keepdims=True)
    m2 = jnp.max(jnp.where(in_grp == f1, neg, b3), axis=1, keepdims=True)
    grp = m1 + m2

    gi = lax.broadcasted_iota(I32, (G, 1, tm), 0)
    gkeep = jnp.zeros((G, 1, tm), F32)
    cur = grp
    for _ in range(TOPK_GROUPS):
        m = jnp.max(cur, axis=0, keepdims=True)
        f = jnp.min(jnp.where(cur == m, gi, G), axis=0, keepdims=True)
        hit = gi == f
        gkeep = jnp.where(hit, 1.0, gkeep)
        cur = jnp.where(hit, neg, cur)
    emask = jnp.broadcast_to(gkeep, (G, GS, tm)) > 0.0

    ei = lax.broadcasted_iota(I32, (G, GS, tm), 0) * GS + in_grp
    cur = jnp.where(emask, b3, neg)
    hits, idxs, ws = [], [], []
    for _ in range(TOP_K):
        m = _max2(cur)
        f = _min2(jnp.where(cur == m, ei, N_EXPERTS))
        hit = ei == f
        cur = jnp.where(hit, neg, cur)
        hits.append(hit)
        idxs.append(f)
        ws.append(_sum2(jnp.where(hit, s3, 0.0)))
    wsum = ws[0]
    for r in range(1, TOP_K):
        wsum = wsum + ws[r]

    sel = jnp.zeros((G, GS, tm), F32)
    for hit in hits:
        sel = jnp.where(hit, 1.0, sel)
    sel2 = sel.reshape(N_EXPERTS, tm)
    tri = (lax.broadcasted_iota(I32, (tm, tm), 0) < lax.broadcasted_iota(I32, (tm, tm), 1))
    rank_local = _dot(sel2.astype(BF16), jnp.where(tri, 1.0, 0.0).astype(BF16))
    base = jnp.where(i < n_tiles, cnt_sc[...], base_sc[...])
    base_sc[...] = base
    rank3 = (rank_local + base[:, 0:1]).reshape(G, GS, tm)
    cnt_new = base + jnp.sum(sel2, axis=1, keepdims=True)
    cnt_sc[...] = cnt_new
    cnt_ref[...] = cnt_new

    for r in range(TOP_K):
        rank_r = _sum2(jnp.where(hits[r], rank3, 0.0)).reshape(1, tm).astype(I32)
        dest_v[r:r + 1, :] = idxs[r].reshape(1, tm) * cap + rank_r
        wsel_ref[r:r + 1, :] = (ws[r] / wsum * ROUTED_SCALE).reshape(1, tm)
    dest_ref[0] = dest_v[...]

    to_smem = pltpu.make_async_copy(dest_v, dest_s.at[slot], idx_sem)
    to_smem.start()
    to_smem.wait()

    @pl.when(i == n_tiles)
    def _():
        wait_rows(prev)


def _ln1_route(mixed, w_o, x2, ln_g, ln_b, w_router_t, router_bias, tm, cap):
    T = x2.shape[0]
    n_tiles = T // tm
    kern = functools.partial(_ln1_route_kernel, tm=tm, n_tiles=n_tiles, cap=cap)
    tile = lambda i: jnp.minimum(i, n_tiles - 1)
    row = lambda i: (tile(i), 0)
    fixed = lambda i: (0, 0)
    return pl.pallas_call(
        kern,
        grid=(n_tiles + 1,),
        in_specs=[pl.BlockSpec((tm, D_MODEL), row),
                  pl.BlockSpec((D_MODEL, D_MODEL), fixed),
                  pl.BlockSpec((tm, D_MODEL), row),
                  pl.BlockSpec((1, D_MODEL), fixed),
                  pl.BlockSpec((1, D_MODEL), fixed),
                  pl.BlockSpec((N_EXPERTS, D_MODEL), fixed),
                  pl.BlockSpec((N_EXPERTS, 1), fixed)],
        out_specs=[pl.BlockSpec((tm, D_MODEL), row),
                   pl.BlockSpec((1, TOP_K, tm), lambda i: (tile(i), 0, 0)),
                   pl.BlockSpec((TOP_K, tm), lambda i: (0, tile(i))),
                   pl.BlockSpec((N_EXPERTS, LANES), fixed),
                   pl.BlockSpec(memory_space=pl.ANY)],
        out_shape=[jax.ShapeDtypeStruct((T, D_MODEL), F32),
                   jax.ShapeDtypeStruct((n_tiles, TOP_K, tm), I32),
                   jax.ShapeDtypeStruct((TOP_K, T), F32),
                   jax.ShapeDtypeStruct((N_EXPERTS, LANES), F32),
                   jax.ShapeDtypeStruct((N_EXPERTS * cap + SUBLANES, HALF), U32)],
        scratch_shapes=[pltpu.VMEM((N_EXPERTS, LANES), F32), pltpu.VMEM((N_EXPERTS, LANES), F32),
                        pltpu.VMEM((2, tm, HALF), U32),
                        pltpu.VMEM((TOP_K, tm), I32),
                        pltpu.SMEM((2, TOP_K, tm), I32),
                        pltpu.SemaphoreType.DMA, pltpu.SemaphoreType.DMA],
        compiler_params=_params(("arbitrary",)),
        name="ln1_route",
    )(mixed, w_o, x2, ln_g.reshape(1, D_MODEL), ln_b.reshape(1, D_MODEL), w_router_t,
      router_bias.reshape(N_EXPERTS, 1))


def _row_copy(src_ref, src_row, dst_ref, dst_row, sem):
    return pltpu.make_async_copy(src_ref.at[pl.ds(src_row, 1)], dst_ref.at[pl.ds(dst_row, 1)], sem)


def _pad_fill_kernel(lo_ref, hi_ref, xs_in_ref, xs_ref, zero_ref, sem):
    del xs_in_ref
    zero_ref[...] = jnp.zeros_like(zero_ref)

    def group_copy(r):
        return pltpu.make_async_copy(zero_ref, xs_ref.at[pl.ds(pl.multiple_of(r, SUBLANES), SUBLANES)], sem)

    def fill_expert(e, counts):
        n_rows, n_groups = counts
        lo = lo_ref[e]
        hi = hi_ref[e]
        mid = jnp.minimum((lo + SUBLANES - 1) // SUBLANES * SUBLANES, hi)
        groups = (hi - mid) // SUBLANES

        def fill_row(r, c):
            _row_copy(zero_ref, 0, xs_ref, r, sem).start()
            return c

        def fill_group(g, c):
            group_copy(mid + g * SUBLANES).start()
            return c

        lax.fori_loop(lo, mid, fill_row, 0)
        lax.fori_loop(0, groups, fill_group, 0)
        return n_rows + (mid - lo), n_groups + groups

    n_rows, n_groups = lax.fori_loop(0, N_EXPERTS, fill_expert, (0, 0))

    def drain_row(r, c):
        _row_copy(zero_ref, 0, xs_ref, 0, sem).wait()
        return c

    def drain_group(g, c):
        group_copy(0).wait()
        return c

    lax.fori_loop(0, n_rows, drain_row, 0)
    lax.fori_loop(0, n_groups, drain_group, 0)


def _pad_fill(fill_lo, fill_hi, xs):
    smem = pl.BlockSpec(memory_space=pltpu.SMEM)
    return pl.pallas_call(
        _pad_fill_kernel,
        in_specs=[smem, smem, pl.BlockSpec(memory_space=pl.ANY)],
        out_specs=pl.BlockSpec(memory_space=pl.ANY),
        out_shape=jax.ShapeDtypeStruct(xs.shape, xs.dtype),
        scratch_shapes=[pltpu.VMEM((SUBLANES, HALF), U32), pltpu.SemaphoreType.DMA],
        input_output_aliases={2: 0},
        name="moe_pad_fill",
    )(fill_lo, fill_hi, xs)


def _experts_kernel(be_ref, nb_ref, brow_ref, next_ref, xs_ref, wg_hbm, wu_hbm, wd_hbm, ys_ref,
                    wgb_ref, wub_ref, wdb_ref, wg_f, wu_f, wd_f, sem, slot_ref):
    del brow_ref
    b = pl.program_id(0)
    new_expert = jnp.logical_or(b == 0, be_ref[b] != be_ref[jnp.maximum(b - 1, 0)])

    def weight_copies(e, s):
        return (pltpu.make_async_copy(wg_hbm.at[e], wg_f.at[s], sem.at[s]),
                pltpu.make_async_copy(wu_hbm.at[e], wu_f.at[s], sem.at[s]),
                pltpu.make_async_copy(wd_hbm.at[e], wd_f.at[s], sem.at[s]))

    @pl.when(b == 0)
    def _():
        slot_ref[0] = 0
        for c in weight_copies(be_ref[0], 0):
            c.start()

    @pl.when(jnp.logical_and(b < nb_ref[0], new_expert))
    def _():
        s = slot_ref[0]
        for c in weight_copies(be_ref[b], s):
            c.wait()

        @pl.when(next_ref[b] != be_ref[b])
        def _():
            for c in weight_copies(next_ref[b], 1 - s):
                c.start(priority=1)

        wgb_ref[...] = wg_f[s].astype(BF16)
        wub_ref[...] = wu_f[s].astype(BF16)
        wdb_ref[...] = wd_f[s].astype(BF16)
        slot_ref[0] = 1 - s

    @pl.when(b < nb_ref[0])
    def _():
        lo, hi = _unpack_bf16_pairs(xs_ref[...])
        gate = _dot(lo, wgb_ref[:HALF, :]) + _dot(hi, wgb_ref[HALF:, :])
        up = _dot(lo, wub_ref[:HALF, :]) + _dot(hi, wub_ref[HALF:, :])
        hb = (jax.nn.silu(gate) * up).astype(BF16)
        ys_ref[...] = _pack_bf16_pairs(_dot(hb, wdb_ref[...]).astype(BF16))


def _experts(block_e, nb_used, block_row, next_e, xs, wg, wu, wd):
    n_rows = xs.shape[0] // MOE_ROWS * MOE_ROWS
    nb = block_e.shape[0]
    rows = lambda b, be, nbu, brow, nxt: (brow[b], 0)
    hbm = pl.BlockSpec(memory_space=pl.ANY)
    grid_spec = pltpu.PrefetchScalarGridSpec(
        num_scalar_prefetch=4,
        grid=(nb,),
        in_specs=[pl.BlockSpec((MOE_ROWS, HALF), rows), hbm, hbm, hbm],
        out_specs=pl.BlockSpec((MOE_ROWS, HALF), rows),
        scratch_shapes=[pltpu.VMEM((D_MODEL, D_EXPERT), BF16), pltpu.VMEM((D_MODEL, D_EXPERT), BF16),
                        pltpu.VMEM((D_EXPERT, D_MODEL), BF16),
                        pltpu.VMEM((2, D_MODEL, D_EXPERT), F32), pltpu.VMEM((2, D_MODEL, D_EXPERT), F32),
                        pltpu.VMEM((2, D_EXPERT, D_MODEL), F32),
                        pltpu.SemaphoreType.DMA((2,)), pltpu.SMEM((1,), I32)],
    )
    return pl.pallas_call(
        _experts_kernel,
        grid_spec=grid_spec,
        out_shape=jax.ShapeDtypeStruct((n_rows, HALF), U32),
        compiler_params=_params(("arbitrary",)),
        name="moe_experts",
    )(block_e, nb_used, block_row, next_e, xs, wg, wu, wd)


def _combine_kernel(dest_ref, next_ref, w_ref, h_ref, wgs_ref, wus_ref, wds_ref, g_ref, b_ref,
                    ys_ref, o_ref, buf_ref, sem, *, tm):
    i = pl.program_id(0)
    slot = i % 2

    def gather(idx_ref, dst_slot, t):
        for k in range(TOP_K):
            _row_copy(ys_ref, idx_ref[0, k, t], buf_ref.at[dst_slot, k], t, sem.at[dst_slot]).start()

    def wait_tile(s):
        pltpu.make_async_copy(buf_ref.at[s], buf_ref.at[s], sem.at[s]).wait()

    @pl.when(i == 0)
    def _():
        def first(t, c):
            gather(dest_ref, 0, t)
            return c

        lax.fori_loop(0, tm, first, 0)

    wait_tile(slot)
    for t in range(tm):
        gather(next_ref, 1 - slot, t)

    h = h_ref[...]
    hb = h.astype(BF16)
    hid = (jax.nn.silu(_dot(hb, wgs_ref[...])) * _dot(hb, wus_ref[...])).astype(BF16)
    shared = _dot(hid, wds_ref[...])

    w = w_ref[...]
    lo_sum = jnp.zeros((tm, HALF), F32)
    hi_sum = jnp.zeros((tm, HALF), F32)
    for k in range(TOP_K):
        p = buf_ref[slot, k]
        lo_sum = lo_sum + lax.bitcast_convert_type(p << 16, F32) * w[:, k:k + 1]
        hi_sum = hi_sum + lax.bitcast_convert_type(p & jnp.uint32(0xFFFF0000), F32) * w[:, k:k + 1]
    routed = jnp.concatenate([lo_sum, hi_sum], axis=1)
    o_ref[...] = _layer_norm(DN_ALPHA * h + (routed + shared), g_ref[...], b_ref[...])

    @pl.when(i == pl.num_programs(0) - 1)
    def _():
        wait_tile(1 - slot)


def _combine(dest3, w_tok, h, wgs, wus, wds, ln_g, ln_b, ys3):
    T = h.shape[0]
    n_tiles, _, tm = dest3.shape
    kern = functools.partial(_combine_kernel, tm=tm)
    row = lambda i: (i, 0)
    fixed = lambda i: (0, 0)
    return pl.pallas_call(
        kern,
        grid=(n_tiles,),
        in_specs=[pl.BlockSpec((1, TOP_K, tm), lambda i: (i, 0, 0), memory_space=pltpu.SMEM),
                  pl.BlockSpec((1, TOP_K, tm), lambda i: (jnp.minimum(i + 1, n_tiles - 1), 0, 0),
                               memory_space=pltpu.SMEM),
                  pl.BlockSpec((tm, TOP_K), row),
                  pl.BlockSpec((tm, D_MODEL), row),
                  pl.BlockSpec((D_MODEL, D_SHARED), fixed),
                  pl.BlockSpec((D_MODEL, D_SHARED), fixed),
                  pl.BlockSpec((D_SHARED, D_MODEL), fixed),
                  pl.BlockSpec((1, D_MODEL), fixed),
                  pl.BlockSpec((1, D_MODEL), fixed),
                  pl.BlockSpec(memory_space=pl.ANY)],
        out_specs=pl.BlockSpec((tm, D_MODEL), row),
        out_shape=jax.ShapeDtypeStruct((T, D_MODEL), F32),
        scratch_shapes=[pltpu.VMEM((2, TOP_K, tm, HALF), U32), pltpu.SemaphoreType.DMA((2,))],
        compiler_params=_params(("arbitrary",)),
        name="moe_combine",
    )(dest3, dest3, w_tok, h, wgs, wus, wds, ln_g.reshape(1, D_MODEL), ln_b.reshape(1, D_MODEL), ys3)


def _tile(n, pref):
    t = min(n, pref)
    assert n % t == 0, (n, pref)
    return t


def _layer(x, w_in, conv_w, conv_b, lru_wa, lru_ba, lru_wi, lru_bi, lru_lambda, ret_gn_gain,
           w_lru_out, w_ret_out, b_gate, w_o, ln1_g, ln1_b, w_router, router_bias,
           w_gate_e, w_up_e, w_down_e, w_gate_s, w_up_s, w_down_s, ln2_g, ln2_b):
    B, S, D = x.shape
    T = B * S
    x2 = x.reshape(T, D)
    xb = x2.astype(BF16)

    half = RET_DK // 2
    freq = ROPE_THETA ** (-jnp.arange(half, dtype=F32) / half)
    ang = jnp.arange(S, dtype=I32).astype(F32)[:, None] * freq
    cos, sin = jnp.cos(ang), jnp.sin(ang)

    ua, v, q, k = _lru_branch(xb, w_in, cos, sin, conv_w, conv_b, lru_wa, lru_ba, lru_wi, lru_bi,
                              lru_lambda, S, _tile(S, 256), 512)
    ub = _retention(q, k, v, xb, w_in, ret_gn_gain, B, S)

    mixed = _mix(ua, ub, xb, w_lru_out.astype(BF16), w_ret_out.astype(BF16),
                 w_in[:, OFF_GA:].astype(BF16), b_gate, _tile(T, 512), 512)

    tm_r = _tile(T, 256)
    cap = (T + MOE_ROWS - 1) // MOE_ROWS * MOE_ROWS
    h, dest, wsel, cnt, xs = _ln1_route(mixed, w_o.astype(BF16), x2, ln1_g, ln1_b,
                                        w_router.T.astype(BF16), router_bias, tm_r, cap)

    counts = cnt[:, 0].astype(I32)
    e_blocks = (counts + MOE_ROWS - 1) // MOE_ROWS
    blocks_end = jnp.cumsum(e_blocks)
    nb = (T * TOP_K + N_EXPERTS * (MOE_ROWS - 1) + MOE_ROWS - 1) // MOE_ROWS
    nb_used = blocks_end[-1:]
    blk = jnp.minimum(jnp.arange(nb, dtype=I32), nb_used - 1)
    block_e = jnp.minimum(jnp.sum((blocks_end[None, :] <= blk[:, None]).astype(I32), axis=1),
                          N_EXPERTS - 1)
    blocks_start = blocks_end - e_blocks
    e_onehot = block_e[:, None] == jnp.arange(N_EXPERTS, dtype=I32)[None, :]
    block_row = block_e * (cap // MOE_ROWS) + blk - jnp.sum(jnp.where(e_onehot, blocks_start[None, :], 0),
                                                            axis=1)
    e_row0 = jnp.arange(N_EXPERTS, dtype=I32) * cap
    seg_end = jnp.sum(jnp.where(e_onehot, blocks_end[None, :], 0), axis=1)
    after = jnp.minimum(seg_end, nb_used - 1)
    next_e = jnp.sum(jnp.where(after[:, None] == jnp.arange(nb, dtype=I32)[None, :], block_e[None, :], 0),
                     axis=1)

    xs = _pad_fill(e_row0 + counts, e_row0 + e_blocks * MOE_ROWS, xs)
    ys = _experts(block_e, nb_used, block_row, next_e, xs, w_gate_e, w_up_e, w_down_e)
    out = _combine(dest, wsel.T, h, w_gate_s.astype(BF16), w_up_s.astype(BF16),
                   w_down_s.astype(BF16), ln2_g, ln2_b, ys)
    return out.reshape(B, S, D)


def kernel(x, w_in, conv_w, conv_b, lru_wa, lru_ba, lru_wi, lru_bi, lru_lambda, ret_gn_gain, w_lru_out, w_ret_out, b_gate, w_o, ln1_g, ln1_b, w_router, router_bias, w_gate_e, w_up_e, w_down_e, w_gate_s, w_up_s, w_down_s, ln2_g, ln2_b):
    assert DEPTH == 1 and w_in.shape[0] == DEPTH
    args = (w_in, conv_w, conv_b, lru_wa, lru_ba, lru_wi, lru_bi, lru_lambda, ret_gn_gain,
            w_lru_out, w_ret_out, b_gate, w_o, ln1_g, ln1_b, w_router, router_bias,
            w_gate_e, w_up_e, w_down_e, w_gate_s, w_up_s, w_down_s, ln2_g, ln2_b)
    return _layer(x, *[a[0] for a in args])
```
